```python
import math
import jax, jax.numpy as jnp
from jax import lax
import numpy as np

D_MODEL = 1024
BATCH = 8
SEQ = 4096
DEPTH = 2
DEC_BATCH = 32
DEC_SEQ = 2048
PAST_LEN = 128

ATT_HEADS = 4
ATT_HEAD_DIM = 64
ATT_V_DIM = 2 * ATT_HEAD_DIM
ATT_WIDTH = ATT_HEADS * 2 * ATT_HEAD_DIM
Q_BLOCK = 128
HY_WIDTH = D_MODEL // 2
HY_EMB = 33
HY_BANDS = (HY_EMB - 1) // 2
HY_FILTER_WIDTH = 64
HY_INNER = 2
HY_TARGET = 1e-2
HY_FAST_PCT = 0.3
HY_SLOW_PCT = 1.5
SC_WIDTH = D_MODEL // 2
CONV_WIDTH = 3
D_FF = 2816
N_EXPERTS = 8
TOP_K = 2
D_FF_EXPERT = 3584
N_DENSE = (DEPTH + 1) // 2
N_MOE = DEPTH // 2
ALPHA = (2 * DEPTH) ** 0.25
BETA = (8 * DEPTH) ** -0.25
LN_EPS = 1e-5
RMS_EPS = 1e-5
D_IN = 3 * ATT_WIDTH + 3 * HY_WIDTH + 3 * SC_WIDTH + 3 * D_MODEL
SPLIT_POINTS = (ATT_WIDTH, 2 * ATT_WIDTH, 3 * ATT_WIDTH,
                3 * ATT_WIDTH + 3 * HY_WIDTH,
                3 * ATT_WIDTH + 3 * HY_WIDTH + 3 * SC_WIDTH)

kernel_name = 'hybrid_diffattn_hyena_shortconv_encoder'


def layer_norm(x, g, b):
    xf = x.astype(jnp.float32)
    mu = jnp.mean(xf, -1, keepdims=True)
    var = jnp.mean(jnp.square(xf - mu), -1, keepdims=True)
    return ((xf - mu) * lax.rsqrt(var + LN_EPS) * g + b).astype(x.dtype)


def conv3_centred(x, w):
    xp = jnp.pad(x, ((0, 0), (1, 1), (0, 0)))
    return w[0] * xp[:, :-2] + w[1] * xp[:, 1:-1] + w[2] * xp[:, 2:]


def alibi_slopes(n):
    return 2.0 ** (-8.0 * np.arange(1, n + 1, dtype=np.float32) / n)


def diff_attention(q, k, v, lam, subln_g, layer):
    bsz, L = q.shape[0], q.shape[1]
    lam_init = 0.8 - 0.6 * math.exp(-0.3 * layer)
    lf = lam.astype(jnp.float32)
    lam_full = jnp.exp(jnp.sum(lf[0] * lf[1])) - jnp.exp(jnp.sum(lf[2] * lf[3])) + lam_init
    slopes = jnp.asarray(alibi_slopes(ATT_HEADS))
    k_pos = jnp.arange(L, dtype=jnp.float32)
    nb = L // Q_BLOCK
    qs = q * (ATT_HEAD_DIM ** -0.5)
    qb = qs.reshape(bsz, nb, Q_BLOCK, ATT_HEADS, 2, ATT_HEAD_DIM).transpose(1, 0, 2, 3, 4, 5)

    def block(args):
        q_blk, bi = args
        s = jnp.einsum('bqhmd,bkhmd->bhmqk', q_blk, k).astype(jnp.float32)
        q_pos = (bi * Q_BLOCK + jnp.arange(Q_BLOCK)).astype(jnp.float32)
        bias = -slopes[:, None, None] * jnp.abs(q_pos[:, None] - k_pos[None, :])
        p = jax.nn.softmax(s + bias[None, :, None], axis=-1)
        a = (p[:, :, 0] - lam_full * p[:, :, 1]).astype(v.dtype)
        return jnp.einsum('bhqk,bkhe->bqhe', a, v)

    o = lax.map(block, (qb, jnp.arange(nb)))
    o = o.transpose(1, 0, 2, 3, 4).reshape(bsz, L, ATT_HEADS, ATT_V_DIM)
    of = o.astype(jnp.float32)
    of = of * lax.rsqrt(jnp.mean(of * of, -1, keepdims=True) + RMS_EPS) * subln_g
    return (of * (1.0 - lam_init)).astype(v.dtype).reshape(bsz, L, ATT_HEADS * ATT_V_DIM)


def hyena_filter(L, w_in, b_in, w_mid, b_mid, freq, w_out):
    f32 = jnp.float32
    t = jnp.linspace(0.0, 1.0, L, dtype=f32)[:, None]
    w = 2.0 * math.pi * jnp.arange(L, dtype=f32)[:, None] / L
    f = jnp.linspace(1e-4, HY_BANDS - 1, HY_BANDS, dtype=f32)[None]
    z = jnp.concatenate([t, jnp.cos(f * w), -jnp.sin(f * w)], -1)
    fr = freq.astype(f32)
    h = jnp.sin(fr * (z @ w_in.astype(f32) + b_in.astype(f32)))
    for j in range(HY_INNER):
        h = jnp.sin(fr * (h @ w_mid[j].astype(f32) + b_mid[j].astype(f32)))
    h = (h @ w_out.astype(f32)).reshape(L, 2, HY_WIDTH)
    max_decay = math.log(HY_TARGET) / HY_FAST_PCT
    min_decay = math.log(HY_TARGET) / HY_SLOW_PCT
    deltas = jnp.abs(jnp.linspace(min_decay, max_decay, HY_WIDTH, dtype=f32))
    h = h * jnp.exp(-t[:, :, None] * deltas)
    h_fwd, h_bwd = h[:, 0], h[:, 1]
    l1 = jnp.sum(jnp.abs(h_fwd), 0) + jnp.sum(jnp.abs(h_bwd[1:]), 0)
    kern = jnp.concatenate([h_fwd, jnp.zeros((1, HY_WIDTH), f32), h_bwd[:0:-1]], 0)
    return kern / l1


def bidir_long_conv(u, kern, bias):
    L = u.shape[1]
    uf = u.astype(jnp.float32)
    U = jnp.fft.rfft(uf, n=2 * L, axis=1)
    K = jnp.fft.rfft(kern, n=2 * L, axis=0)
    y = jnp.fft.irfft(U * K[None], n=2 * L, axis=1)[:, :L]
    return (y + uf * bias.astype(jnp.float32)).astype(u.dtype)


def token_mixers(x, i, p):
    bsz, L, _ = x.shape
    proj = x @ p['w_in'][i]
    q, k, v, hy, sc, gates = jnp.split(proj, SPLIT_POINTS, axis=-1)
    q = q.reshape(bsz, L, ATT_HEADS, 2, ATT_HEAD_DIM)
    k = k.reshape(bsz, L, ATT_HEADS, 2, ATT_HEAD_DIM)
    v = v.reshape(bsz, L, ATT_HEADS, ATT_V_DIM)
    y_a = diff_attention(q, k, v, p['att_lambda'][i], p['att_subln_g'][i], i)
    x0, x1, hv = jnp.split(conv3_centred(hy, p['hy_short_w'][i]), 3, axis=-1)
    kern = hyena_filter(L, p['hf_w_in'][i], p['hf_b_in'][i], p['hf_w_mid'][i],
                        p['hf_b_mid'][i], p['hf_freq'][i], p['hf_w_out'][i])
    y_h = x0 * bidir_long_conv(hv * x1, kern, p['hy_bias'][i])
    b_g, c_g, xs = jnp.split(sc, 3, axis=-1)
    y_c = b_g * conv3_centred(c_g * xs, p['sc_conv_w'][i])
    g_a, g_h, g_c = jnp.split(jax.nn.sigmoid(gates), 3, axis=-1)
    wb = p['w_branch'][i]
    merged = g_a * (y_a @ wb[0]) + g_h * (y_h @ wb[1]) + g_c * (y_c @ wb[2])
    return merged @ p['w_out'][i]


def swiglu(x, wg, wu, wd):
    return (jax.nn.silu(x @ wg) * (x @ wu)) @ wd


def moe_swiglu(x, router, wg, wu, wd):
    logits = (x @ router).astype(jnp.float32)
    top_v, top_i = lax.top_k(logits, TOP_K)
    probs = jax.nn.softmax(top_v, axis=-1)
    comb = jnp.sum(jax.nn.one_hot(top_i, N_EXPERTS, dtype=jnp.float32) * probs[..., None], axis=-2)
    comb = comb.astype(x.dtype)
    out = jnp.zeros_like(x)
    for e in range(N_EXPERTS):
        out = out + comb[..., e:e + 1] * swiglu(x, wg[e], wu[e], wd[e])
    return out


def run_trunk(x, p):
    for i in range(DEPTH):
        x = layer_norm(ALPHA * x + token_mixers(x, i, p), p['ln1_g'][i], p['ln1_b'][i])
        if i % 2 == 0:
            j = i // 2
            f = swiglu(x, p['ffn_w_gate'][j], p['ffn_w_up'][j], p['ffn_w_down'][j])
        else:
            j = i // 2
            f = moe_swiglu(x, p['router_w'][j], p['moe_w_gate'][j], p['moe_w_up'][j], p['moe_w_down'][j])
        x = layer_norm(ALPHA * x + f, p['ln2_g'][i], p['ln2_b'][i])
    return x


def setup_inputs(seed: int = 0) -> dict:
    key = jax.random.key(seed)
    ks = jax.random.split(key, 32)
    nrm = lambda k, shape, s: jax.random.normal(k, shape, jnp.float32) * s
    return {
        'x_prompt': nrm(ks[0], (BATCH, SEQ, D_MODEL), 1.0),
        'x_sample': nrm(ks[1], (DEC_BATCH, DEC_SEQ, D_MODEL), 1.0),
        'w_in': nrm(ks[2], (DEPTH, D_MODEL, D_IN), D_MODEL ** -0.5),
        'att_lambda': nrm(ks[3], (DEPTH, 4, ATT_HEAD_DIM), 0.1),
        'att_subln_g': 1.0 + nrm(ks[4], (DEPTH, ATT_V_DIM), 0.02),
        'hy_short_w': nrm(ks[5], (DEPTH, CONV_WIDTH, 3 * HY_WIDTH), CONV_WIDTH ** -0.5),
        'hf_w_in': nrm(ks[6], (DEPTH, HY_EMB, HY_FILTER_WIDTH), HY_EMB ** -0.5),
        'hf_b_in': nrm(ks[7], (DEPTH, HY_FILTER_WIDTH), 0.1),
        'hf_w_mid': nrm(ks[8], (DEPTH, HY_INNER, HY_FILTER_WIDTH, HY_FILTER_WIDTH), HY_FILTER_WIDTH ** -0.5),
        'hf_b_mid': nrm(ks[9], (DEPTH, HY_INNER, HY_FILTER_WIDTH), 0.1),
        'hf_freq': 1.0 + nrm(ks[10], (DEPTH, HY_FILTER_WIDTH), 0.02),
        'hf_w_out': nrm(ks[11], (DEPTH, HY_FILTER_WIDTH, 2 * HY_WIDTH), HY_FILTER_WIDTH ** -0.5),
        'hy_bias': nrm(ks[12], (DEPTH, HY_WIDTH), 1.0),
        'sc_conv_w': nrm(ks[13], (DEPTH, CONV_WIDTH, SC_WIDTH), CONV_WIDTH ** -0.5),
        'w_branch': nrm(ks[14], (DEPTH, 3, ATT_WIDTH, D_MODEL), ATT_WIDTH ** -0.5),
        'w_out': nrm(ks[15], (DEPTH, D_MODEL, D_MODEL), BETA * D_MODEL ** -0.5),
        'ln1_g': 1.0 + nrm(ks[16], (DEPTH, D_MODEL), 0.02),
        'ln1_b': nrm(ks[17], (DEPTH, D_MODEL), 0.02),
        'ln2_g': 1.0 + nrm(ks[18], (DEPTH, D_MODEL), 0.02),
        'ln2_b': nrm(ks[19], (DEPTH, D_MODEL), 0.02),
        'ffn_w_gate': nrm(ks[20], (N_DENSE, D_MODEL, D_FF), D_MODEL ** -0.5),
        'ffn_w_up': nrm(ks[21], (N_DENSE, D_MODEL, D_FF), D_MODEL ** -0.5),
        'ffn_w_down': nrm(ks[22], (N_DENSE, D_FF, D_MODEL), BETA * D_FF ** -0.5),
        'router_w': nrm(ks[23], (N_MOE, D_MODEL, N_EXPERTS), D_MODEL ** -0.5),
        'moe_w_gate': nrm(ks[24], (N_MOE, N_EXPERTS, D_MODEL, D_FF_EXPERT), D_MODEL ** -0.5),
        'moe_w_up': nrm(ks[25], (N_MOE, N_EXPERTS, D_MODEL, D_FF_EXPERT), D_MODEL ** -0.5),
        'moe_w_down': nrm(ks[26], (N_MOE, N_EXPERTS, D_FF_EXPERT, D_MODEL), BETA * D_FF_EXPERT ** -0.5),
    }


def reference(x_prompt, x_sample, w_in, att_lambda, att_subln_g, hy_short_w, hf_w_in, hf_b_in,
              hf_w_mid, hf_b_mid, hf_freq, hf_w_out, hy_bias, sc_conv_w, w_branch, w_out,
              ln1_g, ln1_b, ln2_g, ln2_b, ffn_w_gate, ffn_w_up, ffn_w_down, router_w,
              moe_w_gate, moe_w_up, moe_w_down):
    p = dict(w_in=w_in, att_lambda=att_lambda, att_subln_g=att_subln_g, hy_short_w=hy_short_w,
             hf_w_in=hf_w_in, hf_b_in=hf_b_in, hf_w_mid=hf_w_mid, hf_b_mid=hf_b_mid,
             hf_freq=hf_freq, hf_w_out=hf_w_out, hy_bias=hy_bias, sc_conv_w=sc_conv_w,
             w_branch=w_branch, w_out=w_out, ln1_g=ln1_g, ln1_b=ln1_b, ln2_g=ln2_g, ln2_b=ln2_b,
             ffn_w_gate=ffn_w_gate, ffn_w_up=ffn_w_up, ffn_w_down=ffn_w_down, router_w=router_w,
             moe_w_gate=moe_w_gate, moe_w_up=moe_w_up, moe_w_down=moe_w_down)
    y_prompt = run_trunk(x_prompt, p)
    y_sample = run_trunk(x_sample, p)
    return (y_prompt, y_sample)
```

```python
import functools
import math

import numpy as np
import jax
import jax.numpy as jnp
from jax import lax
from jax.experimental import pallas as pl
from jax.experimental.pallas import tpu as pltpu

F32 = jnp.float32
BF16 = jnp.bfloat16

ATT_HEADS = 4
ATT_HEAD_DIM = 64
HEAD_W = 2 * ATT_HEAD_DIM
HY_BANDS = 16
HY_TARGET = 1e-2
HY_FAST_PCT = 0.3
HY_SLOW_PCT = 1.5
TOP_K = 2
LN_EPS = 1e-5
RMS_EPS = 1e-5

V7X_LANES = 128
V7X_BF16_SUBLANES = 16
V7X_VMEM_LIMIT = 56 * 1024 * 1024


def _cparams(*sem):
    return pltpu.CompilerParams(dimension_semantics=sem, vmem_limit_bytes=V7X_VMEM_LIMIT)


def _tile(n, pref):
    t = min(n, pref)
    while n % t:
        t //= 2
    return t


def _layer_norm(v, g, b):
    mu = jnp.mean(v, axis=-1, keepdims=True)
    d = v - mu
    var = jnp.mean(d * d, axis=-1, keepdims=True)
    return d * lax.rsqrt(var + LN_EPS) * g + b


def _proj_kernel(x_ref, w_ref, o_ref, xb_ref):
    @pl.when(pl.program_id(1) == 0)
    def _():
        xb_ref[...] = x_ref[...].astype(BF16)

    o_ref[...] = jnp.dot(xb_ref[...], w_ref[...], preferred_element_type=F32).astype(o_ref.dtype)


def _proj(x, w):
    t, d = x.shape
    n = w.shape[1]
    tm = _tile(t, 1024)
    tn = _tile(n, 1536)
    return pl.pallas_call(
        _proj_kernel,
        grid=(t // tm, n // tn),
        in_specs=[pl.BlockSpec((tm, d), lambda i, j: (i, 0)),
                  pl.BlockSpec((d, tn), lambda i, j: (0, j))],
        out_specs=pl.BlockSpec((tm, tn), lambda i, j: (i, j)),
        out_shape=jax.ShapeDtypeStruct((t, n), BF16),
        scratch_shapes=[pltpu.VMEM((tm, d), BF16)],
        compiler_params=_cparams("parallel", "arbitrary"),
        name="proj",
    )(x, w)


def _attn_kernel(slopes_ref, lam_ref, g_ref, q_ref, k_ref, v_ref, o_ref, *, lam_init, tq):
    h = pl.program_id(1)
    qi = pl.program_id(2)
    seq = k_ref.shape[1]
    slope = slopes_ref[h]
    lf = lam_ref[...].astype(F32)
    lam_full = (jnp.exp(jnp.sum(lf[0:1] * lf[1:2], axis=1, keepdims=True))
                - jnp.exp(jnp.sum(lf[2:3] * lf[3:4], axis=1, keepdims=True)) + lam_init)
    q = q_ref[0] * (ATT_HEAD_DIM ** -0.5)
    k = k_ref[0]
    v = v_ref[0]
    lane = lax.broadcasted_iota(jnp.int32, q.shape, 1)
    zero = jnp.zeros_like(q)
    q_pos = qi * tq + lax.broadcasted_iota(jnp.int32, (tq, seq), 0)
    k_pos = lax.broadcasted_iota(jnp.int32, (tq, seq), 1)
    bias = -slope * jnp.abs(q_pos - k_pos).astype(F32)

    def softmax_pv(qm):
        s = lax.dot_general(qm, k, (((1,), (1,)), ((), ())), preferred_element_type=F32) + bias
        m = jnp.max(s, axis=1, keepdims=True)
        e = jnp.exp(s - m)
        l = jnp.sum(e, axis=1, keepdims=True)
        return jnp.dot(e.astype(BF16), v, preferred_element_type=F32) / l

    o = (softmax_pv(jnp.where(lane < ATT_HEAD_DIM, q, zero))
         - lam_full * softmax_pv(jnp.where(lane >= ATT_HEAD_DIM, q, zero)))
    o = o * lax.rsqrt(jnp.mean(o * o, axis=-1, keepdims=True) + RMS_EPS) * g_ref[...].astype(F32)
    o_ref[0] = (o * (1.0 - lam_init)).astype(o_ref.dtype)


def _attention(proj3, lam, subln_g, layer):
    bsz, seq, _ = proj3.shape
    tq = _tile(seq, 256)
    lam_init = 0.8 - 0.6 * math.exp(-0.3 * layer)
    slopes = jnp.asarray(2.0 ** (-8.0 * np.arange(1, ATT_HEADS + 1, dtype=np.float32) / ATT_HEADS))
    kern = functools.partial(_attn_kernel, lam_init=lam_init, tq=tq)
    return pl.pallas_call(
        kern,
        grid=(bsz, ATT_HEADS, seq // tq),
        in_specs=[pl.BlockSpec(memory_space=pltpu.SMEM),
                  pl.BlockSpec((4, ATT_HEAD_DIM), lambda b, h, i: (0, 0)),
                  pl.BlockSpec((1, HEAD_W), lambda b, h, i: (0, 0)),
                  pl.BlockSpec((1, tq, HEAD_W), lambda b, h, i: (b, i, h)),
                  pl.BlockSpec((1, seq, HEAD_W), lambda b, h, i: (b, 0, ATT_HEADS + h)),
                  pl.BlockSpec((1, seq, HEAD_W), lambda b, h, i: (b, 0, 2 * ATT_HEADS + h))],
        out_specs=pl.BlockSpec((1, tq, HEAD_W), lambda b, h, i: (b, i, h)),
        out_shape=jax.ShapeDtypeStruct((bsz, seq, ATT_HEADS * HEAD_W), BF16),
        compiler_params=_cparams("parallel", "parallel", "arbitrary"),
        name="attn",
    )(slopes, lam, subln_g.reshape(1, HEAD_W), proj3, proj3, proj3)


def _conv3(x, prev_row, next_row, w):
    rows = x.shape[0]
    row = lax.broadcasted_iota(jnp.int32, (rows, 1), 0)
    x_prev = jnp.where(row == 0, prev_row, pltpu.roll(x, 1, axis=0))
    x_next = jnp.where(row == rows - 1, next_row, pltpu.roll(x, rows - 1, axis=0))
    return w[0:1] * x_prev + w[1:2] * x + w[2:3] * x_next


def _conv_kernel(hy_ref, hyp_ref, hyn_ref, sc_ref, scp_ref, scn_ref, hw_ref, sw_ref,
                 u_ref, x0_ref, yc_ref):
    i = pl.program_id(1)
    last = pl.num_programs(1) - 1
    width = u_ref.shape[2]
    has_prev = jnp.where(i > 0, 1.0, 0.0)
    has_next = jnp.where(i < last, 1.0, 0.0)
    hsub = hyp_ref.shape[1]

    hy = hy_ref[0].astype(F32)
    hy_prev = hyp_ref[0].astype(F32)[hsub - 1:hsub] * has_prev
    hy_next = hyn_ref[0].astype(F32)[0:1] * has_next
    hyc = _conv3(hy, hy_prev, hy_next, hw_ref[...].astype(F32))
    x0_ref[0] = hyc[:, :width].astype(x0_ref.dtype)
    u_ref[0] = (hyc[:, 2 * width:] * hyc[:, width:2 * width]).astype(u_ref.dtype)

    def gated(blk):
        return blk[:, width:2 * width] * blk[:, 2 * width:]

    sc = sc_ref[0].astype(F32)
    cx_prev = gated(scp_ref[0].astype(F32)[hsub - 1:hsub]) * has_prev
    cx_next = gated(scn_ref[0].astype(F32)[0:1]) * has_next
    conv = _conv3(gated(sc), cx_prev, cx_next, sw_ref[...].astype(F32))
    yc_ref[0] = (sc[:, :width] * conv).astype(yc_ref.dtype)


def _convs(proj3, hy_w, sc_w, hy_col, sc_col):
    bsz, seq, _ = proj3.shape
    width = sc_w.shape[1]
    tl = _tile(seq, 512)
    hs = V7X_BF16_SUBLANES
    nh = tl // hs
    nblk = seq // hs

    def main(col):
        return pl.BlockSpec((1, tl, 3 * width), lambda b, i: (b, i, col))

    def prev(col):
        return pl.BlockSpec((1, hs, 3 * width), lambda b, i: (b, jnp.maximum(i * nh - 1, 0), col))

    def nxt(col):
        return pl.BlockSpec((1, hs, 3 * width), lambda b, i: (b, jnp.minimum((i + 1) * nh, nblk - 1), col))

    out = jax.ShapeDtypeStruct((bsz, seq, width), BF16)
    ospec = pl.BlockSpec((1, tl, width), lambda b, i: (b, i, 0))
    return pl.pallas_call(
        _conv_kernel,
        grid=(bsz, seq // tl),
        in_specs=[main(hy_col), prev(hy_col), nxt(hy_col), main(sc_col), prev(sc_col), nxt(sc_col),
                  pl.BlockSpec((3, 3 * width), lambda b, i: (0, 0)),
                  pl.BlockSpec((3, width), lambda b, i: (0, 0))],
        out_specs=[ospec, ospec, ospec],
        out_shape=[out, out, out],
        compiler_params=_cparams("parallel", "arbitrary"),
        name="conv",
    )(proj3, proj3, proj3, proj3, proj3, proj3, hy_w, sc_w)


def _tab_kernel(c_ref, s_ref, st_ref, *, seq):
    tr = c_ref.shape[0]
    i = pl.program_id(0)
    r = i * tr + lax.broadcasted_iota(jnp.int32, (tr, seq), 0)
    c = lax.broadcasted_iota(jnp.int32, (tr, seq), 1)
    ang = ((r * c) & (2 * seq - 1)).astype(F32) * (math.pi / seq)
    cosv = jnp.cos(ang)
    nsin = -jnp.sin(ang)
    c_ref[...] = cosv.astype(c_ref.dtype)
    s_ref[...] = jnp.where(r == 0, jnp.where((c & 1) == 0, 1.0, -1.0), nsin).astype(s_ref.dtype)
    st_ref[...] = jnp.where(c == 0, jnp.where((r & 1) == 0, 1.0, -1.0), nsin).astype(st_ref.dtype)


def _dft_tables(seq):
    assert seq & (seq - 1) == 0, "sequence length must be a power of two"
    tr = _tile(seq, 256)
    spec = pl.BlockSpec((tr, seq), lambda i: (i, 0))
    out = jax.ShapeDtypeStruct((seq, seq), BF16)
    return pl.pallas_call(
        functools.partial(_tab_kernel, seq=seq),
        grid=(seq // tr,),
        out_specs=[spec, spec, spec],
        out_shape=[out, out, out],
        compiler_params=_cparams("parallel"),
        name="dft_tables",
    )()


def _filt_kernel(z_ref, win_ref, bin_ref, wmid_ref, bmid_ref, fr_ref, wout_ref, dl_ref,
                 hs_ref, l1_ref, *, seq):
    i = pl.program_id(0)
    tl = z_ref.shape[0]
    width = dl_ref.shape[1]
    hi = lax.Precision.HIGHEST
    fr = fr_ref[...]
    h = jnp.sin(fr * (jnp.dot(z_ref[...], win_ref[...], precision=hi, preferred_element_type=F32)
                      + bin_ref[...]))
    for j in range(wmid_ref.shape[0]):
        h = jnp.sin(fr * (jnp.dot(h, wmid_ref[j], precision=hi, preferred_element_type=F32)
                          + bmid_ref[j]))
    ho = jnp.dot(h, wout_ref[...], precision=hi, preferred_element_type=F32)
    pos = i * tl + lax.broadcasted_iota(jnp.int32, (tl, 1), 0)
    t = pos.astype(F32) / float(seq - 1)
    window = jnp.exp(-t * dl_ref[...])
    h_fwd = ho[:, :width] * window
    h_bwd = jnp.where(pos == 0, 0.0, ho[:, width:] * window)
    hs_ref[:, :width] = h_fwd + h_bwd
    hs_ref[:, width:] = h_fwd - h_bwd

    @pl.when(i == 0)
    def _():
        l1_ref[...] = jnp.zeros_like(l1_ref)

    l1_ref[...] += jnp.sum(jnp.abs(h_fwd) + jnp.abs(h_bwd), axis=0, keepdims=True)


def _fspec_kernel(c_ref, s_ref, hs_ref, l1_ref, a_ref, b_ref, d_ref, *, seq):
    i = pl.program_id(0)
    tk = c_ref.shape[0]
    width = l1_ref.shape[1]
    hs = hs_ref[...].astype(BF16)
    xc = jnp.dot(c_ref[...], hs[:, :width], preferred_element_type=F32)
    xs = jnp.dot(s_ref[...], hs, preferred_element_type=F32)
    k = i * tk + lax.broadcasted_iota(jnp.int32, (tk, 1), 0)
    inv_l1 = 1.0 / l1_ref[...]
    wk = jnp.where(k == 0, 1.0, 2.0) / float(2 * seq)
    a = xc * wk * inv_l1
    a_ref[...] = a
    b_ref[...] = jnp.where(k == 0, 0.0, xs[:, width:] * wk * inv_l1)
    d_ref[...] = jnp.where(k == 0, xs[:, :width] * wk * inv_l1, a)


def _hyena_filter_spectrum(seq, tabs, w_in, b_in, w_mid, b_mid, freq, w_out):
    c_tab, s_tab, _ = tabs
    emb, fw = w_in.shape
    width = w_out.shape[1] // 2
    n = np.arange(seq, dtype=np.float64)[:, None]
    f = np.linspace(1e-4, HY_BANDS - 1, HY_BANDS)[None]
    w = 2.0 * math.pi * n / seq
    z = np.concatenate([n / (seq - 1), np.cos(f * w), -np.sin(f * w)], -1)
    z = np.pad(z, ((0, 0), (0, V7X_LANES - emb))).astype(np.float32)
    w_in_p = jnp.pad(w_in.astype(F32), ((0, V7X_LANES - emb), (0, 0)))
    deltas = np.abs(np.linspace(math.log(HY_TARGET) / HY_SLOW_PCT, math.log(HY_TARGET) / HY_FAST_PCT,
                                width)).astype(np.float32)[None]
    tl = _tile(seq, 512)
    full = lambda *shape: pl.BlockSpec(shape, lambda i: (0,) * len(shape))
    hs, l1 = pl.pallas_call(
        functools.partial(_filt_kernel, seq=seq),
        grid=(seq // tl,),
        in_specs=[pl.BlockSpec((tl, V7X_LANES), lambda i: (i, 0)),
                  full(V7X_LANES, fw), full(1, fw), full(*w_mid.shape), full(w_mid.shape[0], 1, fw),
                  full(1, fw), full(fw, 2 * width), full(1, width)],
        out_specs=[pl.BlockSpec((tl, 2 * width), lambda i: (i, 0)), full(1, width)],
        out_shape=[jax.ShapeDtypeStruct((seq, 2 * width), F32), jax.ShapeDtypeStruct((1, width), F32)],
        compiler_params=_cparams("arbitrary"),
        name="hyena_filter",
    )(jnp.asarray(z), w_in_p, b_in.reshape(1, fw).astype(F32), w_mid.astype(F32),
      b_mid.reshape(w_mid.shape[0], 1, fw).astype(F32), freq.reshape(1, fw).astype(F32),
      w_out.astype(F32), jnp.asarray(deltas))

    tk = _tile(seq, 256)
    ospec = pl.BlockSpec((tk, width), lambda i: (i, 0))
    out = jax.ShapeDtypeStruct((seq, width), F32)
    return pl.pallas_call(
        functools.partial(_fspec_kernel, seq=seq),
        grid=(seq // tk,),
        in_specs=[pl.BlockSpec((tk, seq), lambda i: (i, 0)), pl.BlockSpec((tk, seq), lambda i: (i, 0)),
                  full(seq, 2 * width), full(1, width)],
        out_specs=[ospec, ospec, ospec],
        out_shape=[out, out, out],
        compiler_params=_cparams("parallel"),
        name="filter_spectrum",
    )(c_tab, s_tab, hs, l1)


def _dft_fwd_kernel(c_ref, s_ref, u_ref, a_ref, b_ref, d_ref, yre_ref, yim_ref):
    u = u_ref[0]
    x_re = jnp.dot(c_ref[...], u, preferred_element_type=F32)
    x_im = jnp.dot(s_ref[...], u, preferred_element_type=F32)
    b = b_ref[...]
    yre_ref[0] = (x_re * a_ref[...] - x_im * b).astype(yre_ref.dtype)
    yim_ref[0] = (x_re * b + x_im * d_ref[...]).astype(yim_ref.dtype)


def _dft_inv_kernel(c_ref, st_ref, yre_ref, yim_ref, u_ref, x0_ref, bias_ref, o_ref):
    y = (jnp.dot(c_ref[...], yre_ref[0], preferred_element_type=F32)
         + jnp.dot(st_ref[...], yim_ref[0], preferred_element_type=F32))
    y = y + u_ref[0].astype(F32) * bias_ref[...]
    o_ref[0] = (x0_ref[0].astype(F32) * y).astype(o_ref.dtype)


def _long_conv(u, x0, tabs, spec, bias):
    c_tab, s_tab, st_tab = tabs
    a, b, d = spec
    bsz, seq, width = u.shape
    tk = _tile(seq, 512)
    tab = pl.BlockSpec((tk, seq), lambda j, bb: (j, 0))
    whole = pl.BlockSpec((1, seq, width), lambda j, bb: (bb, 0, 0))
    rows = pl.BlockSpec((1, tk, width), lambda j, bb: (bb, j, 0))
    coef = pl.BlockSpec((tk, width), lambda j, bb: (j, 0))
    spec_out = jax.ShapeDtypeStruct((bsz, seq, width), BF16)
    y_re, y_im = pl.pallas_call(
        _dft_fwd_kernel,
        grid=(seq // tk, bsz),
        in_specs=[tab, tab, whole, coef, coef, coef],
        out_specs=[rows, rows],
        out_shape=[spec_out, spec_out],
        compiler_params=_cparams("parallel", "arbitrary"),
        name="dft_fwd",
    )(c_tab, s_tab, u, a, b, d)
    return pl.pallas_call(
        _dft_inv_kernel,
        grid=(seq // tk, bsz),
        in_specs=[tab, tab, whole, whole, rows, rows, pl.BlockSpec((1, width), lambda j, bb: (0, 0))],
        out_specs=rows,
        out_shape=spec_out,
        compiler_params=_cparams("parallel", "arbitrary"),
        name="dft_inv",
    )(c_tab, st_tab, y_re, y_im, u, x0, bias.reshape(1, width).astype(F32))


def _merge_kernel(ya_ref, yh_ref, yc_ref, g0_ref, g1_ref, x_ref, wb_ref, wo_ref, lg_ref, lb_ref,
                  o_ref, m_ref, *, alpha):
    d = o_ref.shape[1]
    half = d // 2
    g0 = g0_ref[...]
    g1 = g1_ref[...]
    gates = ((g0[:, :half], g0[:, d:], g1[:, half:d]),
             (g0[:, half:d], g1[:, :half], g1[:, d:]))
    branches = (ya_ref[...], yh_ref[...], yc_ref[...])
    for c in range(2):
        acc = None
        for j in range(3):
            y = jnp.dot(branches[j], wb_ref[j, :, c * half:(c + 1) * half], preferred_element_type=F32)
            term = jax.nn.sigmoid(gates[c][j].astype(F32)) * y
            acc = term if acc is None else acc + term
        m_ref[:, c * half:(c + 1) * half] = acc.astype(m_ref.dtype)
    t = jnp.dot(m_ref[...], wo_ref[...], preferred_element_type=F32)
    o_ref[...] = _layer_norm(alpha * x_ref[...] + t, lg_ref[...], lb_ref[...])


def _merge(y_a, y_h, y_c, proj, x, wb, wo, ln_g, ln_b, alpha, gate_col):
    t, d = x.shape
    bw = y_a.shape[1]
    tm = _tile(t, 512)
    row = lambda w: pl.BlockSpec((tm, w), lambda i: (i, 0))
    gw = 3 * d // 2
    return pl.pallas_call(
        functools.partial(_merge_kernel, alpha=alpha),
        grid=(t // tm,),
        in_specs=[row(bw), row(bw), row(bw),
                  pl.BlockSpec((tm, gw), lambda i: (i, gate_col)),
                  pl.BlockSpec((tm, gw), lambda i: (i, gate_col + 1)),
                  row(d),
                  pl.BlockSpec(wb.shape, lambda i: (0, 0, 0)),
                  pl.BlockSpec(wo.shape, lambda i: (0, 0)),
                  pl.BlockSpec((1, d), lambda i: (0, 0)), pl.BlockSpec((1, d), lambda i: (0, 0))],
        out_specs=row(d),
        out_shape=jax.ShapeDtypeStruct((t, d), F32),
        scratch_shapes=[pltpu.VMEM((tm, d), BF16)],
        compiler_params=_cparams("parallel"),
        name="merge",
    )(y_a, y_h, y_c, proj, proj, x, wb, wo, ln_g.reshape(1, d), ln_b.reshape(1, d))


def _swiglu_partial(xb, wg, wu, wd):
    g = jnp.dot(xb, wg, preferred_element_type=F32)
    u = jnp.dot(xb, wu, preferred_element_type=F32)
    return jnp.dot((jax.nn.silu(g) * u).astype(BF16), wd, preferred_element_type=F32)


def _ffn_kernel(x_ref, wg_ref, wu_ref, wd_ref, lg_ref, lb_ref, o_ref, xb_ref, acc_ref, *, alpha):
    f = pl.program_id(1)

    @pl.when(f == 0)
    def _():
        xb_ref[...] = x_ref[...].astype(BF16)
        acc_ref[...] = jnp.zeros_like(acc_ref)

    acc_ref[...] += _swiglu_partial(xb_ref[...], wg_ref[...], wu_ref[...], wd_ref[...])

    @pl.when(f == pl.num_programs(1) - 1)
    def _():
        o_ref[...] = _layer_norm(alpha * x_ref[...] + acc_ref[...], lg_ref[...], lb_ref[...])


def _ffn(x, wg, wu, wd, ln_g, ln_b, alpha):
    t, d = x.shape
    ff = wg.shape[1]
    tm = _tile(t, 1024)
    tf = ff // 2 if (ff // 2) % V7X_LANES == 0 else ff
    vec = pl.BlockSpec((1, d), lambda i, f: (0, 0))
    return pl.pallas_call(
        functools.partial(_ffn_kernel, alpha=alpha),
        grid=(t // tm, ff // tf),
        in_specs=[pl.BlockSpec((tm, d), lambda i, f: (i, 0)),
                  pl.BlockSpec((d, tf), lambda i, f: (0, f)),
                  pl.BlockSpec((d, tf), lambda i, f: (0, f)),
                  pl.BlockSpec((tf, d), lambda i, f: (f, 0)),
                  vec, vec],
        out_specs=pl.BlockSpec((tm, d), lambda i, f: (i, 0)),
        out_shape=jax.ShapeDtypeStruct((t, d), F32),
        scratch_shapes=[pltpu.VMEM((tm, d), BF16), pltpu.VMEM((tm, d), F32)],
        compiler_params=_cparams("parallel", "arbitrary"),
        name="ffn",
    )(x, wg, wu, wd, ln_g.reshape(1, d), ln_b.reshape(1, d))


def _top2_combine(logits, n_experts):
    lane = lax.broadcasted_iota(jnp.int32, logits.shape, 1).astype(F32)
    neg = jnp.float32(-jnp.inf)
    sentinel = jnp.float32(n_experts)
    lg = jnp.where(lane < sentinel, logits, neg)
    m1 = jnp.max(lg, axis=1, keepdims=True)
    i1 = jnp.min(jnp.where(lg == m1, lane, sentinel), axis=1, keepdims=True)
    rest = jnp.where(lane == i1, neg, lg)
    m2 = jnp.max(rest, axis=1, keepdims=True)
    i2 = jnp.min(jnp.where(rest == m2, lane, sentinel), axis=1, keepdims=True)
    e2 = jnp.exp(m2 - m1)
    p1 = 1.0 / (1.0 + e2)
    p2 = e2 / (1.0 + e2)
    return jnp.where(lane == i1, p1, jnp.where(lane == i2, p2, 0.0))


def _moe_kernel(x_ref, r_ref, wg_ref, wu_ref, wd_ref, lg_ref, lb_ref, o_ref,
                xb_ref, acc_ref, comb_ref, *, alpha, n_experts):
    e = pl.program_id(1)
    f = pl.program_id(2)
    first = jnp.logical_and(e == 0, f == 0)
    final = jnp.logical_and(e == pl.num_programs(1) - 1, f == pl.num_programs(2) - 1)

    @pl.when(first)
    def _():
        x = x_ref[...]
        xb_ref[...] = x.astype(BF16)
        acc_ref[...] = jnp.zeros_like(acc_ref)
        logits = jnp.dot(x, r_ref[...], precision=lax.Precision.HIGHEST, preferred_element_type=F32)
        comb_ref[...] = _top2_combine(logits, n_experts)

    comb = comb_ref[...]
    lane = lax.broadcasted_iota(jnp.int32, comb.shape, 1)
    weight = jnp.sum(jnp.where(lane == e, comb, 0.0), axis=1, keepdims=True)
    acc_ref[...] += weight * _swiglu_partial(xb_ref[...], wg_ref[...], wu_ref[...], wd_ref[...])

    @pl.when(final)
    def _():
        o_ref[...] = _layer_norm(alpha * x_ref[...] + acc_ref[...], lg_ref[...], lb_ref[...])


def _moe(x, router, wg, wu, wd, ln_g, ln_b, alpha):
    t, d = x.shape
    n_experts, _, ff = wg.shape
    tm = _tile(t, 512)
    tf = ff // 2 if (ff // 2) % V7X_LANES == 0 else ff
    router_p = jnp.pad(router.astype(F32), ((0, 0), (0, V7X_LANES - n_experts)))
    vec = pl.BlockSpec((1, d), lambda i, e, f: (0, 0))
    return pl.pallas_call(
        functools.partial(_moe_kernel, alpha=alpha, n_experts=n_experts),
        grid=(t // tm, n_experts, ff // tf),
        in_specs=[pl.BlockSpec((tm, d), lambda i, e, f: (i, 0)),
                  pl.BlockSpec((d, V7X_LANES), lambda i, e, f: (0, 0)),
                  pl.BlockSpec((None, d, tf), lambda i, e, f: (e, 0, f)),
                  pl.BlockSpec((None, d, tf), lambda i, e, f: (e, 0, f)),
                  pl.BlockSpec((None, tf, d), lambda i, e, f: (e, f, 0)),
                  vec, vec],
        out_specs=pl.BlockSpec((tm, d), lambda i, e, f: (i, 0)),
        out_shape=jax.ShapeDtypeStruct((t, d), F32),
        scratch_shapes=[pltpu.VMEM((tm, d), BF16), pltpu.VMEM((tm, d), F32),
                        pltpu.VMEM((tm, V7X_LANES), F32)],
        compiler_params=_cparams("parallel", "arbitrary", "arbitrary"),
        name="moe",
    )(x, router_p, wg, wu, wd, ln_g.reshape(1, d), ln_b.reshape(1, d))


def _run_trunk(x3, p, tables):
    bsz, seq, d = x3.shape
    depth = p['w_in'].shape[0]
    alpha = (2 * depth) ** 0.25
    att_w = ATT_HEADS * HEAD_W
    width = p['sc_conv_w'].shape[2]
    gate_w = 3 * d // 2
    assert att_w % width == 0 and (3 * att_w + 6 * width) % gate_w == 0, "column blocks must align"
    hy_col = att_w // width
    x = x3.reshape(bsz * seq, d)
    for i in range(depth):
        proj = _proj(x, p['w_in'][i])
        proj3 = proj.reshape(bsz, seq, proj.shape[1])
        y_a = _attention(proj3, p['att_lambda'][i], p['att_subln_g'][i], i)
        u, x0, y_c = _convs(proj3, p['hy_short_w'][i], p['sc_conv_w'][i], hy_col=hy_col, sc_col=hy_col + 1)
        spec = _hyena_filter_spectrum(seq, tables, p['hf_w_in'][i], p['hf_b_in'][i], p['hf_w_mid'][i],
                                      p['hf_b_mid'][i], p['hf_freq'][i], p['hf_w_out'][i])
        y_h = _long_conv(u, x0, tables, spec, p['hy_bias'][i])
        flat = lambda a: a.reshape(bsz * seq, a.shape[2])
        x = _merge(flat(y_a), flat(y_h), flat(y_c), proj, x, p['w_branch'][i], p['w_out'][i],
                   p['ln1_g'][i], p['ln1_b'][i], alpha, gate_col=(3 * att_w + 6 * width) // gate_w)
        j = i // 2
        if i % 2 == 0:
            x = _ffn(x, p['ffn_w_gate'][j], p['ffn_w_up'][j], p['ffn_w_down'][j],
                     p['ln2_g'][i], p['ln2_b'][i], alpha)
        else:
            x = _moe(x, p['router_w'][j], p['moe_w_gate'][j], p['moe_w_up'][j], p['moe_w_down'][j],
                     p['ln2_g'][i], p['ln2_b'][i], alpha)
    return x.reshape(bsz, seq, d)


def kernel(x_prompt, x_sample, w_in, att_lambda, att_subln_g, hy_short_w, hf_w_in, hf_b_in,
           hf_w_mid, hf_b_mid, hf_freq, hf_w_out, hy_bias, sc_conv_w, w_branch, w_out,
           ln1_g, ln1_b, ln2_g, ln2_b, ffn_w_gate, ffn_w_up, ffn_w_down, router_w,
           moe_w_gate, moe_w_up, moe_w_down):
    bf = lambda a: a.astype(BF16)
    p = dict(w_in=bf(w_in), att_lambda=att_lambda, att_subln_g=att_subln_g, hy_short_w=hy_short_w,
             hf_w_in=hf_w_in, hf_b_in=hf_b_in, hf_w_mid=hf_w_mid, hf_b_mid=hf_b_mid,
             hf_freq=hf_freq, hf_w_out=hf_w_out, hy_bias=hy_bias, sc_conv_w=sc_conv_w,
             w_branch=bf(w_branch), w_out=bf(w_out), ln1_g=ln1_g, ln1_b=ln1_b, ln2_g=ln2_g, ln2_b=ln2_b,
             ffn_w_gate=bf(ffn_w_gate), ffn_w_up=bf(ffn_w_up), ffn_w_down=bf(ffn_w_down),
             router_w=router_w, moe_w_gate=bf(moe_w_gate), moe_w_up=bf(moe_w_up),
             moe_w_down=bf(moe_w_down))
    tables = {}
    outs = []
    for x3 in (x_prompt, x_sample):
        seq = x3.shape[1]
        if seq not in tables:
            tables[seq] = _dft_tables(seq)
        outs.append(_run_trunk(x3, p, tables[seq]))
    return tuple(outs)
```

```python
import functools
import math

import numpy as np
import jax
import jax.numpy as jnp
from jax import lax
from jax.experimental import pallas as pl
from jax.experimental.pallas import tpu as pltpu

F32 = jnp.float32
BF16 = jnp.bfloat16

ATT_HEADS = 4
ATT_HEAD_DIM = 64
HEAD_W = 2 * ATT_HEAD_DIM
HY_BANDS = 16
HY_TARGET = 1e-2
HY_FAST_PCT = 0.3
HY_SLOW_PCT = 1.5
TOP_K = 2
LN_EPS = 1e-5
RMS_EPS = 1e-5

V7X_LANES = 128
V7X_BF16_SUBLANES = 16
V7X_VMEM_LIMIT = 56 * 1024 * 1024


def _cparams(*sem):
    return pltpu.CompilerParams(dimension_semantics=sem, vmem_limit_bytes=V7X_VMEM_LIMIT)


def _tile(n, pref):
    t = min(n, pref)
    while n % t:
        t //= 2
    return t


def _layer_norm(v, g, b):
    mu = jnp.mean(v, axis=-1, keepdims=True)
    d = v - mu
    var = jnp.mean(d * d, axis=-1, keepdims=True)
    return d * lax.rsqrt(var + LN_EPS) * g + b


def _proj_kernel(x_ref, w_ref, o_ref, xb_ref):
    @pl.when(pl.program_id(1) == 0)
    def _():
        xb_ref[...] = x_ref[...].astype(BF16)

    o_ref[...] = jnp.dot(xb_ref[...], w_ref[...], preferred_element_type=F32).astype(o_ref.dtype)


def _proj(x, w):
    t, d = x.shape
    n = w.shape[1]
    tm = _tile(t, 1024)
    tn = _tile(n, 1536)
    return pl.pallas_call(
        _proj_kernel,
        grid=(t // tm, n // tn),
        in_specs=[pl.BlockSpec((tm, d), lambda i, j: (i, 0)),
                  pl.BlockSpec((d, tn), lambda i, j: (0, j))],
        out_specs=pl.BlockSpec((tm, tn), lambda i, j: (i, j)),
        out_shape=jax.ShapeDtypeStruct((t, n), BF16),
        scratch_shapes=[pltpu.VMEM((tm, d), BF16)],
        compiler_params=_cparams("parallel", "arbitrary"),
        name="proj",
    )(x, w)


def _attn_kernel(slopes_ref, lam_ref, g_ref, q_ref, k_ref, v_ref, o_ref, *, lam_init, tq):
    h = pl.program_id(1)
    qi = pl.program_id(2)
    seq = k_ref.shape[1]
    slope = slopes_ref[h]
    lf = lam_ref[...].astype(F32)
    lam_full = (jnp.exp(jnp.sum(lf[0:1] * lf[1:2], axis=1, keepdims=True))
                - jnp.exp(jnp.sum(lf[2:3] * lf[3:4], axis=1, keepdims=True)) + lam_init)
    q = q_ref[0] * (ATT_HEAD_DIM ** -0.5)
    k = k_ref[0]
    v = v_ref[0]
    lane = lax.broadcasted_iota(jnp.int32, q.shape, 1)
    zero = jnp.zeros_like(q)
    q_pos = qi * tq + lax.broadcasted_iota(jnp.int32, (tq, seq), 0)
    k_pos = lax.broadcasted_iota(jnp.int32, (tq, seq), 1)
    bias = -slope * jnp.abs(q_pos - k_pos).astype(F32)

    def softmax_pv(qm):
        s = lax.dot_general(qm, k, (((1,), (1,)), ((), ())), preferred_element_type=F32) + bias
        m = jnp.max(s, axis=1, keepdims=True)
        e = jnp.exp(s - m)
        l = jnp.sum(e, axis=1, keepdims=True)
        return jnp.dot(e.astype(BF16), v, preferred_element_type=F32) / l

    o = (softmax_pv(jnp.where(lane < ATT_HEAD_DIM, q, zero))
         - lam_full * softmax_pv(jnp.where(lane >= ATT_HEAD_DIM, q, zero)))
    o = o * lax.rsqrt(jnp.mean(o * o, axis=-1, keepdims=True) + RMS_EPS) * g_ref[...].astype(F32)
    o_ref[0] = (o * (1.0 - lam_init)).astype(o_ref.dtype)


def _attention(proj3, lam, subln_g, layer):
    bsz, seq, _ = proj3.shape
    tq = _tile(seq, 256)
    lam_init = 0.8 - 0.6 * math.exp(-0.3 * layer)
    slopes = jnp.asarray(2.0 ** (-8.0 * np.arange(1, ATT_HEADS + 1, dtype=np.float32) / ATT_HEADS))
    kern = functools.partial(_attn_kernel, lam_init=lam_init, tq=tq)
    return pl.pallas_call(
        kern,
        grid=(bsz, ATT_HEADS, seq // tq),
        in_specs=[pl.BlockSpec(memory_space=pltpu.SMEM),
                  pl.BlockSpec((4, ATT_HEAD_DIM), lambda b, h, i: (0, 0)),
                  pl.BlockSpec((1, HEAD_W), lambda b, h, i: (0, 0)),
                  pl.BlockSpec((1, tq, HEAD_W), lambda b, h, i: (b, i, h)),
                  pl.BlockSpec((1, seq, HEAD_W), lambda b, h, i: (b, 0, ATT_HEADS + h)),
                  pl.BlockSpec((1, seq, HEAD_W), lambda b, h, i: (b, 0, 2 * ATT_HEADS + h))],
        out_specs=pl.BlockSpec((1, tq, HEAD_W), lambda b, h, i: (b, i, h)),
        out_shape=jax.ShapeDtypeStruct((bsz, seq, ATT_HEADS * HEAD_W), BF16),
        compiler_params=_cparams("parallel", "parallel", "arbitrary"),
        name="attn",
    )(slopes, lam, subln_g.reshape(1, HEAD_W), proj3, proj3, proj3)


def _conv3(x, prev_row, next_row, w):
    rows = x.shape[0]
    row = lax.broadcasted_iota(jnp.int32, (rows, 1), 0)
    x_prev = jnp.where(row == 0, prev_row, pltpu.roll(x, 1, axis=0))
    x_next = jnp.where(row == rows - 1, next_row, pltpu.roll(x, rows - 1, axis=0))
    return w[0:1] * x_prev + w[1:2] * x + w[2:3] * x_next


def _conv_kernel(hy_ref, hyp_ref, hyn_ref, sc_ref, scp_ref, scn_ref, hw_ref, sw_ref,
                 u_ref, x0_ref, yc_ref):
    i = pl.program_id(1)
    last = pl.num_programs(1) - 1
    width = u_ref.shape[2]
    has_prev = jnp.where(i > 0, 1.0, 0.0)
    has_next = jnp.where(i < last, 1.0, 0.0)
    hsub = hyp_ref.shape[1]

    hy = hy_ref[0].astype(F32)
    hy_prev = hyp_ref[0].astype(F32)[hsub - 1:hsub] * has_prev
    hy_next = hyn_ref[0].astype(F32)[0:1] * has_next
    hyc = _conv3(hy, hy_prev, hy_next, hw_ref[...].astype(F32))
    x0_ref[0] = hyc[:, :width].astype(x0_ref.dtype)
    u_ref[0] = (hyc[:, 2 * width:] * hyc[:, width:2 * width]).astype(u_ref.dtype)

    def gated(blk):
        return blk[:, width:2 * width] * blk[:, 2 * width:]

    sc = sc_ref[0].astype(F32)
    cx_prev = gated(scp_ref[0].astype(F32)[hsub - 1:hsub]) * has_prev
    cx_next = gated(scn_ref[0].astype(F32)[0:1]) * has_next
    conv = _conv3(gated(sc), cx_prev, cx_next, sw_ref[...].astype(F32))
    yc_ref[0] = (sc[:, :width] * conv).astype(yc_ref.dtype)


def _convs(proj3, hy_w, sc_w, hy_col, sc_col):
    bsz, seq, _ = proj3.shape
    width = sc_w.shape[1]
    tl = _tile(seq, 512)
    hs = V7X_BF16_SUBLANES
    nh = tl // hs
    nblk = seq // hs

    def main(col):
        return pl.BlockSpec((1, tl, 3 * width), lambda b, i: (b, i, col))

    def prev(col):
        return pl.BlockSpec((1, hs, 3 * width), lambda b, i: (b, jnp.maximum(i * nh - 1, 0), col))

    def nxt(col):
        return pl.BlockSpec((1, hs, 3 * width), lambda b, i: (b, jnp.minimum((i + 1) * nh, nblk - 1), col))

    out = jax.ShapeDtypeStruct((bsz, seq, width), BF16)
    ospec = pl.BlockSpec((1, tl, width), lambda b, i: (b, i, 0))
    return pl.pallas_call(
        _conv_kernel,
        grid=(bsz, seq // tl),
        in_specs=[main(hy_col), prev(hy_col), nxt(hy_col), main(sc_col), prev(sc_col), nxt(sc_col),
                  pl.BlockSpec((3, 3 * width), lambda b, i: (0, 0)),
                  pl.BlockSpec((3, width), lambda b, i: (0, 0))],
        out_specs=[ospec, ospec, ospec],
        out_shape=[out, out, out],
        compiler_params=_cparams("parallel", "arbitrary"),
        name="conv",
    )(proj3, proj3, proj3, proj3, proj3, proj3, hy_w, sc_w)


def _tab_kernel(c_ref, s_ref, st_ref, *, seq):
    tr = c_ref.shape[0]
    i = pl.program_id(0)
    r = i * tr + lax.broadcasted_iota(jnp.int32, (tr, seq), 0)
    c = lax.broadcasted_iota(jnp.int32, (tr, seq), 1)
    ang = ((r * c) & (2 * seq - 1)).astype(F32) * (math.pi / seq)
    cosv = jnp.cos(ang)
    nsin = -jnp.sin(ang)
    c_ref[...] = cosv.astype(c_ref.dtype)
    s_ref[...] = jnp.where(r == 0, jnp.where((c & 1) == 0, 1.0, -1.0), nsin).astype(s_ref.dtype)
    st_ref[...] = jnp.where(c == 0, jnp.where((r & 1) == 0, 1.0, -1.0), nsin).astype(st_ref.dtype)


def _dft_tables(seq):
    assert seq & (seq - 1) == 0, "sequence length must be a power of two"
    tr = _tile(seq, 256)
    spec = pl.BlockSpec((tr, seq), lambda i: (i, 0))
    out = jax.ShapeDtypeStruct((seq, seq), BF16)
    return pl.pallas_call(
        functools.partial(_tab_kernel, seq=seq),
        grid=(seq // tr,),
        out_specs=[spec, spec, spec],
        out_shape=[out, out, out],
        compiler_params=_cparams("parallel"),
        name="dft_tables",
    )()


def _filt_kernel(z_ref, win_ref, bin_ref, wmid_ref, bmid_ref, fr_ref, wout_ref, dl_ref,
                 hs_ref, l1_ref, *, seq):
    i = pl.program_id(0)
    tl = z_ref.shape[0]
    width = dl_ref.shape[1]
    hi = lax.Precision.HIGHEST
    fr = fr_ref[...]
    h = jnp.sin(fr * (jnp.dot(z_ref[...], win_ref[...], precision=hi, preferred_element_type=F32)
                      + bin_ref[...]))
    for j in range(wmid_ref.shape[0]):
        h = jnp.sin(fr * (jnp.dot(h, wmid_ref[j], precision=hi, preferred_element_type=F32)
                          + bmid_ref[j]))
    ho = jnp.dot(h, wout_ref[...], precision=hi, preferred_element_type=F32)
    pos = i * tl + lax.broadcasted_iota(jnp.int32, (tl, 1), 0)
    t = pos.astype(F32) / float(seq - 1)
    window = jnp.exp(-t * dl_ref[...])
    h_fwd = ho[:, :width] * window
    h_bwd = jnp.where(pos == 0, 0.0, ho[:, width:] * window)
    hs_ref[:, :width] = h_fwd + h_bwd
    hs_ref[:, width:] = h_fwd - h_bwd

    @pl.when(i == 0)
    def _():
        l1_ref[...] = jnp.zeros_like(l1_ref)

    l1_ref[...] += jnp.sum(jnp.abs(h_fwd) + jnp.abs(h_bwd), axis=0, keepdims=True)


def _fspec_kernel(c_ref, s_ref, hs_ref, l1_ref, a_ref, b_ref, d_ref, *, seq):
    i = pl.program_id(0)
    tk = c_ref.shape[0]
    width = l1_ref.shape[1]
    hs = hs_ref[...].astype(BF16)
    xc = jnp.dot(c_ref[...], hs[:, :width], preferred_element_type=F32)
    xs = jnp.dot(s_ref[...], hs, preferred_element_type=F32)
    k = i * tk + lax.broadcasted_iota(jnp.int32, (tk, 1), 0)
    inv_l1 = 1.0 / l1_ref[...]
    wk = jnp.where(k == 0, 1.0, 2.0) / float(2 * seq)
    a = xc * wk * inv_l1
    a_ref[...] = a
    b_ref[...] = jnp.where(k == 0, 0.0, xs[:, width:] * wk * inv_l1)
    d_ref[...] = jnp.where(k == 0, xs[:, :width] * wk * inv_l1, a)


def _hyena_filter_spectrum(seq, tabs, w_in, b_in, w_mid, b_mid, freq, w_out):
    c_tab, s_tab, _ = tabs
    emb, fw = w_in.shape
    width = w_out.shape[1] // 2
    n = np.arange(seq, dtype=np.float64)[:, None]
    f = np.linspace(1e-4, HY_BANDS - 1, HY_BANDS)[None]
    w = 2.0 * math.pi * n / seq
    z = np.concatenate([n / (seq - 1), np.cos(f * w), -np.sin(f * w)], -1)
    z = np.pad(z, ((0, 0), (0, V7X_LANES - emb))).astype(np.float32)
    w_in_p = jnp.pad(w_in.astype(F32), ((0, V7X_LANES - emb), (0, 0)))
    deltas = np.abs(np.linspace(math.log(HY_TARGET) / HY_SLOW_PCT, math.log(HY_TARGET) / HY_FAST_PCT,
                                width)).astype(np.float32)[None]
    tl = _tile(seq, 512)
    full = lambda *shape: pl.BlockSpec(shape, lambda i: (0,) * len(shape))
    hs, l1 = pl.pallas_call(
        functools.partial(_filt_kernel, seq=seq),
        grid=(seq // tl,),
        in_specs=[pl.BlockSpec((tl, V7X_LANES), lambda i: (i, 0)),
                  full(V7X_LANES, fw), full(1, fw), full(*w_mid.shape), full(w_mid.shape[0], 1, fw),
                  full(1, fw), full(fw, 2 * width), full(1, width)],
        out_specs=[pl.BlockSpec((tl, 2 * width), lambda i: (i, 0)), full(1, width)],
        out_shape=[jax.ShapeDtypeStruct((seq, 2 * width), F32), jax.ShapeDtypeStruct((1, width), F32)],
        compiler_params=_cparams("arbitrary"),
        name="hyena_filter",
    )(jnp.asarray(z), w_in_p, b_in.reshape(1, fw).astype(F32), w_mid.astype(F32),
      b_mid.reshape(w_mid.shape[0], 1, fw).astype(F32), freq.reshape(1, fw).astype(F32),
      w_out.astype(F32), jnp.asarray(deltas))

    tk = _tile(seq, 256)
    ospec = pl.BlockSpec((tk, width), lambda i: (i, 0))
    out = jax.ShapeDtypeStruct((seq, width), F32)
    return pl.pallas_call(
        functools.partial(_fspec_kernel, seq=seq),
        grid=(seq // tk,),
        in_specs=[pl.BlockSpec((tk, seq), lambda i: (i, 0)), pl.BlockSpec((tk, seq), lambda i: (i, 0)),
                  full(seq, 2 * width), full(1, width)],
        out_specs=[ospec, ospec, ospec],
        out_shape=[out, out, out],
        compiler_params=_cparams("parallel"),
        name="filter_spectrum",
    )(c_tab, s_tab, hs, l1)


def _dft_fwd_kernel(c_ref, s_ref, u_ref, a_ref, b_ref, d_ref, yre_ref, yim_ref):
    u = u_ref[0]
    x_re = jnp.dot(c_ref[...], u, preferred_element_type=F32)
    x_im = jnp.dot(s_ref[...], u, preferred_element_type=F32)
    b = b_ref[...]
    yre_ref[0] = (x_re * a_ref[...] - x_im * b).astype(yre_ref.dtype)
    yim_ref[0] = (x_re * b + x_im * d_ref[...]).astype(yim_ref.dtype)


def _dft_inv_kernel(c_ref, st_ref, yre_ref, yim_ref, u_ref, x0_ref, bias_ref, o_ref):
    y = (jnp.dot(c_ref[...], yre_ref[0], preferred_element_type=F32)
         + jnp.dot(st_ref[...], yim_ref[0], preferred_element_type=F32))
    y = y + u_ref[0].astype(F32) * bias_ref[...]
    o_ref[0] = (x0_ref[0].astype(F32) * y).astype(o_ref.dtype)


def _long_conv(u, x0, tabs, spec, bias):
    c_tab, s_tab, st_tab = tabs
    a, b, d = spec
    bsz, seq, width = u.shape
    tk = _tile(seq, 512)
    tab = pl.BlockSpec((tk, seq), lambda j, bb: (j, 0))
    whole = pl.BlockSpec((1, seq, width), lambda j, bb: (bb, 0, 0))
    rows = pl.BlockSpec((1, tk, width), lambda j, bb: (bb, j, 0))
    coef = pl.BlockSpec((tk, width), lambda j, bb: (j, 0))
    spec_out = jax.ShapeDtypeStruct((bsz, seq, width), BF16)
    y_re, y_im = pl.pallas_call(
        _dft_fwd_kernel,
        grid=(seq // tk, bsz),
        in_specs=[tab, tab, whole, coef, coef, coef],
        out_specs=[rows, rows],
        out_shape=[spec_out, spec_out],
        compiler_params=_cparams("parallel", "arbitrary"),
        name="dft_fwd",
    )(c_tab, s_tab, u, a, b, d)
    return pl.pallas_call(
        _dft_inv_kernel,
        grid=(seq // tk, bsz),
        in_specs=[tab, tab, whole, whole, rows, rows, pl.BlockSpec((1, width), lambda j, bb: (0, 0))],
        out_specs=rows,
        out_shape=spec_out,
        compiler_params=_cparams("parallel", "arbitrary"),
        name="dft_inv",
    )(c_tab, st_tab, y_re, y_im, u, x0, bias.reshape(1, width).astype(F32))


def _merge_kernel(ya_ref, yh_ref, yc_ref, g0_ref, g1_ref, x_ref, wb_ref, wo_ref, lg_ref, lb_ref,
                  o_ref, m_ref, *, alpha):
    d = o_ref.shape[1]
    half = d // 2
    g0 = g0_ref[...]
    g1 = g1_ref[...]
    gates = ((g0[:, :half], g0[:, d:], g1[:, half:d]),
             (g0[:, half:d], g1[:, :half], g1[:, d:]))
    branches = (ya_ref[...], yh_ref[...], yc_ref[...])
    for c in range(2):
        acc = None
        for j in range(3):
            y = jnp.dot(branches[j], wb_ref[j, :, c * half:(c + 1) * half], preferred_element_type=F32)
            term = jax.nn.sigmoid(gates[c][j].astype(F32)) * y
            acc = term if acc is None else acc + term
        m_ref[:, c * half:(c + 1) * half] = acc.astype(m_ref.dtype)
    t = jnp.dot(m_ref[...], wo_ref[...], preferred_element_type=F32)
    o_ref[...] = _layer_norm(alpha * x_ref[...] + t, lg_ref[...], lb_ref[...])


def _merge(y_a, y_h, y_c, proj, x, wb, wo, ln_g, ln_b, alpha, gate_col):
    t, d = x.shape
    bw = y_a.shape[1]
    tm = _tile(t, 512)
    row = lambda w: pl.BlockSpec((tm, w), lambda i: (i, 0))
    gw = 3 * d // 2
    return pl.pallas_call(
        functools.partial(_merge_kernel, alpha=alpha),
        grid=(t // tm,),
        in_specs=[row(bw), row(bw), row(bw),
                  pl.BlockSpec((tm, gw), lambda i: (i, gate_col)),
                  pl.BlockSpec((tm, gw), lambda i: (i, gate_col + 1)),
                  row(d),
                  pl.BlockSpec(wb.shape, lambda i: (0, 0, 0)),
                  pl.BlockSpec(wo.shape, lambda i: (0, 0)),
                  pl.BlockSpec((1, d), lambda i: (0, 0)), pl.BlockSpec((1, d), lambda i: (0, 0))],
        out_specs=row(d),
        out_shape=jax.ShapeDtypeStruct((t, d), F32),
        scratch_shapes=[pltpu.VMEM((tm, d), BF16)],
        compiler_params=_cparams("parallel"),
        name="merge",
    )(y_a, y_h, y_c, proj, proj, x, wb, wo, ln_g.reshape(1, d), ln_b.reshape(1, d))


def _swiglu_partial(xb, wg, wu, wd):
    g = jnp.dot(xb, wg, preferred_element_type=F32)
    u = jnp.dot(xb, wu, preferred_element_type=F32)
    return jnp.dot((jax.nn.silu(g) * u).astype(BF16), wd, preferred_element_type=F32)


def _ffn_kernel(x_ref, wg_ref, wu_ref, wd_ref, lg_ref, lb_ref, o_ref, xb_ref, acc_ref, *, alpha):
    f = pl.program_id(1)

    @pl.when(f == 0)
    def _():
        xb_ref[...] = x_ref[...].astype(BF16)
        acc_ref[...] = jnp.zeros_like(acc_ref)

    acc_ref[...] += _swiglu_partial(xb_ref[...], wg_ref[...], wu_ref[...], wd_ref[...])

    @pl.when(f == pl.num_programs(1) - 1)
    def _():
        o_ref[...] = _layer_norm(alpha * x_ref[...] + acc_ref[...], lg_ref[...], lb_ref[...])


def _ffn(x, wg, wu, wd, ln_g, ln_b, alpha):
    t, d = x.shape
    ff = wg.shape[1]
    tm = _tile(t, 1024)
    tf = ff // 2 if (ff // 2) % V7X_LANES == 0 else ff
    vec = pl.BlockSpec((1, d), lambda i, f: (0, 0))
    return pl.pallas_call(
        functools.partial(_ffn_kernel, alpha=alpha),
        grid=(t // tm, ff // tf),
        in_specs=[pl.BlockSpec((tm, d), lambda i, f: (i, 0)),
                  pl.BlockSpec((d, tf), lambda i, f: (0, f)),
                  pl.BlockSpec((d, tf), lambda i, f: (0, f)),
                  pl.BlockSpec((tf, d), lambda i, f: (f, 0)),
                  vec, vec],
        out_specs=pl.BlockSpec((tm, d), lambda i, f: (i, 0)),
        out_shape=jax.ShapeDtypeStruct((t, d), F32),
        scratch_shapes=[pltpu.VMEM((tm, d), BF16), pltpu.VMEM((tm, d), F32)],
        compiler_params=_cparams("parallel", "arbitrary"),
        name="ffn",
    )(x, wg, wu, wd, ln_g.reshape(1, d), ln_b.reshape(1, d))


def _top2(logits, n_experts):
    lane = lax.broadcasted_iota(jnp.int32, logits.shape, 1).astype(F32)
    neg = jnp.float32(-jnp.inf)
    sentinel = jnp.float32(n_experts)
    lg = jnp.where(lane < sentinel, logits, neg)
    m1 = jnp.max(lg, axis=1, keepdims=True)
    i1 = jnp.min(jnp.where(lg == m1, lane, sentinel), axis=1, keepdims=True)
    rest = jnp.where(lane == i1, neg, lg)
    m2 = jnp.max(rest, axis=1, keepdims=True)
    i2 = jnp.min(jnp.where(rest == m2, lane, sentinel), axis=1, keepdims=True)
    e2 = jnp.exp(m2 - m1)
    p1 = 1.0 / (1.0 + e2)
    p2 = e2 / (1.0 + e2)
    return lane == i1, lane == i2, i1, i2, p1, p2


ROUTE_COLS = 8
MOE_TILE_ROWS = 512


def _router_kernel(x_ref, r_ref, route_ref, cnt_ref, tri_ref, *, n_experts):
    i = pl.program_id(0)
    tm = x_ref.shape[0]

    @pl.when(i == 0)
    def _():
        cnt_ref[...] = jnp.zeros_like(cnt_ref)
        row = lax.broadcasted_iota(jnp.int32, (tm, tm), 0)
        col = lax.broadcasted_iota(jnp.int32, (tm, tm), 1)
        tri_ref[...] = jnp.where(col < row, 1.0, 0.0).astype(tri_ref.dtype)

    logits = jnp.dot(x_ref[...], r_ref[...], precision=lax.Precision.HIGHEST, preferred_element_type=F32)
    sel1, sel2, i1, i2, p1, p2 = _top2(logits, n_experts)
    chosen = jnp.where(sel1, 1.0, jnp.where(sel2, 1.0, 0.0))
    before = jnp.dot(tri_ref[...], chosen.astype(tri_ref.dtype), preferred_element_type=F32) + cnt_ref[...]
    r1 = jnp.sum(jnp.where(sel1, before, 0.0), axis=1, keepdims=True)
    r2 = jnp.sum(jnp.where(sel2, before, 0.0), axis=1, keepdims=True)
    cnt_ref[...] += jnp.sum(chosen, axis=0, keepdims=True)
    lane = lax.broadcasted_iota(jnp.int32, logits.shape, 1)
    record = jnp.zeros_like(logits)
    for c, val in enumerate((i1, i2, p1, p2, r1, r2)):
        record = jnp.where(lane == c, val, record)
    route_ref[...] = record[:, :ROUTE_COLS]


def _moe_scatter_kernel(pos_ref, x_ref, xs_in_ref, xs_ref, sem):
    del xs_in_ref
    tm = x_ref.shape[0]

    def row_copy(r, k):
        slot = pos_ref[0, 0, k * tm + r]
        return pltpu.make_async_copy(x_ref.at[pl.ds(r, 1)], xs_ref.at[pl.ds(slot, 1)], sem)

    def start(r, carry):
        row_copy(r, 0).start()
        row_copy(r, 1).start()
        return carry

    def wait(r, carry):
        row_copy(r, 0).wait()
        row_copy(r, 1).wait()
        return carry

    lax.fori_loop(0, tm, start, 0)
    lax.fori_loop(0, tm, wait, 0)


def _gmm_kernel(te_ref, nu_ref, xs_ref, wg_ref, wu_ref, wd_ref, ys_ref, xb_ref, acc_ref):
    del te_ref
    j = pl.program_id(0)
    f = pl.program_id(1)

    @pl.when(jnp.logical_and(j >= nu_ref[0], f == 0))
    def _():
        ys_ref[...] = jnp.zeros_like(ys_ref)

    @pl.when(j < nu_ref[0])
    def _():
        @pl.when(f == 0)
        def _():
            xb_ref[...] = xs_ref[...].astype(BF16)
            acc_ref[...] = jnp.zeros_like(acc_ref)

        acc_ref[...] += _swiglu_partial(xb_ref[...], wg_ref[...], wu_ref[...], wd_ref[...])

        @pl.when(f == pl.num_programs(1) - 1)
        def _():
            ys_ref[...] = acc_ref[...]


def _moe_combine_kernel(pos_ref, route_ref, x_ref, lg_ref, lb_ref, ys_ref, o_ref, y1_ref, y2_ref, sem,
                        *, alpha):
    tm = x_ref.shape[0]

    def row_copy(r, k):
        slot = pos_ref[0, 0, k * tm + r]
        dst = (y1_ref, y2_ref)[k]
        return pltpu.make_async_copy(ys_ref.at[pl.ds(slot, 1)], dst.at[pl.ds(r, 1)], sem)

    def start(r, carry):
        row_copy(r, 0).start()
        row_copy(r, 1).start()
        return carry

    def wait(r, carry):
        row_copy(r, 0).wait()
        row_copy(r, 1).wait()
        return carry

    lax.fori_loop(0, tm, start, 0)
    lax.fori_loop(0, tm, wait, 0)
    route = route_ref[...]
    mixed = route[:, 2:3] * y1_ref[...] + route[:, 3:4] * y2_ref[...]
    o_ref[...] = _layer_norm(alpha * x_ref[...] + mixed, lg_ref[...], lb_ref[...])


def _moe(x, router, wg, wu, wd, ln_g, ln_b, alpha):
    t, d = x.shape
    n_experts, _, ff = wg.shape
    tm = _tile(t, 512)
    tr = MOE_TILE_ROWS
    n_tok_tiles = t // tm

    router_p = jnp.pad(router.astype(F32), ((0, 0), (0, V7X_LANES - n_experts)))
    route, counts = pl.pallas_call(
        functools.partial(_router_kernel, n_experts=n_experts),
        grid=(n_tok_tiles,),
        in_specs=[pl.BlockSpec((tm, d), lambda i: (i, 0)),
                  pl.BlockSpec((d, V7X_LANES), lambda i: (0, 0))],
        out_specs=[pl.BlockSpec((tm, ROUTE_COLS), lambda i: (i, 0)),
                   pl.BlockSpec((1, V7X_LANES), lambda i: (0, 0))],
        out_shape=[jax.ShapeDtypeStruct((t, ROUTE_COLS), F32), jax.ShapeDtypeStruct((1, V7X_LANES), F32)],
        scratch_shapes=[pltpu.VMEM((tm, tm), BF16)],
        compiler_params=_cparams("arbitrary"),
        name="moe_router",
    )(x, router_p)

    cnt = counts[0, :n_experts].astype(jnp.int32)
    padded = (cnt + tr - 1) // tr * tr
    ends = jnp.cumsum(padded)
    starts = ends - padded
    expert_ids = jnp.arange(n_experts, dtype=jnp.int32)

    def slots(expert_col, rank_col):
        onehot = route[:, expert_col:expert_col + 1].astype(jnp.int32) == expert_ids[None]
        return jnp.sum(jnp.where(onehot, starts[None], 0), axis=1) + route[:, rank_col].astype(jnp.int32)

    pos = jnp.concatenate([slots(0, 4).reshape(n_tok_tiles, 1, tm), slots(1, 5).reshape(n_tok_tiles, 1, tm)],
                          axis=2)
    n_tiles = (TOP_K * t) // tr + n_experts
    n_used = (ends[-1] // tr).reshape(1)
    tile_start = jnp.arange(n_tiles, dtype=jnp.int32) * tr
    tile_expert = jnp.sum((tile_start[:, None] >= ends[None]).astype(jnp.int32), axis=1)
    last_expert = jnp.sum((ends[-1] - tr >= ends).astype(jnp.int32))
    tile_expert = jnp.where(tile_start < ends[-1], tile_expert, last_expert)

    pos_spec = pl.BlockSpec((1, 1, 2 * tm), lambda i: (i, 0, 0), memory_space=pltpu.SMEM)
    any_spec = pl.BlockSpec(memory_space=pl.ANY)
    xs = pl.pallas_call(
        _moe_scatter_kernel,
        grid=(n_tok_tiles,),
        in_specs=[pos_spec, pl.BlockSpec((tm, d), lambda i: (i, 0)), any_spec],
        out_specs=any_spec,
        out_shape=jax.ShapeDtypeStruct((n_tiles * tr, d), F32),
        scratch_shapes=[pltpu.SemaphoreType.DMA(())],
        input_output_aliases={2: 0},
        compiler_params=_cparams("arbitrary"),
        name="moe_scatter",
    )(pos, x, jnp.zeros((n_tiles * tr, d), F32))

    tf = ff // 2 if (ff // 2) % V7X_LANES == 0 else ff
    nf = ff // tf
    used = lambda j, nu: jnp.minimum(j, nu[0] - 1)
    fcol = lambda j, f, nu: jnp.where(j < nu[0], f, nf - 1)
    ys = pl.pallas_call(
        _gmm_kernel,
        grid_spec=pltpu.PrefetchScalarGridSpec(
            num_scalar_prefetch=2,
            grid=(n_tiles, nf),
            in_specs=[pl.BlockSpec((tr, d), lambda j, f, te, nu: (used(j, nu), 0)),
                      pl.BlockSpec((None, d, tf), lambda j, f, te, nu: (te[j], 0, fcol(j, f, nu))),
                      pl.BlockSpec((None, d, tf), lambda j, f, te, nu: (te[j], 0, fcol(j, f, nu))),
                      pl.BlockSpec((None, tf, d), lambda j, f, te, nu: (te[j], fcol(j, f, nu), 0))],
            out_specs=pl.BlockSpec((tr, d), lambda j, f, te, nu: (j, 0)),
            scratch_shapes=[pltpu.VMEM((tr, d), BF16), pltpu.VMEM((tr, d), F32)]),
        out_shape=jax.ShapeDtypeStruct((n_tiles * tr, d), F32),
        compiler_params=_cparams("arbitrary", "arbitrary"),
        name="moe_experts",
    )(tile_expert, n_used, xs, wg, wu, wd)

    vec = pl.BlockSpec((1, d), lambda i: (0, 0))
    return pl.pallas_call(
        functools.partial(_moe_combine_kernel, alpha=alpha),
        grid=(n_tok_tiles,),
        in_specs=[pos_spec, pl.BlockSpec((tm, ROUTE_COLS), lambda i: (i, 0)),
                  pl.BlockSpec((tm, d), lambda i: (i, 0)), vec, vec, any_spec],
        out_specs=pl.BlockSpec((tm, d), lambda i: (i, 0)),
        out_shape=jax.ShapeDtypeStruct((t, d), F32),
        scratch_shapes=[pltpu.VMEM((tm, d), F32), pltpu.VMEM((tm, d), F32), pltpu.SemaphoreType.DMA(())],
        compiler_params=_cparams("arbitrary"),
        name="moe_combine",
    )(pos, route, x, ln_g.reshape(1, d), ln_b.reshape(1, d), ys)


def _run_trunk(x3, p, tables):
    bsz, seq, d = x3.shape
    depth = p['w_in'].shape[0]
    alpha = (2 * depth) ** 0.25
    att_w = ATT_HEADS * HEAD_W
    width = p['sc_conv_w'].shape[2]
    gate_w = 3 * d // 2
    assert att_w % width == 0 and (3 * att_w + 6 * width) % gate_w == 0, "column blocks must align"
    hy_col = att_w // width
    x = x3.reshape(bsz * seq, d)
    for i in range(depth):
        proj = _proj(x, p['w_in'][i])
        proj3 = proj.reshape(bsz, seq, proj.shape[1])
        y_a = _attention(proj3, p['att_lambda'][i], p['att_subln_g'][i], i)
        u, x0, y_c = _convs(proj3, p['hy_short_w'][i], p['sc_conv_w'][i], hy_col=hy_col, sc_col=hy_col + 1)
        spec = _hyena_filter_spectrum(seq, tables, p['hf_w_in'][i], p['hf_b_in'][i], p['hf_w_mid'][i],
                                      p['hf_b_mid'][i], p['hf_freq'][i], p['hf_w_out'][i])
        y_h = _long_conv(u, x0, tables, spec, p['hy_bias'][i])
        flat = lambda a: a.reshape(bsz * seq, a.shape[2])
        x = _merge(flat(y_a), flat(y_h), flat(y_c), proj, x, p['w_branch'][i], p['w_out'][i],
                   p['ln1_g'][i], p['ln1_b'][i], alpha, gate_col=(3 * att_w + 6 * width) // gate_w)
        j = i // 2
        if i % 2 == 0:
            x = _ffn(x, p['ffn_w_gate'][j], p['ffn_w_up'][j], p['ffn_w_down'][j],
                     p['ln2_g'][i], p['ln2_b'][i], alpha)
        else:
            x = _moe(x, p['router_w'][j], p['moe_w_gate'][j], p['moe_w_up'][j], p['moe_w_down'][j],
                     p['ln2_g'][i], p['ln2_b'][i], alpha)
    return x.reshape(bsz, seq, d)


def kernel(x_prompt, x_sample, w_in, att_lambda, att_subln_g, hy_short_w, hf_w_in, hf_b_in,
           hf_w_mid, hf_b_mid, hf_freq, hf_w_out, hy_bias, sc_conv_w, w_branch, w_out,
           ln1_g, ln1_b, ln2_g, ln2_b, ffn_w_gate, ffn_w_up, ffn_w_down, router_w,
           moe_w_gate, moe_w_up, moe_w_down):
    bf = lambda a: a.astype(BF16)
    p = dict(w_in=bf(w_in), att_lambda=att_lambda, att_subln_g=att_subln_g, hy_short_w=hy_short_w,
             hf_w_in=hf_w_in, hf_b_in=hf_b_in, hf_w_mid=hf_w_mid, hf_b_mid=hf_b_mid,
             hf_freq=hf_freq, hf_w_out=hf_w_out, hy_bias=hy_bias, sc_conv_w=sc_conv_w,
             w_branch=bf(w_branch), w_out=bf(w_out), ln1_g=ln1_g, ln1_b=ln1_b, ln2_g=ln2_g, ln2_b=ln2_b,
             ffn_w_gate=bf(ffn_w_gate), ffn_w_up=bf(ffn_w_up), ffn_w_down=bf(ffn_w_down),
             router_w=router_w, moe_w_gate=bf(moe_w_gate), moe_w_up=bf(moe_w_up),
             moe_w_down=bf(moe_w_down))
    tables = {}
    outs = []
    for x3 in (x_prompt, x_sample):
        seq = x3.shape[1]
        if seq not in tables:
            tables[seq] = _dft_tables(seq)
        outs.append(_run_trunk(x3, p, tables[seq]))
    return tuple(outs)
```

```python
import functools
import math

import numpy as np
import jax
import jax.numpy as jnp
from jax import lax
from jax.experimental import pallas as pl
from jax.experimental.pallas import tpu as pltpu

F32 = jnp.float32
BF16 = jnp.bfloat16

ATT_HEADS = 4
ATT_HEAD_DIM = 64
HEAD_W = 2 * ATT_HEAD_DIM
HY_BANDS = 16
HY_TARGET = 1e-2
HY_FAST_PCT = 0.3
HY_SLOW_PCT = 1.5
TOP_K = 2
LN_EPS = 1e-5
RMS_EPS = 1e-5
LOG2E = math.log2(math.e)
ATT_ROW_SPLIT = 2

V7X_LANES = 128
V7X_BF16_SUBLANES = 16
V7X_VMEM_LIMIT = 56 * 1024 * 1024


def _cparams(*sem):
    return pltpu.CompilerParams(dimension_semantics=sem, vmem_limit_bytes=V7X_VMEM_LIMIT)


def _tile(n, pref):
    t = min(n, pref)
    while n % t:
        t //= 2
    return t


def _layer_norm(v, g, b):
    mu = jnp.mean(v, axis=-1, keepdims=True)
    d = v - mu
    var = jnp.mean(d * d, axis=-1, keepdims=True)
    return d * lax.rsqrt(var + LN_EPS) * g + b


def _proj_kernel(x_ref, w_ref, cs_ref, o_ref, xb_ref):
    @pl.when(pl.program_id(1) == 0)
    def _():
        xb_ref[...] = x_ref[...].astype(BF16)

    acc = jnp.dot(xb_ref[...], w_ref[...], preferred_element_type=F32)
    o_ref[...] = (acc * cs_ref[...]).astype(o_ref.dtype)


def _proj(x, w, col_scale):
    t, d = x.shape
    n = w.shape[1]
    tm = _tile(t, 1024)
    tn = _tile(n, 1536)
    return pl.pallas_call(
        _proj_kernel,
        grid=(t // tm, n // tn),
        in_specs=[pl.BlockSpec((tm, d), lambda i, j: (i, 0)),
                  pl.BlockSpec((d, tn), lambda i, j: (0, j)),
                  pl.BlockSpec((1, tn), lambda i, j: (0, j))],
        out_specs=pl.BlockSpec((tm, tn), lambda i, j: (i, j)),
        out_shape=jax.ShapeDtypeStruct((t, n), BF16),
        scratch_shapes=[pltpu.VMEM((tm, d), BF16)],
        compiler_params=_cparams("parallel", "arbitrary"),
        name="proj",
    )(x, w, col_scale)


def _attn_kernel(slopes_ref, lam_ref, g_ref, q_ref, k_ref, v_ref, o_ref, bias_ref, *, lam_init, tq):
    h = pl.program_id(1)
    qi = pl.program_id(2)
    seq = k_ref.shape[1]

    @pl.when(qi == 0)
    def _():
        r = lax.broadcasted_iota(jnp.int32, bias_ref.shape, 0)
        j = lax.broadcasted_iota(jnp.int32, bias_ref.shape, 1)
        bias_ref[...] = (-LOG2E * slopes_ref[h]) * jnp.abs(r - j + (seq - tq)).astype(F32)

    start = pl.multiple_of((pl.num_programs(2) - 1 - qi) * tq, tq)
    lf = lam_ref[...].astype(F32)
    lam_full = (jnp.exp(jnp.sum(lf[0:1] * lf[1:2], axis=1, keepdims=True))
                - jnp.exp(jnp.sum(lf[2:3] * lf[3:4], axis=1, keepdims=True)) + lam_init)
    k = k_ref[0]
    v = v_ref[0]
    rows = tq // ATT_ROW_SPLIT

    def softmax_terms(s_raw, bias):
        m = jnp.max(s_raw + bias, axis=1, keepdims=True)
        e = jnp.exp2(s_raw + (bias - m))
        return e, jnp.sum(e, axis=1, keepdims=True)

    outs = []
    for part in range(ATT_ROW_SPLIT):
        q = q_ref[0, part * rows:(part + 1) * rows, :]
        bias = bias_ref[part * rows:(part + 1) * rows, pl.ds(start, seq)]
        lane = lax.broadcasted_iota(jnp.int32, q.shape, 1)
        zero = jnp.zeros_like(q)
        q_maps = jnp.concatenate([jnp.where(lane < ATT_HEAD_DIM, q, zero),
                                  jnp.where(lane >= ATT_HEAD_DIM, q, zero)], axis=0)
        s = lax.dot_general(q_maps, k, (((1,), (1,)), ((), ())), preferred_element_type=F32)
        e1, l1 = softmax_terms(s[:rows], bias)
        e2, l2 = softmax_terms(s[rows:], bias)
        a = e1 * (1.0 / l1) - e2 * (lam_full / l2)
        outs.append(jnp.dot(a.astype(BF16), v, preferred_element_type=F32))
    o = jnp.concatenate(outs, axis=0)
    o = o * lax.rsqrt(jnp.mean(o * o, axis=-1, keepdims=True) + RMS_EPS) * g_ref[...].astype(F32)
    o_ref[0] = (o * (1.0 - lam_init)).astype(o_ref.dtype)


def _attention(proj3, lam, subln_g, layer):
    bsz, seq, _ = proj3.shape
    tq = _tile(seq, 256)
    lam_init = 0.8 - 0.6 * math.exp(-0.3 * layer)
    slopes = jnp.asarray(2.0 ** (-8.0 * np.arange(1, ATT_HEADS + 1, dtype=np.float32) / ATT_HEADS))
    kern = functools.partial(_attn_kernel, lam_init=lam_init, tq=tq)
    return pl.pallas_call(
        kern,
        grid=(bsz, ATT_HEADS, seq // tq),
        in_specs=[pl.BlockSpec(memory_space=pltpu.SMEM),
                  pl.BlockSpec((4, ATT_HEAD_DIM), lambda b, h, i: (0, 0)),
                  pl.BlockSpec((1, HEAD_W), lambda b, h, i: (0, 0)),
                  pl.BlockSpec((1, tq, HEAD_W), lambda b, h, i: (b, i, h)),
                  pl.BlockSpec((1, seq, HEAD_W), lambda b, h, i: (b, 0, ATT_HEADS + h)),
                  pl.BlockSpec((1, seq, HEAD_W), lambda b, h, i: (b, 0, 2 * ATT_HEADS + h))],
        out_specs=pl.BlockSpec((1, tq, HEAD_W), lambda b, h, i: (b, i, h)),
        out_shape=jax.ShapeDtypeStruct((bsz, seq, ATT_HEADS * HEAD_W), BF16),
        scratch_shapes=[pltpu.VMEM((tq, 2 * seq - tq), F32)],
        compiler_params=_cparams("parallel", "parallel", "arbitrary"),
        name="attn",
    )(slopes, lam, subln_g.reshape(1, HEAD_W), proj3, proj3, proj3)


def _conv3(x, prev_row, next_row, w):
    rows = x.shape[0]
    row = lax.broadcasted_iota(jnp.int32, (rows, 1), 0)
    x_prev = jnp.where(row == 0, prev_row, pltpu.roll(x, 1, axis=0))
    x_next = jnp.where(row == rows - 1, next_row, pltpu.roll(x, rows - 1, axis=0))
    return w[0:1] * x_prev + w[1:2] * x + w[2:3] * x_next


def _conv_kernel(hy_ref, hyp_ref, hyn_ref, sc_ref, scp_ref, scn_ref, hw_ref, sw_ref,
                 u_ref, x0_ref, yc_ref):
    i = pl.program_id(1)
    last = pl.num_programs(1) - 1
    width = u_ref.shape[2]
    has_prev = jnp.where(i > 0, 1.0, 0.0)
    has_next = jnp.where(i < last, 1.0, 0.0)
    hsub = hyp_ref.shape[1]

    hy = hy_ref[0].astype(F32)
    hy_prev = hyp_ref[0].astype(F32)[hsub - 1:hsub] * has_prev
    hy_next = hyn_ref[0].astype(F32)[0:1] * has_next
    hyc = _conv3(hy, hy_prev, hy_next, hw_ref[...].astype(F32))
    x0_ref[0] = hyc[:, :width].astype(x0_ref.dtype)
    u_ref[0] = (hyc[:, 2 * width:] * hyc[:, width:2 * width]).astype(u_ref.dtype)

    def gated(blk):
        return blk[:, width:2 * width] * blk[:, 2 * width:]

    sc = sc_ref[0].astype(F32)
    cx_prev = gated(scp_ref[0].astype(F32)[hsub - 1:hsub]) * has_prev
    cx_next = gated(scn_ref[0].astype(F32)[0:1]) * has_next
    conv = _conv3(gated(sc), cx_prev, cx_next, sw_ref[...].astype(F32))
    yc_ref[0] = (sc[:, :width] * conv).astype(yc_ref.dtype)


def _convs(proj3, hy_w, sc_w, hy_col, sc_col):
    bsz, seq, _ = proj3.shape
    width = sc_w.shape[1]
    tl = _tile(seq, 512)
    hs = V7X_BF16_SUBLANES
    nh = tl // hs
    nblk = seq // hs

    def main(col):
        return pl.BlockSpec((1, tl, 3 * width), lambda b, i: (b, i, col))

    def prev(col):
        return pl.BlockSpec((1, hs, 3 * width), lambda b, i: (b, jnp.maximum(i * nh - 1, 0), col))

    def nxt(col):
        return pl.BlockSpec((1, hs, 3 * width), lambda b, i: (b, jnp.minimum((i + 1) * nh, nblk - 1), col))

    out = jax.ShapeDtypeStruct((bsz, seq, width), BF16)
    ospec = pl.BlockSpec((1, tl, width), lambda b, i: (b, i, 0))
    return pl.pallas_call(
        _conv_kernel,
        grid=(bsz, seq // tl),
        in_specs=[main(hy_col), prev(hy_col), nxt(hy_col), main(sc_col), prev(sc_col), nxt(sc_col),
                  pl.BlockSpec((3, 3 * width), lambda b, i: (0, 0)),
                  pl.BlockSpec((3, width), lambda b, i: (0, 0))],
        out_specs=[ospec, ospec, ospec],
        out_shape=[out, out, out],
        compiler_params=_cparams("parallel", "arbitrary"),
        name="conv",
    )(proj3, proj3, proj3, proj3, proj3, proj3, hy_w, sc_w)


def _tab_kernel(c_ref, s_ref, st_ref, *, seq):
    tr = c_ref.shape[0]
    i = pl.program_id(0)
    r = i * tr + lax.broadcasted_iota(jnp.int32, (tr, seq), 0)
    c = lax.broadcasted_iota(jnp.int32, (tr, seq), 1)
    ang = ((r * c) & (2 * seq - 1)).astype(F32) * (math.pi / seq)
    cosv = jnp.cos(ang)
    nsin = -jnp.sin(ang)
    c_ref[...] = cosv.astype(c_ref.dtype)
    s_ref[...] = jnp.where(r == 0, jnp.where((c & 1) == 0, 1.0, -1.0), nsin).astype(s_ref.dtype)
    st_ref[...] = jnp.where(c == 0, jnp.where((r & 1) == 0, 1.0, -1.0), nsin).astype(st_ref.dtype)


def _dft_tables(seq):
    assert seq & (seq - 1) == 0, "sequence length must be a power of two"
    tr = _tile(seq, 256)
    spec = pl.BlockSpec((tr, seq), lambda i: (i, 0))
    out = jax.ShapeDtypeStruct((seq, seq), BF16)
    return pl.pallas_call(
        functools.partial(_tab_kernel, seq=seq),
        grid=(seq // tr,),
        out_specs=[spec, spec, spec],
        out_shape=[out, out, out],
        compiler_params=_cparams("parallel"),
        name="dft_tables",
    )()


def _filt_kernel(z_ref, win_ref, bin_ref, wmid_ref, bmid_ref, fr_ref, wout_ref, dl_ref,
                 hs_ref, l1_ref, *, seq):
    i = pl.program_id(0)
    tl = z_ref.shape[0]
    width = dl_ref.shape[1]
    hi = lax.Precision.HIGHEST
    fr = fr_ref[...]
    h = jnp.sin(fr * (jnp.dot(z_ref[...], win_ref[...], precision=hi, preferred_element_type=F32)
                      + bin_ref[...]))
    for j in range(wmid_ref.shape[0]):
        h = jnp.sin(fr * (jnp.dot(h, wmid_ref[j], precision=hi, preferred_element_type=F32)
                          + bmid_ref[j]))
    ho = jnp.dot(h, wout_ref[...], precision=hi, preferred_element_type=F32)
    pos = i * tl + lax.broadcasted_iota(jnp.int32, (tl, 1), 0)
    t = pos.astype(F32) / float(seq - 1)
    window = jnp.exp(-t * dl_ref[...])
    h_fwd = ho[:, :width] * window
    h_bwd = jnp.where(pos == 0, 0.0, ho[:, width:] * window)
    hs_ref[:, :width] = h_fwd + h_bwd
    hs_ref[:, width:] = h_fwd - h_bwd

    @pl.when(i == 0)
    def _():
        l1_ref[...] = jnp.zeros_like(l1_ref)

    l1_ref[...] += jnp.sum(jnp.abs(h_fwd) + jnp.abs(h_bwd), axis=0, keepdims=True)


def _fspec_kernel(c_ref, s_ref, hs_ref, l1_ref, a_ref, b_ref, d_ref, *, seq):
    i = pl.program_id(0)
    tk = c_ref.shape[0]
    width = l1_ref.shape[1]
    hs = hs_ref[...].astype(BF16)
    xc = jnp.dot(c_ref[...], hs[:, :width], preferred_element_type=F32)
    xs = jnp.dot(s_ref[...], hs, preferred_element_type=F32)
    k = i * tk + lax.broadcasted_iota(jnp.int32, (tk, 1), 0)
    inv_l1 = 1.0 / l1_ref[...]
    wk = jnp.where(k == 0, 1.0, 2.0) / float(2 * seq)
    a = xc * wk * inv_l1
    a_ref[...] = a
    b_ref[...] = jnp.where(k == 0, 0.0, xs[:, width:] * wk * inv_l1)
    d_ref[...] = jnp.where(k == 0, xs[:, :width] * wk * inv_l1, a)


def _hyena_filter_spectrum(seq, tabs, w_in, b_in, w_mid, b_mid, freq, w_out):
    c_tab, s_tab, _ = tabs
    emb, fw = w_in.shape
    width = w_out.shape[1] // 2
    n = np.arange(seq, dtype=np.float64)[:, None]
    f = np.linspace(1e-4, HY_BANDS - 1, HY_BANDS)[None]
    w = 2.0 * math.pi * n / seq
    z = np.concatenate([n / (seq - 1), np.cos(f * w), -np.sin(f * w)], -1)
    z = np.pad(z, ((0, 0), (0, V7X_LANES - emb))).astype(np.float32)
    w_in_p = jnp.pad(w_in.astype(F32), ((0, V7X_LANES - emb), (0, 0)))
    deltas = np.abs(np.linspace(math.log(HY_TARGET) / HY_SLOW_PCT, math.log(HY_TARGET) / HY_FAST_PCT,
                                width)).astype(np.float32)[None]
    tl = _tile(seq, 512)
    full = lambda *shape: pl.BlockSpec(shape, lambda i: (0,) * len(shape))
    hs, l1 = pl.pallas_call(
        functools.partial(_filt_kernel, seq=seq),
        grid=(seq // tl,),
        in_specs=[pl.BlockSpec((tl, V7X_LANES), lambda i: (i, 0)),
                  full(V7X_LANES, fw), full(1, fw), full(*w_mid.shape), full(w_mid.shape[0], 1, fw),
                  full(1, fw), full(fw, 2 * width), full(1, width)],
        out_specs=[pl.BlockSpec((tl, 2 * width), lambda i: (i, 0)), full(1, width)],
        out_shape=[jax.ShapeDtypeStruct((seq, 2 * width), F32), jax.ShapeDtypeStruct((1, width), F32)],
        compiler_params=_cparams("arbitrary"),
        name="hyena_filter",
    )(jnp.asarray(z), w_in_p, b_in.reshape(1, fw).astype(F32), w_mid.astype(F32),
      b_mid.reshape(w_mid.shape[0], 1, fw).astype(F32), freq.reshape(1, fw).astype(F32),
      w_out.astype(F32), jnp.asarray(deltas))

    tk = _tile(seq, 256)
    ospec = pl.BlockSpec((tk, width), lambda i: (i, 0))
    out = jax.ShapeDtypeStruct((seq, width), F32)
    return pl.pallas_call(
        functools.partial(_fspec_kernel, seq=seq),
        grid=(seq // tk,),
        in_specs=[pl.BlockSpec((tk, seq), lambda i: (i, 0)), pl.BlockSpec((tk, seq), lambda i: (i, 0)),
                  full(seq, 2 * width), full(1, width)],
        out_specs=[ospec, ospec, ospec],
        out_shape=[out, out, out],
        compiler_params=_cparams("parallel"),
        name="filter_spectrum",
    )(c_tab, s_tab, hs, l1)


def _dft_fwd_kernel(c_ref, s_ref, u_ref, a_ref, b_ref, d_ref, yre_ref, yim_ref):
    u = u_ref[0]
    x_re = jnp.dot(c_ref[...], u, preferred_element_type=F32)
    x_im = jnp.dot(s_ref[...], u, preferred_element_type=F32)
    b = b_ref[...]
    yre_ref[0] = (x_re * a_ref[...] - x_im * b).astype(yre_ref.dtype)
    yim_ref[0] = (x_re * b + x_im * d_ref[...]).astype(yim_ref.dtype)


def _dft_inv_kernel(c_ref, st_ref, yre_ref, yim_ref, u_ref, x0_ref, bias_ref, o_ref):
    y = (jnp.dot(c_ref[...], yre_ref[0], preferred_element_type=F32)
         + jnp.dot(st_ref[...], yim_ref[0], preferred_element_type=F32))
    y = y + u_ref[0].astype(F32) * bias_ref[...]
    o_ref[0] = (x0_ref[0].astype(F32) * y).astype(o_ref.dtype)


def _long_conv(u, x0, tabs, spec, bias):
    c_tab, s_tab, st_tab = tabs
    a, b, d = spec
    bsz, seq, width = u.shape
    tk = _tile(seq, 512)
    tab = pl.BlockSpec((tk, seq), lambda j, bb: (j, 0))
    whole = pl.BlockSpec((1, seq, width), lambda j, bb: (bb, 0, 0))
    rows = pl.BlockSpec((1, tk, width), lambda j, bb: (bb, j, 0))
    coef = pl.BlockSpec((tk, width), lambda j, bb: (j, 0))
    spec_out = jax.ShapeDtypeStruct((bsz, seq, width), BF16)
    y_re, y_im = pl.pallas_call(
        _dft_fwd_kernel,
        grid=(seq // tk, bsz),
        in_specs=[tab, tab, whole, coef, coef, coef],
        out_specs=[rows, rows],
        out_shape=[spec_out, spec_out],
        compiler_params=_cparams("parallel", "arbitrary"),
        name="dft_fwd",
    )(c_tab, s_tab, u, a, b, d)
    return pl.pallas_call(
        _dft_inv_kernel,
        grid=(seq // tk, bsz),
        in_specs=[tab, tab, whole, whole, rows, rows, pl.BlockSpec((1, width), lambda j, bb: (0, 0))],
        out_specs=rows,
        out_shape=spec_out,
        compiler_params=_cparams("parallel", "arbitrary"),
        name="dft_inv",
    )(c_tab, st_tab, y_re, y_im, u, x0, bias.reshape(1, width).astype(F32))


def _merge_kernel(ya_ref, yh_ref, yc_ref, g0_ref, g1_ref, x_ref, wb_ref, wo_ref, lg_ref, lb_ref,
                  o_ref, m_ref, *, alpha):
    d = o_ref.shape[1]
    half = d // 2
    g0 = g0_ref[...]
    g1 = g1_ref[...]
    gates = ((g0[:, :half], g0[:, d:], g1[:, half:d]),
             (g0[:, half:d], g1[:, :half], g1[:, d:]))
    branches = (ya_ref[...], yh_ref[...], yc_ref[...])
    for c in range(2):
        acc = None
        for j in range(3):
            y = jnp.dot(branches[j], wb_ref[j, :, c * half:(c + 1) * half], preferred_element_type=F32)
            term = jax.nn.sigmoid(gates[c][j].astype(F32)) * y
            acc = term if acc is None else acc + term
        m_ref[:, c * half:(c + 1) * half] = acc.astype(m_ref.dtype)
    t = jnp.dot(m_ref[...], wo_ref[...], preferred_element_type=F32)
    o_ref[...] = _layer_norm(alpha * x_ref[...] + t, lg_ref[...], lb_ref[...])


def _merge(y_a, y_h, y_c, proj, x, wb, wo, ln_g, ln_b, alpha, gate_col):
    t, d = x.shape
    bw = y_a.shape[1]
    tm = _tile(t, 512)
    row = lambda w: pl.BlockSpec((tm, w), lambda i: (i, 0))
    gw = 3 * d // 2
    return pl.pallas_call(
        functools.partial(_merge_kernel, alpha=alpha),
        grid=(t // tm,),
        in_specs=[row(bw), row(bw), row(bw),
                  pl.BlockSpec((tm, gw), lambda i: (i, gate_col)),
                  pl.BlockSpec((tm, gw), lambda i: (i, gate_col + 1)),
                  row(d),
                  pl.BlockSpec(wb.shape, lambda i: (0, 0, 0)),
                  pl.BlockSpec(wo.shape, lambda i: (0, 0)),
                  pl.BlockSpec((1, d), lambda i: (0, 0)), pl.BlockSpec((1, d), lambda i: (0, 0))],
        out_specs=row(d),
        out_shape=jax.ShapeDtypeStruct((t, d), F32),
        scratch_shapes=[pltpu.VMEM((tm, d), BF16)],
        compiler_params=_cparams("parallel"),
        name="merge",
    )(y_a, y_h, y_c, proj, proj, x, wb, wo, ln_g.reshape(1, d), ln_b.reshape(1, d))


def _swiglu_partial(xb, wg, wu, wd):
    g = jnp.dot(xb, wg, preferred_element_type=F32)
    u = jnp.dot(xb, wu, preferred_element_type=F32)
    return jnp.dot((jax.nn.silu(g) * u).astype(BF16), wd, preferred_element_type=F32)


def _ffn_kernel(x_ref, wg_ref, wu_ref, wd_ref, lg_ref, lb_ref, o_ref, xb_ref, acc_ref, *, alpha):
    f = pl.program_id(1)

    @pl.when(f == 0)
    def _():
        xb_ref[...] = x_ref[...].astype(BF16)
        acc_ref[...] = jnp.zeros_like(acc_ref)

    acc_ref[...] += _swiglu_partial(xb_ref[...], wg_ref[...], wu_ref[...], wd_ref[...])

    @pl.when(f == pl.num_programs(1) - 1)
    def _():
        o_ref[...] = _layer_norm(alpha * x_ref[...] + acc_ref[...], lg_ref[...], lb_ref[...])


def _ffn(x, wg, wu, wd, ln_g, ln_b, alpha):
    t, d = x.shape
    ff = wg.shape[1]
    tm = _tile(t, 1024)
    tf = ff // 2 if (ff // 2) % V7X_LANES == 0 else ff
    vec = pl.BlockSpec((1, d), lambda i, f: (0, 0))
    return pl.pallas_call(
        functools.partial(_ffn_kernel, alpha=alpha),
        grid=(t // tm, ff // tf),
        in_specs=[pl.BlockSpec((tm, d), lambda i, f: (i, 0)),
                  pl.BlockSpec((d, tf), lambda i, f: (0, f)),
                  pl.BlockSpec((d, tf), lambda i, f: (0, f)),
                  pl.BlockSpec((tf, d), lambda i, f: (f, 0)),
                  vec, vec],
        out_specs=pl.BlockSpec((tm, d), lambda i, f: (i, 0)),
        out_shape=jax.ShapeDtypeStruct((t, d), F32),
        scratch_shapes=[pltpu.VMEM((tm, d), BF16), pltpu.VMEM((tm, d), F32)],
        compiler_params=_cparams("parallel", "arbitrary"),
        name="ffn",
    )(x, wg, wu, wd, ln_g.reshape(1, d), ln_b.reshape(1, d))


def _top2(logits, n_experts):
    lane = lax.broadcasted_iota(jnp.int32, logits.shape, 1).astype(F32)
    neg = jnp.float32(-jnp.inf)
    sentinel = jnp.float32(n_experts)
    lg = jnp.where(lane < sentinel, logits, neg)
    m1 = jnp.max(lg, axis=1, keepdims=True)
    i1 = jnp.min(jnp.where(lg == m1, lane, sentinel), axis=1, keepdims=True)
    rest = jnp.where(lane == i1, neg, lg)
    m2 = jnp.max(rest, axis=1, keepdims=True)
    i2 = jnp.min(jnp.where(rest == m2, lane, sentinel), axis=1, keepdims=True)
    e2 = jnp.exp(m2 - m1)
    p1 = 1.0 / (1.0 + e2)
    p2 = e2 / (1.0 + e2)
    return lane == i1, lane == i2, i1, i2, p1, p2


ROUTE_COLS = 8
MOE_TILE_ROWS = 512
ROW_DMA_UNROLL = 8


def _router_kernel(x_ref, r_ref, route_ref, cnt_ref, tri_ref, *, n_experts):
    i = pl.program_id(0)
    tm = x_ref.shape[0]

    @pl.when(i == 0)
    def _():
        cnt_ref[...] = jnp.zeros_like(cnt_ref)
        row = lax.broadcasted_iota(jnp.int32, (tm, tm), 0)
        col = lax.broadcasted_iota(jnp.int32, (tm, tm), 1)
        tri_ref[...] = jnp.where(col < row, 1.0, 0.0).astype(tri_ref.dtype)

    logits = jnp.dot(x_ref[...], r_ref[...], precision=lax.Precision.HIGHEST, preferred_element_type=F32)
    sel1, sel2, i1, i2, p1, p2 = _top2(logits, n_experts)
    chosen = jnp.where(sel1, 1.0, jnp.where(sel2, 1.0, 0.0))
    before = jnp.dot(tri_ref[...], chosen.astype(tri_ref.dtype), preferred_element_type=F32) + cnt_ref[...]
    r1 = jnp.sum(jnp.where(sel1, before, 0.0), axis=1, keepdims=True)
    r2 = jnp.sum(jnp.where(sel2, before, 0.0), axis=1, keepdims=True)
    cnt_ref[...] += jnp.sum(chosen, axis=0, keepdims=True)
    lane = lax.broadcasted_iota(jnp.int32, logits.shape, 1)
    record = jnp.zeros_like(logits)
    for c, val in enumerate((i1, i2, p1, p2, r1, r2)):
        record = jnp.where(lane == c, val, record)
    route_ref[...] = record[:, :ROUTE_COLS]


def _moe_scatter_kernel(pos_ref, x_ref, xs_in_ref, xs_ref, sem):
    del xs_in_ref
    tm = x_ref.shape[0]

    def row_copy(r, k):
        slot = pos_ref[0, 0, k * tm + r]
        return pltpu.make_async_copy(x_ref.at[pl.ds(r, 1)], xs_ref.at[pl.ds(slot, 1)], sem)

    def start(r, carry):
        row_copy(r, 0).start()
        row_copy(r, 1).start()
        return carry

    def wait(r, carry):
        row_copy(r, 0).wait()
        row_copy(r, 1).wait()
        return carry

    lax.fori_loop(0, tm, start, 0, unroll=ROW_DMA_UNROLL)
    lax.fori_loop(0, tm, wait, 0, unroll=ROW_DMA_UNROLL)


def _gmm_kernel(te_ref, nu_ref, xs_ref, wg_ref, wu_ref, wd_ref, ys_ref, xb_ref, acc_ref):
    del te_ref
    j = pl.program_id(0)
    f = pl.program_id(1)

    @pl.when(jnp.logical_and(j >= nu_ref[0], f == 0))
    def _():
        ys_ref[...] = jnp.zeros_like(ys_ref)

    @pl.when(j < nu_ref[0])
    def _():
        @pl.when(f == 0)
        def _():
            xb_ref[...] = xs_ref[...].astype(BF16)
            acc_ref[...] = jnp.zeros_like(acc_ref)

        acc_ref[...] += _swiglu_partial(xb_ref[...], wg_ref[...], wu_ref[...], wd_ref[...])

        @pl.when(f == pl.num_programs(1) - 1)
        def _():
            ys_ref[...] = acc_ref[...]


def _moe_combine_kernel(pos_ref, route_ref, x_ref, lg_ref, lb_ref, ys_ref, o_ref, y1_ref, y2_ref, sem,
                        *, alpha):
    tm = x_ref.shape[0]

    def row_copy(r, k):
        slot = pos_ref[0, 0, k * tm + r]
        dst = (y1_ref, y2_ref)[k]
        return pltpu.make_async_copy(ys_ref.at[pl.ds(slot, 1)], dst.at[pl.ds(r, 1)], sem)

    def start(r, carry):
        row_copy(r, 0).start()
        row_copy(r, 1).start()
        return carry

    def wait(r, carry):
        row_copy(r, 0).wait()
        row_copy(r, 1).wait()
        return carry

    lax.fori_loop(0, tm, start, 0, unroll=ROW_DMA_UNROLL)
    lax.fori_loop(0, tm, wait, 0, unroll=ROW_DMA_UNROLL)
    route = route_ref[...]
    mixed = route[:, 2:3] * y1_ref[...] + route[:, 3:4] * y2_ref[...]
    o_ref[...] = _layer_norm(alpha * x_ref[...] + mixed, lg_ref[...], lb_ref[...])


def _moe(x, router, wg, wu, wd, ln_g, ln_b, alpha):
    t, d = x.shape
    n_experts, _, ff = wg.shape
    tm = _tile(t, 512)
    tr = MOE_TILE_ROWS
    n_tok_tiles = t // tm

    router_p = jnp.pad(router.astype(F32), ((0, 0), (0, V7X_LANES - n_experts)))
    route, counts = pl.pallas_call(
        functools.partial(_router_kernel, n_experts=n_experts),
        grid=(n_tok_tiles,),
        in_specs=[pl.BlockSpec((tm, d), lambda i: (i, 0)),
                  pl.BlockSpec((d, V7X_LANES), lambda i: (0, 0))],
        out_specs=[pl.BlockSpec((tm, ROUTE_COLS), lambda i: (i, 0)),
                   pl.BlockSpec((1, V7X_LANES), lambda i: (0, 0))],
        out_shape=[jax.ShapeDtypeStruct((t, ROUTE_COLS), F32), jax.ShapeDtypeStruct((1, V7X_LANES), F32)],
        scratch_shapes=[pltpu.VMEM((tm, tm), BF16)],
        compiler_params=_cparams("arbitrary"),
        name="moe_router",
    )(x, router_p)

    cnt = counts[0, :n_experts].astype(jnp.int32)
    padded = (cnt + tr - 1) // tr * tr
    ends = jnp.cumsum(padded)
    starts = ends - padded
    expert_ids = jnp.arange(n_experts, dtype=jnp.int32)

    def slots(expert_col, rank_col):
        onehot = route[:, expert_col:expert_col + 1].astype(jnp.int32) == expert_ids[None]
        return jnp.sum(jnp.where(onehot, starts[None], 0), axis=1) + route[:, rank_col].astype(jnp.int32)

    pos = jnp.concatenate([slots(0, 4).reshape(n_tok_tiles, 1, tm), slots(1, 5).reshape(n_tok_tiles, 1, tm)],
                          axis=2)
    n_tiles = (TOP_K * t) // tr + n_experts
    n_used = (ends[-1] // tr).reshape(1)
    tile_start = jnp.arange(n_tiles, dtype=jnp.int32) * tr
    tile_expert = jnp.sum((tile_start[:, None] >= ends[None]).astype(jnp.int32), axis=1)
    last_expert = jnp.sum((ends[-1] - tr >= ends).astype(jnp.int32))
    tile_expert = jnp.where(tile_start < ends[-1], tile_expert, last_expert)

    pos_spec = pl.BlockSpec((1, 1, 2 * tm), lambda i: (i, 0, 0), memory_space=pltpu.SMEM)
    any_spec = pl.BlockSpec(memory_space=pl.ANY)
    xs = pl.pallas_call(
        _moe_scatter_kernel,
        grid=(n_tok_tiles,),
        in_specs=[pos_spec, pl.BlockSpec((tm, d), lambda i: (i, 0)), any_spec],
        out_specs=any_spec,
        out_shape=jax.ShapeDtypeStruct((n_tiles * tr, d), F32),
        scratch_shapes=[pltpu.SemaphoreType.DMA(())],
        input_output_aliases={2: 0},
        compiler_params=_cparams("arbitrary"),
        name="moe_scatter",
    )(pos, x, jnp.zeros((n_tiles * tr, d), F32))

    tf = ff // 2 if (ff // 2) % V7X_LANES == 0 else ff
    nf = ff // tf
    used = lambda j, nu: jnp.minimum(j, nu[0] - 1)
    fcol = lambda j, f, nu: jnp.where(j < nu[0], f, nf - 1)
    ys = pl.pallas_call(
        _gmm_kernel,
        grid_spec=pltpu.PrefetchScalarGridSpec(
            num_scalar_prefetch=2,
            grid=(n_tiles, nf),
            in_specs=[pl.BlockSpec((tr, d), lambda j, f, te, nu: (used(j, nu), 0)),
                      pl.BlockSpec((None, d, tf), lambda j, f, te, nu: (te[j], 0, fcol(j, f, nu))),
                      pl.BlockSpec((None, d, tf), lambda j, f, te, nu: (te[j], 0, fcol(j, f, nu))),
                      pl.BlockSpec((None, tf, d), lambda j, f, te, nu: (te[j], fcol(j, f, nu), 0))],
            out_specs=pl.BlockSpec((tr, d), lambda j, f, te, nu: (j, 0)),
            scratch_shapes=[pltpu.VMEM((tr, d), BF16), pltpu.VMEM((tr, d), F32)]),
        out_shape=jax.ShapeDtypeStruct((n_tiles * tr, d), F32),
        compiler_params=_cparams("arbitrary", "arbitrary"),
        name="moe_experts",
    )(tile_expert, n_used, xs, wg, wu, wd)

    vec = pl.BlockSpec((1, d), lambda i: (0, 0))
    return pl.pallas_call(
        functools.partial(_moe_combine_kernel, alpha=alpha),
        grid=(n_tok_tiles,),
        in_specs=[pos_spec, pl.BlockSpec((tm, ROUTE_COLS), lambda i: (i, 0)),
                  pl.BlockSpec((tm, d), lambda i: (i, 0)), vec, vec, any_spec],
        out_specs=pl.BlockSpec((tm, d), lambda i: (i, 0)),
        out_shape=jax.ShapeDtypeStruct((t, d), F32),
        scratch_shapes=[pltpu.VMEM((tm, d), F32), pltpu.VMEM((tm, d), F32), pltpu.SemaphoreType.DMA(())],
        compiler_params=_cparams("arbitrary"),
        name="moe_combine",
    )(pos, route, x, ln_g.reshape(1, d), ln_b.reshape(1, d), ys)


def _run_trunk(x3, p, tables):
    bsz, seq, d = x3.shape
    depth = p['w_in'].shape[0]
    alpha = (2 * depth) ** 0.25
    att_w = ATT_HEADS * HEAD_W
    width = p['sc_conv_w'].shape[2]
    gate_w = 3 * d // 2
    assert att_w % width == 0 and (3 * att_w + 6 * width) % gate_w == 0, "column blocks must align"
    hy_col = att_w // width
    d_in = p['w_in'].shape[2]
    col_scale = jnp.asarray(np.where(np.arange(d_in) < att_w, LOG2E * ATT_HEAD_DIM ** -0.5, 1.0)
                            .astype(np.float32)[None])
    x = x3.reshape(bsz * seq, d)
    for i in range(depth):
        proj = _proj(x, p['w_in'][i], col_scale)
        proj3 = proj.reshape(bsz, seq, proj.shape[1])
        y_a = _attention(proj3, p['att_lambda'][i], p['att_subln_g'][i], i)
        u, x0, y_c = _convs(proj3, p['hy_short_w'][i], p['sc_conv_w'][i], hy_col=hy_col, sc_col=hy_col + 1)
        spec = _hyena_filter_spectrum(seq, tables, p['hf_w_in'][i], p['hf_b_in'][i], p['hf_w_mid'][i],
                                      p['hf_b_mid'][i], p['hf_freq'][i], p['hf_w_out'][i])
        y_h = _long_conv(u, x0, tables, spec, p['hy_bias'][i])
        flat = lambda a: a.reshape(bsz * seq, a.shape[2])
        x = _merge(flat(y_a), flat(y_h), flat(y_c), proj, x, p['w_branch'][i], p['w_out'][i],
                   p['ln1_g'][i], p['ln1_b'][i], alpha, gate_col=(3 * att_w + 6 * width) // gate_w)
        j = i // 2
        if i % 2 == 0:
            x = _ffn(x, p['ffn_w_gate'][j], p['ffn_w_up'][j], p['ffn_w_down'][j],
                     p['ln2_g'][i], p['ln2_b'][i], alpha)
        else:
            x = _moe(x, p['router_w'][j], p['moe_w_gate'][j], p['moe_w_up'][j], p['moe_w_down'][j],
                     p['ln2_g'][i], p['ln2_b'][i], alpha)
    return x.reshape(bsz, seq, d)


def kernel(x_prompt, x_sample, w_in, att_lambda, att_subln_g, hy_short_w, hf_w_in, hf_b_in,
           hf_w_mid, hf_b_mid, hf_freq, hf_w_out, hy_bias, sc_conv_w, w_branch, w_out,
           ln1_g, ln1_b, ln2_g, ln2_b, ffn_w_gate, ffn_w_up, ffn_w_down, router_w,
           moe_w_gate, moe_w_up, moe_w_down):
    bf = lambda a: a.astype(BF16)
    p = dict(w_in=bf(w_in), att_lambda=att_lambda, att_subln_g=att_subln_g, hy_short_w=hy_short_w,
             hf_w_in=hf_w_in, hf_b_in=hf_b_in, hf_w_mid=hf_w_mid, hf_b_mid=hf_b_mid,
             hf_freq=hf_freq, hf_w_out=hf_w_out, hy_bias=hy_bias, sc_conv_w=sc_conv_w,
             w_branch=bf(w_branch), w_out=bf(w_out), ln1_g=ln1_g, ln1_b=ln1_b, ln2_g=ln2_g, ln2_b=ln2_b,
             ffn_w_gate=bf(ffn_w_gate), ffn_w_up=bf(ffn_w_up), ffn_w_down=bf(ffn_w_down),
             router_w=router_w, moe_w_gate=bf(moe_w_gate), moe_w_up=bf(moe_w_up),
             moe_w_down=bf(moe_w_down))
    tables = {}
    outs = []
    for x3 in (x_prompt, x_sample):
        seq = x3.shape[1]
        if seq not in tables:
            tables[seq] = _dft_tables(seq)
        outs.append(_run_trunk(x3, p, tables[seq]))
    return tuple(outs)
```

```python
import functools
import math

import numpy as np
import jax
import jax.numpy as jnp
from jax import lax
from jax.experimental import pallas as pl
from jax.experimental.pallas import tpu as pltpu

F32 = jnp.float32
BF16 = jnp.bfloat16

ATT_HEADS = 4
ATT_HEAD_DIM = 64
HEAD_W = 2 * ATT_HEAD_DIM
HY_BANDS = 16
HY_TARGET = 1e-2
HY_FAST_PCT = 0.3
HY_SLOW_PCT = 1.5
TOP_K = 2
LN_EPS = 1e-5
RMS_EPS = 1e-5
LOG2E = math.log2(math.e)
ATT_ROW_SPLIT = 2

V7X_LANES = 128
V7X_BF16_SUBLANES = 16
V7X_VMEM_LIMIT = 56 * 1024 * 1024


def _cparams(*sem):
    return pltpu.CompilerParams(dimension_semantics=sem, vmem_limit_bytes=V7X_VMEM_LIMIT)


def _tile(n, pref):
    t = min(n, pref)
    while n % t:
        t //= 2
    return t


def _layer_norm(v, g, b):
    mu = jnp.mean(v, axis=-1, keepdims=True)
    d = v - mu
    var = jnp.mean(d * d, axis=-1, keepdims=True)
    return d * lax.rsqrt(var + LN_EPS) * g + b


def _proj_kernel(x_ref, w_ref, cs_ref, o_ref, xb_ref):
    @pl.when(pl.program_id(1) == 0)
    def _():
        xb_ref[...] = x_ref[...].astype(BF16)

    acc = jnp.dot(xb_ref[...], w_ref[...], preferred_element_type=F32)
    o_ref[...] = (acc * cs_ref[...]).astype(o_ref.dtype)


def _proj(x, w, col_scale):
    t, d = x.shape
    n = w.shape[1]
    tm = _tile(t, 1024)
    tn = _tile(n, 1536)
    return pl.pallas_call(
        _proj_kernel,
        grid=(t // tm, n // tn),
        in_specs=[pl.BlockSpec((tm, d), lambda i, j: (i, 0)),
                  pl.BlockSpec((d, tn), lambda i, j: (0, j)),
                  pl.BlockSpec((1, tn), lambda i, j: (0, j))],
        out_specs=pl.BlockSpec((tm, tn), lambda i, j: (i, j)),
        out_shape=jax.ShapeDtypeStruct((t, n), BF16),
        scratch_shapes=[pltpu.VMEM((tm, d), BF16)],
        compiler_params=_cparams("parallel", "arbitrary"),
        name="proj",
    )(x, w, col_scale)


def _attn_kernel(slopes_ref, lam_ref, g_ref, q_ref, k_ref, v_ref, o_ref, bias_ref, *, lam_init, tq):
    h = pl.program_id(0)
    b = pl.program_id(1)
    qi = pl.program_id(2)
    seq = k_ref.shape[1]

    @pl.when(jnp.logical_and(b == 0, qi == 0))
    def _():
        r = lax.broadcasted_iota(jnp.int32, bias_ref.shape, 0)
        j = lax.broadcasted_iota(jnp.int32, bias_ref.shape, 1)
        bias_ref[...] = (-LOG2E * slopes_ref[h]) * jnp.abs(r - j + (seq - tq)).astype(F32)

    start = pl.multiple_of((pl.num_programs(2) - 1 - qi) * tq, tq)
    lf = lam_ref[...].astype(F32)
    lam_full = (jnp.exp(jnp.sum(lf[0:1] * lf[1:2], axis=1, keepdims=True))
                - jnp.exp(jnp.sum(lf[2:3] * lf[3:4], axis=1, keepdims=True)) + lam_init)
    q = q_ref[0]
    k = k_ref[0]
    v = v_ref[0]
    bias = bias_ref[:, pl.ds(start, seq)]
    lane = lax.broadcasted_iota(jnp.int32, q.shape, 1)
    zero = jnp.zeros_like(q)

    def softmax_pv(qm):
        s = lax.dot_general(qm, k, (((1,), (1,)), ((), ())), preferred_element_type=F32)
        m = jnp.max(s + bias, axis=1, keepdims=True)
        e = jnp.exp2(s + (bias - m))
        l = jnp.sum(e, axis=1, keepdims=True)
        return jnp.dot(e.astype(BF16), v, preferred_element_type=F32) * (1.0 / l)

    o = (softmax_pv(jnp.where(lane < ATT_HEAD_DIM, q, zero))
         - lam_full * softmax_pv(jnp.where(lane >= ATT_HEAD_DIM, q, zero)))
    o = o * lax.rsqrt(jnp.mean(o * o, axis=-1, keepdims=True) + RMS_EPS) * g_ref[...].astype(F32)
    o_ref[0] = (o * (1.0 - lam_init)).astype(o_ref.dtype)


def _attention(proj3, lam, subln_g, layer):
    bsz, seq, _ = proj3.shape
    tq = _tile(seq, 256)
    lam_init = 0.8 - 0.6 * math.exp(-0.3 * layer)
    slopes = jnp.asarray(2.0 ** (-8.0 * np.arange(1, ATT_HEADS + 1, dtype=np.float32) / ATT_HEADS))
    kern = functools.partial(_attn_kernel, lam_init=lam_init, tq=tq)
    return pl.pallas_call(
        kern,
        grid=(ATT_HEADS, bsz, seq // tq),
        in_specs=[pl.BlockSpec(memory_space=pltpu.SMEM),
                  pl.BlockSpec((4, ATT_HEAD_DIM), lambda h, b, i: (0, 0)),
                  pl.BlockSpec((1, HEAD_W), lambda h, b, i: (0, 0)),
                  pl.BlockSpec((1, tq, HEAD_W), lambda h, b, i: (b, i, h)),
                  pl.BlockSpec((1, seq, HEAD_W), lambda h, b, i: (b, 0, ATT_HEADS + h)),
                  pl.BlockSpec((1, seq, HEAD_W), lambda h, b, i: (b, 0, 2 * ATT_HEADS + h))],
        out_specs=pl.BlockSpec((1, tq, HEAD_W), lambda h, b, i: (b, i, h)),
        out_shape=jax.ShapeDtypeStruct((bsz, seq, ATT_HEADS * HEAD_W), BF16),
        scratch_shapes=[pltpu.VMEM((tq, 2 * seq - tq), F32)],
        compiler_params=_cparams("arbitrary", "arbitrary", "arbitrary"),
        name="attn",
    )(slopes, lam, subln_g.reshape(1, HEAD_W), proj3, proj3, proj3)


def _conv3(x, prev_row, next_row, w):
    rows = x.shape[0]
    row = lax.broadcasted_iota(jnp.int32, (rows, 1), 0)
    x_prev = jnp.where(row == 0, prev_row, pltpu.roll(x, 1, axis=0))
    x_next = jnp.where(row == rows - 1, next_row, pltpu.roll(x, rows - 1, axis=0))
    return w[0:1] * x_prev + w[1:2] * x + w[2:3] * x_next


def _conv_kernel(hy_ref, hyp_ref, hyn_ref, sc_ref, scp_ref, scn_ref, hw_ref, sw_ref,
                 u_ref, x0_ref, yc_ref):
    i = pl.program_id(1)
    last = pl.num_programs(1) - 1
    width = u_ref.shape[2]
    has_prev = jnp.where(i > 0, 1.0, 0.0)
    has_next = jnp.where(i < last, 1.0, 0.0)
    hsub = hyp_ref.shape[1]

    hy = hy_ref[0].astype(F32)
    hy_prev = hyp_ref[0].astype(F32)[hsub - 1:hsub] * has_prev
    hy_next = hyn_ref[0].astype(F32)[0:1] * has_next
    hyc = _conv3(hy, hy_prev, hy_next, hw_ref[...].astype(F32))
    x0_ref[0] = hyc[:, :width].astype(x0_ref.dtype)
    u_ref[0] = (hyc[:, 2 * width:] * hyc[:, width:2 * width]).astype(u_ref.dtype)

    def gated(blk):
        return blk[:, width:2 * width] * blk[:, 2 * width:]

    sc = sc_ref[0].astype(F32)
    cx_prev = gated(scp_ref[0].astype(F32)[hsub - 1:hsub]) * has_prev
    cx_next = gated(scn_ref[0].astype(F32)[0:1]) * has_next
    conv = _conv3(gated(sc), cx_prev, cx_next, sw_ref[...].astype(F32))
    yc_ref[0] = (sc[:, :width] * conv).astype(yc_ref.dtype)


def _convs(proj3, hy_w, sc_w, hy_col, sc_col):
    bsz, seq, _ = proj3.shape
    width = sc_w.shape[1]
    tl = _tile(seq, 512)
    hs = V7X_BF16_SUBLANES
    nh = tl // hs
    nblk = seq // hs

    def main(col):
        return pl.BlockSpec((1, tl, 3 * width), lambda b, i: (b, i, col))

    def prev(col):
        return pl.BlockSpec((1, hs, 3 * width), lambda b, i: (b, jnp.maximum(i * nh - 1, 0), col))

    def nxt(col):
        return pl.BlockSpec((1, hs, 3 * width), lambda b, i: (b, jnp.minimum((i + 1) * nh, nblk - 1), col))

    out = jax.ShapeDtypeStruct((bsz, seq, width), BF16)
    ospec = pl.BlockSpec((1, tl, width), lambda b, i: (b, i, 0))
    return pl.pallas_call(
        _conv_kernel,
        grid=(bsz, seq // tl),
        in_specs=[main(hy_col), prev(hy_col), nxt(hy_col), main(sc_col), prev(sc_col), nxt(sc_col),
                  pl.BlockSpec((3, 3 * width), lambda b, i: (0, 0)),
                  pl.BlockSpec((3, width), lambda b, i: (0, 0))],
        out_specs=[ospec, ospec, ospec],
        out_shape=[out, out, out],
        compiler_params=_cparams("parallel", "arbitrary"),
        name="conv",
    )(proj3, proj3, proj3, proj3, proj3, proj3, hy_w, sc_w)


def _tab_kernel(c_ref, s_ref, st_ref, *, seq):
    tr = c_ref.shape[0]
    i = pl.program_id(0)
    r = i * tr + lax.broadcasted_iota(jnp.int32, (tr, seq), 0)
    c = lax.broadcasted_iota(jnp.int32, (tr, seq), 1)
    ang = ((r * c) & (2 * seq - 1)).astype(F32) * (math.pi / seq)
    cosv = jnp.cos(ang)
    nsin = -jnp.sin(ang)
    c_ref[...] = cosv.astype(c_ref.dtype)
    s_ref[...] = jnp.where(r == 0, jnp.where((c & 1) == 0, 1.0, -1.0), nsin).astype(s_ref.dtype)
    st_ref[...] = jnp.where(c == 0, jnp.where((r & 1) == 0, 1.0, -1.0), nsin).astype(st_ref.dtype)


def _dft_tables(seq):
    assert seq & (seq - 1) == 0, "sequence length must be a power of two"
    tr = _tile(seq, 256)
    spec = pl.BlockSpec((tr, seq), lambda i: (i, 0))
    out = jax.ShapeDtypeStruct((seq, seq), BF16)
    return pl.pallas_call(
        functools.partial(_tab_kernel, seq=seq),
        grid=(seq // tr,),
        out_specs=[spec, spec, spec],
        out_shape=[out, out, out],
        compiler_params=_cparams("parallel"),
        name="dft_tables",
    )()


def _filt_kernel(z_ref, win_ref, bin_ref, wmid_ref, bmid_ref, fr_ref, wout_ref, dl_ref,
                 hs_ref, l1_ref, *, seq):
    i = pl.program_id(0)
    tl = z_ref.shape[0]
    width = dl_ref.shape[1]
    hi = lax.Precision.HIGHEST
    fr = fr_ref[...]
    h = jnp.sin(fr * (jnp.dot(z_ref[...], win_ref[...], precision=hi, preferred_element_type=F32)
                      + bin_ref[...]))
    for j in range(wmid_ref.shape[0]):
        h = jnp.sin(fr * (jnp.dot(h, wmid_ref[j], precision=hi, preferred_element_type=F32)
                          + bmid_ref[j]))
    ho = jnp.dot(h, wout_ref[...], precision=hi, preferred_element_type=F32)
    pos = i * tl + lax.broadcasted_iota(jnp.int32, (tl, 1), 0)
    t = pos.astype(F32) / float(seq - 1)
    window = jnp.exp(-t * dl_ref[...])
    h_fwd = ho[:, :width] * window
    h_bwd = jnp.where(pos == 0, 0.0, ho[:, width:] * window)
    hs_ref[:, :width] = h_fwd + h_bwd
    hs_ref[:, width:] = h_fwd - h_bwd

    @pl.when(i == 0)
    def _():
        l1_ref[...] = jnp.zeros_like(l1_ref)

    l1_ref[...] += jnp.sum(jnp.abs(h_fwd) + jnp.abs(h_bwd), axis=0, keepdims=True)


def _fspec_kernel(c_ref, s_ref, hs_ref, l1_ref, a_ref, b_ref, d_ref, *, seq):
    i = pl.program_id(0)
    tk = c_ref.shape[0]
    width = l1_ref.shape[1]
    hs = hs_ref[...].astype(BF16)
    xc = jnp.dot(c_ref[...], hs[:, :width], preferred_element_type=F32)
    xs = jnp.dot(s_ref[...], hs, preferred_element_type=F32)
    k = i * tk + lax.broadcasted_iota(jnp.int32, (tk, 1), 0)
    inv_l1 = 1.0 / l1_ref[...]
    wk = jnp.where(k == 0, 1.0, 2.0) / float(2 * seq)
    a = xc * wk * inv_l1
    a_ref[...] = a
    b_ref[...] = jnp.where(k == 0, 0.0, xs[:, width:] * wk * inv_l1)
    d_ref[...] = jnp.where(k == 0, xs[:, :width] * wk * inv_l1, a)


def _hyena_filter_spectrum(seq, tabs, w_in, b_in, w_mid, b_mid, freq, w_out):
    c_tab, s_tab, _ = tabs
    emb, fw = w_in.shape
    width = w_out.shape[1] // 2
    n = np.arange(seq, dtype=np.float64)[:, None]
    f = np.linspace(1e-4, HY_BANDS - 1, HY_BANDS)[None]
    w = 2.0 * math.pi * n / seq
    z = np.concatenate([n / (seq - 1), np.cos(f * w), -np.sin(f * w)], -1)
    z = np.pad(z, ((0, 0), (0, V7X_LANES - emb))).astype(np.float32)
    w_in_p = jnp.pad(w_in.astype(F32), ((0, V7X_LANES - emb), (0, 0)))
    deltas = np.abs(np.linspace(math.log(HY_TARGET) / HY_SLOW_PCT, math.log(HY_TARGET) / HY_FAST_PCT,
                                width)).astype(np.float32)[None]
    tl = _tile(seq, 512)
    full = lambda *shape: pl.BlockSpec(shape, lambda i: (0,) * len(shape))
    hs, l1 = pl.pallas_call(
        functools.partial(_filt_kernel, seq=seq),
        grid=(seq // tl,),
        in_specs=[pl.BlockSpec((tl, V7X_LANES), lambda i: (i, 0)),
                  full(V7X_LANES, fw), full(1, fw), full(*w_mid.shape), full(w_mid.shape[0], 1, fw),
                  full(1, fw), full(fw, 2 * width), full(1, width)],
        out_specs=[pl.BlockSpec((tl, 2 * width), lambda i: (i, 0)), full(1, width)],
        out_shape=[jax.ShapeDtypeStruct((seq, 2 * width), F32), jax.ShapeDtypeStruct((1, width), F32)],
        compiler_params=_cparams("arbitrary"),
        name="hyena_filter",
    )(jnp.asarray(z), w_in_p, b_in.reshape(1, fw).astype(F32), w_mid.astype(F32),
      b_mid.reshape(w_mid.shape[0], 1, fw).astype(F32), freq.reshape(1, fw).astype(F32),
      w_out.astype(F32), jnp.asarray(deltas))

    tk = _tile(seq, 256)
    ospec = pl.BlockSpec((tk, width), lambda i: (i, 0))
    out = jax.ShapeDtypeStruct((seq, width), F32)
    return pl.pallas_call(
        functools.partial(_fspec_kernel, seq=seq),
        grid=(seq // tk,),
        in_specs=[pl.BlockSpec((tk, seq), lambda i: (i, 0)), pl.BlockSpec((tk, seq), lambda i: (i, 0)),
                  full(seq, 2 * width), full(1, width)],
        out_specs=[ospec, ospec, ospec],
        out_shape=[out, out, out],
        compiler_params=_cparams("parallel"),
        name="filter_spectrum",
    )(c_tab, s_tab, hs, l1)


def _dft_fwd_kernel(c_ref, s_ref, u_ref, a_ref, b_ref, d_ref, yre_ref, yim_ref):
    u = u_ref[0]
    x_re = jnp.dot(c_ref[...], u, preferred_element_type=F32)
    x_im = jnp.dot(s_ref[...], u, preferred_element_type=F32)
    b = b_ref[...]
    yre_ref[0] = (x_re * a_ref[...] - x_im * b).astype(yre_ref.dtype)
    yim_ref[0] = (x_re * b + x_im * d_ref[...]).astype(yim_ref.dtype)


def _dft_inv_kernel(c_ref, st_ref, yre_ref, yim_ref, u_ref, x0_ref, bias_ref, o_ref):
    y = (jnp.dot(c_ref[...], yre_ref[0], preferred_element_type=F32)
         + jnp.dot(st_ref[...], yim_ref[0], preferred_element_type=F32))
    y = y + u_ref[0].astype(F32) * bias_ref[...]
    o_ref[0] = (x0_ref[0].astype(F32) * y).astype(o_ref.dtype)


def _long_conv(u, x0, tabs, spec, bias):
    c_tab, s_tab, st_tab = tabs
    a, b, d = spec
    bsz, seq, width = u.shape
    tk = _tile(seq, 512)
    tab = pl.BlockSpec((tk, seq), lambda j, bb: (j, 0))
    whole = pl.BlockSpec((1, seq, width), lambda j, bb: (bb, 0, 0))
    rows = pl.BlockSpec((1, tk, width), lambda j, bb: (bb, j, 0))
    coef = pl.BlockSpec((tk, width), lambda j, bb: (j, 0))
    spec_out = jax.ShapeDtypeStruct((bsz, seq, width), BF16)
    y_re, y_im = pl.pallas_call(
        _dft_fwd_kernel,
        grid=(seq // tk, bsz),
        in_specs=[tab, tab, whole, coef, coef, coef],
        out_specs=[rows, rows],
        out_shape=[spec_out, spec_out],
        compiler_params=_cparams("parallel", "arbitrary"),
        name="dft_fwd",
    )(c_tab, s_tab, u, a, b, d)
    return pl.pallas_call(
        _dft_inv_kernel,
        grid=(seq // tk, bsz),
        in_specs=[tab, tab, whole, whole, rows, rows, pl.BlockSpec((1, width), lambda j, bb: (0, 0))],
        out_specs=rows,
        out_shape=spec_out,
        compiler_params=_cparams("parallel", "arbitrary"),
        name="dft_inv",
    )(c_tab, st_tab, y_re, y_im, u, x0, bias.reshape(1, width).astype(F32))


def _merge_kernel(ya_ref, yh_ref, yc_ref, g0_ref, g1_ref, x_ref, wb_ref, wo_ref, lg_ref, lb_ref,
                  o_ref, m_ref, *, alpha):
    d = o_ref.shape[1]
    half = d // 2
    g0 = g0_ref[...]
    g1 = g1_ref[...]
    gates = ((g0[:, :half], g0[:, d:], g1[:, half:d]),
             (g0[:, half:d], g1[:, :half], g1[:, d:]))
    branches = (ya_ref[...], yh_ref[...], yc_ref[...])
    for c in range(2):
        acc = None
        for j in range(3):
            y = jnp.dot(branches[j], wb_ref[j, :, c * half:(c + 1) * half], preferred_element_type=F32)
            term = jax.nn.sigmoid(gates[c][j].astype(F32)) * y
            acc = term if acc is None else acc + term
        m_ref[:, c * half:(c + 1) * half] = acc.astype(m_ref.dtype)
    t = jnp.dot(m_ref[...], wo_ref[...], preferred_element_type=F32)
    o_ref[...] = _layer_norm(alpha * x_ref[...] + t, lg_ref[...], lb_ref[...])


def _merge(y_a, y_h, y_c, proj, x, wb, wo, ln_g, ln_b, alpha, gate_col):
    t, d = x.shape
    bw = y_a.shape[1]
    tm = _tile(t, 512)
    row = lambda w: pl.BlockSpec((tm, w), lambda i: (i, 0))
    gw = 3 * d // 2
    return pl.pallas_call(
        functools.partial(_merge_kernel, alpha=alpha),
        grid=(t // tm,),
        in_specs=[row(bw), row(bw), row(bw),
                  pl.BlockSpec((tm, gw), lambda i: (i, gate_col)),
                  pl.BlockSpec((tm, gw), lambda i: (i, gate_col + 1)),
                  row(d),
                  pl.BlockSpec(wb.shape, lambda i: (0, 0, 0)),
                  pl.BlockSpec(wo.shape, lambda i: (0, 0)),
                  pl.BlockSpec((1, d), lambda i: (0, 0)), pl.BlockSpec((1, d), lambda i: (0, 0))],
        out_specs=row(d),
        out_shape=jax.ShapeDtypeStruct((t, d), F32),
        scratch_shapes=[pltpu.VMEM((tm, d), BF16)],
        compiler_params=_cparams("parallel"),
        name="merge",
    )(y_a, y_h, y_c, proj, proj, x, wb, wo, ln_g.reshape(1, d), ln_b.reshape(1, d))


def _swiglu_partial(xb, wg, wu, wd):
    g = jnp.dot(xb, wg, preferred_element_type=F32)
    u = jnp.dot(xb, wu, preferred_element_type=F32)
    return jnp.dot((jax.nn.silu(g) * u).astype(BF16), wd, preferred_element_type=F32)


def _ffn_kernel(x_ref, wg_ref, wu_ref, wd_ref, lg_ref, lb_ref, o_ref, xb_ref, acc_ref, *, alpha):
    f = pl.program_id(1)

    @pl.when(f == 0)
    def _():
        xb_ref[...] = x_ref[...].astype(BF16)
        acc_ref[...] = jnp.zeros_like(acc_ref)

    acc_ref[...] += _swiglu_partial(xb_ref[...], wg_ref[...], wu_ref[...], wd_ref[...])

    @pl.when(f == pl.num_programs(1) - 1)
    def _():
        o_ref[...] = _layer_norm(alpha * x_ref[...] + acc_ref[...], lg_ref[...], lb_ref[...])


def _ffn(x, wg, wu, wd, ln_g, ln_b, alpha):
    t, d = x.shape
    ff = wg.shape[1]
    tm = _tile(t, 1024)
    tf = ff // 2 if (ff // 2) % V7X_LANES == 0 else ff
    vec = pl.BlockSpec((1, d), lambda i, f: (0, 0))
    return pl.pallas_call(
        functools.partial(_ffn_kernel, alpha=alpha),
        grid=(t // tm, ff // tf),
        in_specs=[pl.BlockSpec((tm, d), lambda i, f: (i, 0)),
                  pl.BlockSpec((d, tf), lambda i, f: (0, f)),
                  pl.BlockSpec((d, tf), lambda i, f: (0, f)),
                  pl.BlockSpec((tf, d), lambda i, f: (f, 0)),
                  vec, vec],
        out_specs=pl.BlockSpec((tm, d), lambda i, f: (i, 0)),
        out_shape=jax.ShapeDtypeStruct((t, d), F32),
        scratch_shapes=[pltpu.VMEM((tm, d), BF16), pltpu.VMEM((tm, d), F32)],
        compiler_params=_cparams("parallel", "arbitrary"),
        name="ffn",
    )(x, wg, wu, wd, ln_g.reshape(1, d), ln_b.reshape(1, d))


def _top2(logits, n_experts):
    lane = lax.broadcasted_iota(jnp.int32, logits.shape, 1).astype(F32)
    neg = jnp.float32(-jnp.inf)
    sentinel = jnp.float32(n_experts)
    lg = jnp.where(lane < sentinel, logits, neg)
    m1 = jnp.max(lg, axis=1, keepdims=True)
    i1 = jnp.min(jnp.where(lg == m1, lane, sentinel), axis=1, keepdims=True)
    rest = jnp.where(lane == i1, neg, lg)
    m2 = jnp.max(rest, axis=1, keepdims=True)
    i2 = jnp.min(jnp.where(rest == m2, lane, sentinel), axis=1, keepdims=True)
    e2 = jnp.exp(m2 - m1)
    p1 = 1.0 / (1.0 + e2)
    p2 = e2 / (1.0 + e2)
    return lane == i1, lane == i2, i1, i2, p1, p2


ROUTE_COLS = 8
MOE_TILE_ROWS = 512
ROW_DMA_UNROLL = 8


def _router_kernel(x_ref, r_ref, route_ref, cnt_ref, tri_ref, *, n_experts):
    i = pl.program_id(0)
    tm = x_ref.shape[0]

    @pl.when(i == 0)
    def _():
        cnt_ref[...] = jnp.zeros_like(cnt_ref)
        row = lax.broadcasted_iota(jnp.int32, (tm, tm), 0)
        col = lax.broadcasted_iota(jnp.int32, (tm, tm), 1)
        tri_ref[...] = jnp.where(col < row, 1.0, 0.0).astype(tri_ref.dtype)

    logits = jnp.dot(x_ref[...], r_ref[...], precision=lax.Precision.HIGHEST, preferred_element_type=F32)
    sel1, sel2, i1, i2, p1, p2 = _top2(logits, n_experts)
    chosen = jnp.where(sel1, 1.0, jnp.where(sel2, 1.0, 0.0))
    before = jnp.dot(tri_ref[...], chosen.astype(tri_ref.dtype), preferred_element_type=F32) + cnt_ref[...]
    r1 = jnp.sum(jnp.where(sel1, before, 0.0), axis=1, keepdims=True)
    r2 = jnp.sum(jnp.where(sel2, before, 0.0), axis=1, keepdims=True)
    cnt_ref[...] += jnp.sum(chosen, axis=0, keepdims=True)
    lane = lax.broadcasted_iota(jnp.int32, logits.shape, 1)
    record = jnp.zeros_like(logits)
    for c, val in enumerate((i1, i2, p1, p2, r1, r2)):
        record = jnp.where(lane == c, val, record)
    route_ref[...] = record[:, :ROUTE_COLS]


def _moe_scatter_kernel(pos_ref, x_ref, xs_in_ref, xs_ref, sem):
    del xs_in_ref
    tm = x_ref.shape[0]

    def row_copy(r, k):
        slot = pos_ref[0, 0, k * tm + r]
        return pltpu.make_async_copy(x_ref.at[pl.ds(r, 1)], xs_ref.at[pl.ds(slot, 1)], sem)

    def start(r, carry):
        row_copy(r, 0).start()
        row_copy(r, 1).start()
        return carry

    def wait(r, carry):
        row_copy(r, 0).wait()
        row_copy(r, 1).wait()
        return carry

    lax.fori_loop(0, tm, start, 0, unroll=ROW_DMA_UNROLL)
    lax.fori_loop(0, tm, wait, 0, unroll=ROW_DMA_UNROLL)


def _gmm_kernel(te_ref, nu_ref, xs_ref, wg_ref, wu_ref, wd_ref, ys_ref, xb_ref, acc_ref):
    del te_ref
    j = pl.program_id(0)
    f = pl.program_id(1)

    @pl.when(jnp.logical_and(j >= nu_ref[0], f == 0))
    def _():
        ys_ref[...] = jnp.zeros_like(ys_ref)

    @pl.when(j < nu_ref[0])
    def _():
        @pl.when(f == 0)
        def _():
            xb_ref[...] = xs_ref[...].astype(BF16)
            acc_ref[...] = jnp.zeros_like(acc_ref)

        acc_ref[...] += _swiglu_partial(xb_ref[...], wg_ref[...], wu_ref[...], wd_ref[...])

        @pl.when(f == pl.num_programs(1) - 1)
        def _():
            ys_ref[...] = acc_ref[...]


def _moe_combine_kernel(pos_ref, route_ref, x_ref, lg_ref, lb_ref, ys_ref, o_ref, y1_ref, y2_ref, sem,
                        *, alpha):
    tm = x_ref.shape[0]

    def row_copy(r, k):
        slot = pos_ref[0, 0, k * tm + r]
        dst = (y1_ref, y2_ref)[k]
        return pltpu.make_async_copy(ys_ref.at[pl.ds(slot, 1)], dst.at[pl.ds(r, 1)], sem)

    def start(r, carry):
        row_copy(r, 0).start()
        row_copy(r, 1).start()
        return carry

    def wait(r, carry):
        row_copy(r, 0).wait()
        row_copy(r, 1).wait()
        return carry

    lax.fori_loop(0, tm, start, 0, unroll=ROW_DMA_UNROLL)
    lax.fori_loop(0, tm, wait, 0, unroll=ROW_DMA_UNROLL)
    route = route_ref[...]
    mixed = route[:, 2:3] * y1_ref[...] + route[:, 3:4] * y2_ref[...]
    o_ref[...] = _layer_norm(alpha * x_ref[...] + mixed, lg_ref[...], lb_ref[...])


def _moe(x, router, wg, wu, wd, ln_g, ln_b, alpha):
    t, d = x.shape
    n_experts, _, ff = wg.shape
    tm = _tile(t, 512)
    tr = MOE_TILE_ROWS
    n_tok_tiles = t // tm

    router_p = jnp.pad(router.astype(F32), ((0, 0), (0, V7X_LANES - n_experts)))
    route, counts = pl.pallas_call(
        functools.partial(_router_kernel, n_experts=n_experts),
        grid=(n_tok_tiles,),
        in_specs=[pl.BlockSpec((tm, d), lambda i: (i, 0)),
                  pl.BlockSpec((d, V7X_LANES), lambda i: (0, 0))],
        out_specs=[pl.BlockSpec((tm, ROUTE_COLS), lambda i: (i, 0)),
                   pl.BlockSpec((1, V7X_LANES), lambda i: (0, 0))],
        out_shape=[jax.ShapeDtypeStruct((t, ROUTE_COLS), F32), jax.ShapeDtypeStruct((1, V7X_LANES), F32)],
        scratch_shapes=[pltpu.VMEM((tm, tm), BF16)],
        compiler_params=_cparams("arbitrary"),
        name="moe_router",
    )(x, router_p)

    cnt = counts[0, :n_experts].astype(jnp.int32)
    padded = (cnt + tr - 1) // tr * tr
    ends = jnp.cumsum(padded)
    starts = ends - padded
    expert_ids = jnp.arange(n_experts, dtype=jnp.int32)

    def slots(expert_col, rank_col):
        onehot = route[:, expert_col:expert_col + 1].astype(jnp.int32) == expert_ids[None]
        return jnp.sum(jnp.where(onehot, starts[None], 0), axis=1) + route[:, rank_col].astype(jnp.int32)

    pos = jnp.concatenate([slots(0, 4).reshape(n_tok_tiles, 1, tm), slots(1, 5).reshape(n_tok_tiles, 1, tm)],
                          axis=2)
    n_tiles = (TOP_K * t) // tr + n_experts
    n_used = (ends[-1] // tr).reshape(1)
    tile_start = jnp.arange(n_tiles, dtype=jnp.int32) * tr
    tile_expert = jnp.sum((tile_start[:, None] >= ends[None]).astype(jnp.int32), axis=1)
    last_expert = jnp.sum((ends[-1] - tr >= ends).astype(jnp.int32))
    tile_expert = jnp.where(tile_start < ends[-1], tile_expert, last_expert)

    pos_spec = pl.BlockSpec((1, 1, 2 * tm), lambda i: (i, 0, 0), memory_space=pltpu.SMEM)
    any_spec = pl.BlockSpec(memory_space=pl.ANY)
    xs = pl.pallas_call(
        _moe_scatter_kernel,
        grid=(n_tok_tiles,),
        in_specs=[pos_spec, pl.BlockSpec((tm, d), lambda i: (i, 0)), any_spec],
        out_specs=any_spec,
        out_shape=jax.ShapeDtypeStruct((n_tiles * tr, d), F32),
        scratch_shapes=[pltpu.SemaphoreType.DMA(())],
        input_output_aliases={2: 0},
        compiler_params=_cparams("arbitrary"),
        name="moe_scatter",
    )(pos, x, jnp.zeros((n_tiles * tr, d), F32))

    tf = ff // 2 if (ff // 2) % V7X_LANES == 0 else ff
    nf = ff // tf
    used = lambda j, nu: jnp.minimum(j, nu[0] - 1)
    fcol = lambda j, f, nu: jnp.where(j < nu[0], f, nf - 1)
    ys = pl.pallas_call(
        _gmm_kernel,
        grid_spec=pltpu.PrefetchScalarGridSpec(
            num_scalar_prefetch=2,
            grid=(n_tiles, nf),
            in_specs=[pl.BlockSpec((tr, d), lambda j, f, te, nu: (used(j, nu), 0)),
                      pl.BlockSpec((None, d, tf), lambda j, f, te, nu: (te[j], 0, fcol(j, f, nu))),
                      pl.BlockSpec((None, d, tf), lambda j, f, te, nu: (te[j], 0, fcol(j, f, nu))),
                      pl.BlockSpec((None, tf, d), lambda j, f, te, nu: (te[j], fcol(j, f, nu), 0))],
            out_specs=pl.BlockSpec((tr, d), lambda j, f, te, nu: (j, 0)),
            scratch_shapes=[pltpu.VMEM((tr, d), BF16), pltpu.VMEM((tr, d), F32)]),
        out_shape=jax.ShapeDtypeStruct((n_tiles * tr, d), F32),
        compiler_params=_cparams("arbitrary", "arbitrary"),
        name="moe_experts",
    )(tile_expert, n_used, xs, wg, wu, wd)

    vec = pl.BlockSpec((1, d), lambda i: (0, 0))
    return pl.pallas_call(
        functools.partial(_moe_combine_kernel, alpha=alpha),
        grid=(n_tok_tiles,),
        in_specs=[pos_spec, pl.BlockSpec((tm, ROUTE_COLS), lambda i: (i, 0)),
                  pl.BlockSpec((tm, d), lambda i: (i, 0)), vec, vec, any_spec],
        out_specs=pl.BlockSpec((tm, d), lambda i: (i, 0)),
        out_shape=jax.ShapeDtypeStruct((t, d), F32),
        scratch_shapes=[pltpu.VMEM((tm, d), F32), pltpu.VMEM((tm, d), F32), pltpu.SemaphoreType.DMA(())],
        compiler_params=_cparams("arbitrary"),
        name="moe_combine",
    )(pos, route, x, ln_g.reshape(1, d), ln_b.reshape(1, d), ys)


def _run_trunk(x3, p, tables):
    bsz, seq, d = x3.shape
    depth = p['w_in'].shape[0]
    alpha = (2 * depth) ** 0.25
    att_w = ATT_HEADS * HEAD_W
    width = p['sc_conv_w'].shape[2]
    gate_w = 3 * d // 2
    assert att_w % width == 0 and (3 * att_w + 6 * width) % gate_w == 0, "column blocks must align"
    hy_col = att_w // width
    d_in = p['w_in'].shape[2]
    col_scale = jnp.asarray(np.where(np.arange(d_in) < att_w, LOG2E * ATT_HEAD_DIM ** -0.5, 1.0)
                            .astype(np.float32)[None])
    x = x3.reshape(bsz * seq, d)
    for i in range(depth):
        proj = _proj(x, p['w_in'][i], col_scale)
        proj3 = proj.reshape(bsz, seq, proj.shape[1])
        y_a = _attention(proj3, p['att_lambda'][i], p['att_subln_g'][i], i)
        u, x0, y_c = _convs(proj3, p['hy_short_w'][i], p['sc_conv_w'][i], hy_col=hy_col, sc_col=hy_col + 1)
        spec = _hyena_filter_spectrum(seq, tables, p['hf_w_in'][i], p['hf_b_in'][i], p['hf_w_mid'][i],
                                      p['hf_b_mid'][i], p['hf_freq'][i], p['hf_w_out'][i])
        y_h = _long_conv(u, x0, tables, spec, p['hy_bias'][i])
        flat = lambda a: a.reshape(bsz * seq, a.shape[2])
        x = _merge(flat(y_a), flat(y_h), flat(y_c), proj, x, p['w_branch'][i], p['w_out'][i],
                   p['ln1_g'][i], p['ln1_b'][i], alpha, gate_col=(3 * att_w + 6 * width) // gate_w)
        j = i // 2
        if i % 2 == 0:
            x = _ffn(x, p['ffn_w_gate'][j], p['ffn_w_up'][j], p['ffn_w_down'][j],
                     p['ln2_g'][i], p['ln2_b'][i], alpha)
        else:
            x = _moe(x, p['router_w'][j], p['moe_w_gate'][j], p['moe_w_up'][j], p['moe_w_down'][j],
                     p['ln2_g'][i], p['ln2_b'][i], alpha)
    return x.reshape(bsz, seq, d)


def kernel(x_prompt, x_sample, w_in, att_lambda, att_subln_g, hy_short_w, hf_w_in, hf_b_in,
           hf_w_mid, hf_b_mid, hf_freq, hf_w_out, hy_bias, sc_conv_w, w_branch, w_out,
           ln1_g, ln1_b, ln2_g, ln2_b, ffn_w_gate, ffn_w_up, ffn_w_down, router_w,
           moe_w_gate, moe_w_up, moe_w_down):
    bf = lambda a: a.astype(BF16)
    p = dict(w_in=bf(w_in), att_lambda=att_lambda, att_subln_g=att_subln_g, hy_short_w=hy_short_w,
             hf_w_in=hf_w_in, hf_b_in=hf_b_in, hf_w_mid=hf_w_mid, hf_b_mid=hf_b_mid,
             hf_freq=hf_freq, hf_w_out=hf_w_out, hy_bias=hy_bias, sc_conv_w=sc_conv_w,
             w_branch=bf(w_branch), w_out=bf(w_out), ln1_g=ln1_g, ln1_b=ln1_b, ln2_g=ln2_g, ln2_b=ln2_b,
             ffn_w_gate=bf(ffn_w_gate), ffn_w_up=bf(ffn_w_up), ffn_w_down=bf(ffn_w_down),
             router_w=router_w, moe_w_gate=bf(moe_w_gate), moe_w_up=bf(moe_w_up),
             moe_w_down=bf(moe_w_down))
    tables = {}
    outs = []
    for x3 in (x_prompt, x_sample):
        seq = x3.shape[1]
        if seq not in tables:
            tables[seq] = _dft_tables(seq)
        outs.append(_run_trunk(x3, p, tables[seq]))
    return tuple(outs)
```

```python
import functools
import math

import numpy as np
import jax
import jax.numpy as jnp
from jax import lax
from jax.experimental import pallas as pl
from jax.experimental.pallas import tpu as pltpu

F32 = jnp.float32
BF16 = jnp.bfloat16

ATT_HEADS = 4
ATT_HEAD_DIM = 64
HEAD_W = 2 * ATT_HEAD_DIM
HY_BANDS = 16
HY_TARGET = 1e-2
HY_FAST_PCT = 0.3
HY_SLOW_PCT = 1.5
TOP_K = 2
LN_EPS = 1e-5
RMS_EPS = 1e-5
LOG2E = math.log2(math.e)
ATT_Q_TILE = 256

V7X_LANES = 128
V7X_BF16_SUBLANES = 16
V7X_VMEM_LIMIT = 56 * 1024 * 1024


def _cparams(*sem):
    return pltpu.CompilerParams(dimension_semantics=sem, vmem_limit_bytes=V7X_VMEM_LIMIT)


def _tile(n, pref):
    t = min(n, pref)
    while n % t:
        t //= 2
    return t


def _layer_norm(v, g, b):
    mu = jnp.mean(v, axis=-1, keepdims=True)
    d = v - mu
    var = jnp.mean(d * d, axis=-1, keepdims=True)
    return d * lax.rsqrt(var + LN_EPS) * g + b


def _proj_kernel(x_ref, w_ref, cs_ref, o_ref, xb_ref):
    @pl.when(pl.program_id(1) == 0)
    def _():
        xb_ref[...] = x_ref[...].astype(BF16)

    acc = jnp.dot(xb_ref[...], w_ref[...], preferred_element_type=F32)
    o_ref[...] = (acc * cs_ref[...]).astype(o_ref.dtype)


def _proj(x, w, col_scale):
    t, d = x.shape
    n = w.shape[1]
    tm = _tile(t, 1024)
    tn = _tile(n, 1536)
    return pl.pallas_call(
        _proj_kernel,
        grid=(t // tm, n // tn),
        in_specs=[pl.BlockSpec((tm, d), lambda i, j: (i, 0)),
                  pl.BlockSpec((d, tn), lambda i, j: (0, j)),
                  pl.BlockSpec((1, tn), lambda i, j: (0, j))],
        out_specs=pl.BlockSpec((tm, tn), lambda i, j: (i, j)),
        out_shape=jax.ShapeDtypeStruct((t, n), BF16),
        scratch_shapes=[pltpu.VMEM((tm, d), BF16)],
        compiler_params=_cparams("parallel", "arbitrary"),
        name="proj",
    )(x, w, col_scale)


def _attn_kernel(slopes_ref, lam_ref, g_ref, q_ref, k_ref, v_ref, o_ref, bias_ref, *, lam_init, tq):
    h = pl.program_id(0)
    b = pl.program_id(1)
    qi = pl.program_id(2)
    seq = k_ref.shape[1]

    @pl.when(jnp.logical_and(b == 0, qi == 0))
    def _():
        r = lax.broadcasted_iota(jnp.int32, bias_ref.shape, 0)
        j = lax.broadcasted_iota(jnp.int32, bias_ref.shape, 1)
        bias_ref[...] = (-LOG2E * slopes_ref[h]) * jnp.abs(r - j + (seq - tq)).astype(F32)

    start = pl.multiple_of((pl.num_programs(2) - 1 - qi) * tq, tq)
    lf = lam_ref[...].astype(F32)
    lam_full = (jnp.exp(jnp.sum(lf[0:1] * lf[1:2], axis=1, keepdims=True))
                - jnp.exp(jnp.sum(lf[2:3] * lf[3:4], axis=1, keepdims=True)) + lam_init)
    q = q_ref[0]
    k = k_ref[0]
    v = v_ref[0]
    bias = bias_ref[:, pl.ds(start, seq)]
    lane = lax.broadcasted_iota(jnp.int32, q.shape, 1)
    zero = jnp.zeros_like(q)

    def softmax_pv(qm):
        s = lax.dot_general(qm, k, (((1,), (1,)), ((), ())), preferred_element_type=F32)
        m = jnp.max(s + bias, axis=1, keepdims=True)
        e = jnp.exp2(s + (bias - m))
        l = jnp.sum(e, axis=1, keepdims=True)
        return jnp.dot(e.astype(BF16), v, preferred_element_type=F32) * (1.0 / l)

    o = (softmax_pv(jnp.where(lane < ATT_HEAD_DIM, q, zero))
         - lam_full * softmax_pv(jnp.where(lane >= ATT_HEAD_DIM, q, zero)))
    o = o * lax.rsqrt(jnp.mean(o * o, axis=-1, keepdims=True) + RMS_EPS) * g_ref[...].astype(F32)
    o_ref[0] = (o * (1.0 - lam_init)).astype(o_ref.dtype)


def _attention(proj3, lam, subln_g, layer):
    bsz, seq, _ = proj3.shape
    tq = _tile(seq, ATT_Q_TILE)
    lam_init = 0.8 - 0.6 * math.exp(-0.3 * layer)
    slopes = jnp.asarray(2.0 ** (-8.0 * np.arange(1, ATT_HEADS + 1, dtype=np.float32) / ATT_HEADS))
    kern = functools.partial(_attn_kernel, lam_init=lam_init, tq=tq)
    return pl.pallas_call(
        kern,
        grid=(ATT_HEADS, bsz, seq // tq),
        in_specs=[pl.BlockSpec(memory_space=pltpu.SMEM),
                  pl.BlockSpec((4, ATT_HEAD_DIM), lambda h, b, i: (0, 0)),
                  pl.BlockSpec((1, HEAD_W), lambda h, b, i: (0, 0)),
                  pl.BlockSpec((1, tq, HEAD_W), lambda h, b, i: (b, i, h)),
                  pl.BlockSpec((1, seq, HEAD_W), lambda h, b, i: (b, 0, ATT_HEADS + h)),
                  pl.BlockSpec((1, seq, HEAD_W), lambda h, b, i: (b, 0, 2 * ATT_HEADS + h))],
        out_specs=pl.BlockSpec((1, tq, HEAD_W), lambda h, b, i: (b, i, h)),
        out_shape=jax.ShapeDtypeStruct((bsz, seq, ATT_HEADS * HEAD_W), BF16),
        scratch_shapes=[pltpu.VMEM((tq, 2 * seq - tq), F32)],
        compiler_params=_cparams("arbitrary", "arbitrary", "arbitrary"),
        name="attn",
    )(slopes, lam, subln_g.reshape(1, HEAD_W), proj3, proj3, proj3)


def _conv3(x, prev_row, next_row, w):
    rows = x.shape[0]
    row = lax.broadcasted_iota(jnp.int32, (rows, 1), 0)
    x_prev = jnp.where(row == 0, prev_row, pltpu.roll(x, 1, axis=0))
    x_next = jnp.where(row == rows - 1, next_row, pltpu.roll(x, rows - 1, axis=0))
    return w[0:1] * x_prev + w[1:2] * x + w[2:3] * x_next


def _conv_kernel(hy_ref, hyp_ref, hyn_ref, sc_ref, scp_ref, scn_ref, hw_ref, sw_ref,
                 u_ref, x0_ref, yc_ref):
    i = pl.program_id(1)
    last = pl.num_programs(1) - 1
    width = u_ref.shape[2]
    has_prev = jnp.where(i > 0, 1.0, 0.0)
    has_next = jnp.where(i < last, 1.0, 0.0)
    hsub = hyp_ref.shape[1]

    hy = hy_ref[0].astype(F32)
    hy_prev = hyp_ref[0].astype(F32)[hsub - 1:hsub] * has_prev
    hy_next = hyn_ref[0].astype(F32)[0:1] * has_next
    hyc = _conv3(hy, hy_prev, hy_next, hw_ref[...].astype(F32))
    x0_ref[0] = hyc[:, :width].astype(x0_ref.dtype)
    u_ref[0] = (hyc[:, 2 * width:] * hyc[:, width:2 * width]).astype(u_ref.dtype)

    def gated(blk):
        return blk[:, width:2 * width] * blk[:, 2 * width:]

    sc = sc_ref[0].astype(F32)
    cx_prev = gated(scp_ref[0].astype(F32)[hsub - 1:hsub]) * has_prev
    cx_next = gated(scn_ref[0].astype(F32)[0:1]) * has_next
    conv = _conv3(gated(sc), cx_prev, cx_next, sw_ref[...].astype(F32))
    yc_ref[0] = (sc[:, :width] * conv).astype(yc_ref.dtype)


def _convs(proj3, hy_w, sc_w, hy_col, sc_col):
    bsz, seq, _ = proj3.shape
    width = sc_w.shape[1]
    tl = _tile(seq, 512)
    hs = V7X_BF16_SUBLANES
    nh = tl // hs
    nblk = seq // hs

    def main(col):
        return pl.BlockSpec((1, tl, 3 * width), lambda b, i: (b, i, col))

    def prev(col):
        return pl.BlockSpec((1, hs, 3 * width), lambda b, i: (b, jnp.maximum(i * nh - 1, 0), col))

    def nxt(col):
        return pl.BlockSpec((1, hs, 3 * width), lambda b, i: (b, jnp.minimum((i + 1) * nh, nblk - 1), col))

    out = jax.ShapeDtypeStruct((bsz, seq, width), BF16)
    ospec = pl.BlockSpec((1, tl, width), lambda b, i: (b, i, 0))
    return pl.pallas_call(
        _conv_kernel,
        grid=(bsz, seq // tl),
        in_specs=[main(hy_col), prev(hy_col), nxt(hy_col), main(sc_col), prev(sc_col), nxt(sc_col),
                  pl.BlockSpec((3, 3 * width), lambda b, i: (0, 0)),
                  pl.BlockSpec((3, width), lambda b, i: (0, 0))],
        out_specs=[ospec, ospec, ospec],
        out_shape=[out, out, out],
        compiler_params=_cparams("parallel", "arbitrary"),
        name="conv",
    )(proj3, proj3, proj3, proj3, proj3, proj3, hy_w, sc_w)


def _tab_kernel(c_ref, s_ref, st_ref, cb_ref, sb_ref, *, seq):
    tr = c_ref.shape[0]
    i = pl.program_id(0)
    dr = lax.broadcasted_iota(jnp.int32, (tr, seq), 0)
    c = lax.broadcasted_iota(jnp.int32, (tr, seq), 1)
    r = i * tr + dr

    def angle(prod):
        return (prod & (2 * seq - 1)).astype(F32) * (math.pi / seq)

    @pl.when(i == 0)
    def _():
        base = angle(dr * c)
        cb_ref[...] = jnp.cos(base)
        sb_ref[...] = jnp.sin(base)

    col = lax.broadcasted_iota(jnp.int32, (1, seq), 1)
    lead = angle((i * tr) * col)
    c0, s0 = jnp.cos(lead), jnp.sin(lead)
    cb, sb = cb_ref[...], sb_ref[...]
    cosv = c0 * cb - s0 * sb
    nsin = -(s0 * cb + c0 * sb)
    c_ref[...] = cosv.astype(c_ref.dtype)
    s_ref[...] = jnp.where(r == 0, jnp.where((c & 1) == 0, 1.0, -1.0), nsin).astype(s_ref.dtype)
    st_ref[...] = jnp.where(c == 0, jnp.where((r & 1) == 0, 1.0, -1.0), nsin).astype(st_ref.dtype)


def _dft_tables(seq):
    assert seq & (seq - 1) == 0, "sequence length must be a power of two"
    tr = _tile(seq, 256)
    spec = pl.BlockSpec((tr, seq), lambda i: (i, 0))
    out = jax.ShapeDtypeStruct((seq, seq), BF16)
    return pl.pallas_call(
        functools.partial(_tab_kernel, seq=seq),
        grid=(seq // tr,),
        out_specs=[spec, spec, spec],
        out_shape=[out, out, out],
        scratch_shapes=[pltpu.VMEM((tr, seq), F32), pltpu.VMEM((tr, seq), F32)],
        compiler_params=_cparams("arbitrary"),
        name="dft_tables",
    )()


def _filt_kernel(z_ref, win_ref, bin_ref, wmid_ref, bmid_ref, fr_ref, wout_ref, dl_ref,
                 hs_ref, l1_ref, *, seq):
    i = pl.program_id(0)
    tl = z_ref.shape[0]
    width = dl_ref.shape[1]
    hi = lax.Precision.HIGHEST
    fr = fr_ref[...]
    h = jnp.sin(fr * (jnp.dot(z_ref[...], win_ref[...], precision=hi, preferred_element_type=F32)
                      + bin_ref[...]))
    for j in range(wmid_ref.shape[0]):
        h = jnp.sin(fr * (jnp.dot(h, wmid_ref[j], precision=hi, preferred_element_type=F32)
                          + bmid_ref[j]))
    ho = jnp.dot(h, wout_ref[...], precision=hi, preferred_element_type=F32)
    pos = i * tl + lax.broadcasted_iota(jnp.int32, (tl, 1), 0)
    t = pos.astype(F32) / float(seq - 1)
    window = jnp.exp(-t * dl_ref[...])
    h_fwd = ho[:, :width] * window
    h_bwd = jnp.where(pos == 0, 0.0, ho[:, width:] * window)
    hs_ref[:, :width] = h_fwd + h_bwd
    hs_ref[:, width:] = h_fwd - h_bwd

    @pl.when(i == 0)
    def _():
        l1_ref[...] = jnp.zeros_like(l1_ref)

    l1_ref[...] += jnp.sum(jnp.abs(h_fwd) + jnp.abs(h_bwd), axis=0, keepdims=True)


def _fspec_kernel(c_ref, s_ref, hs_ref, l1_ref, a_ref, b_ref, d_ref, *, seq):
    i = pl.program_id(0)
    tk = c_ref.shape[0]
    width = l1_ref.shape[1]
    hs = hs_ref[...].astype(BF16)
    xc = jnp.dot(c_ref[...], hs[:, :width], preferred_element_type=F32)
    xs = jnp.dot(s_ref[...], hs, preferred_element_type=F32)
    k = i * tk + lax.broadcasted_iota(jnp.int32, (tk, 1), 0)
    inv_l1 = 1.0 / l1_ref[...]
    wk = jnp.where(k == 0, 1.0, 2.0) / float(2 * seq)
    a = xc * wk * inv_l1
    a_ref[...] = a
    b_ref[...] = jnp.where(k == 0, 0.0, xs[:, width:] * wk * inv_l1)
    d_ref[...] = jnp.where(k == 0, xs[:, :width] * wk * inv_l1, a)


def _hyena_filter_spectrum(seq, tabs, w_in, b_in, w_mid, b_mid, freq, w_out):
    c_tab, s_tab, _ = tabs
    emb, fw = w_in.shape
    width = w_out.shape[1] // 2
    n = np.arange(seq, dtype=np.float64)[:, None]
    f = np.linspace(1e-4, HY_BANDS - 1, HY_BANDS)[None]
    w = 2.0 * math.pi * n / seq
    z = np.concatenate([n / (seq - 1), np.cos(f * w), -np.sin(f * w)], -1)
    z = np.pad(z, ((0, 0), (0, V7X_LANES - emb))).astype(np.float32)
    w_in_p = jnp.pad(w_in.astype(F32), ((0, V7X_LANES - emb), (0, 0)))
    deltas = np.abs(np.linspace(math.log(HY_TARGET) / HY_SLOW_PCT, math.log(HY_TARGET) / HY_FAST_PCT,
                                width)).astype(np.float32)[None]
    tl = _tile(seq, 512)
    full = lambda *shape: pl.BlockSpec(shape, lambda i: (0,) * len(shape))
    hs, l1 = pl.pallas_call(
        functools.partial(_filt_kernel, seq=seq),
        grid=(seq // tl,),
        in_specs=[pl.BlockSpec((tl, V7X_LANES), lambda i: (i, 0)),
                  full(V7X_LANES, fw), full(1, fw), full(*w_mid.shape), full(w_mid.shape[0], 1, fw),
                  full(1, fw), full(fw, 2 * width), full(1, width)],
        out_specs=[pl.BlockSpec((tl, 2 * width), lambda i: (i, 0)), full(1, width)],
        out_shape=[jax.ShapeDtypeStruct((seq, 2 * width), F32), jax.ShapeDtypeStruct((1, width), F32)],
        compiler_params=_cparams("arbitrary"),
        name="hyena_filter",
    )(jnp.asarray(z), w_in_p, b_in.reshape(1, fw).astype(F32), w_mid.astype(F32),
      b_mid.reshape(w_mid.shape[0], 1, fw).astype(F32), freq.reshape(1, fw).astype(F32),
      w_out.astype(F32), jnp.asarray(deltas))

    tk = _tile(seq, 256)
    ospec = pl.BlockSpec((tk, width), lambda i: (i, 0))
    out = jax.ShapeDtypeStruct((seq, width), F32)
    return pl.pallas_call(
        functools.partial(_fspec_kernel, seq=seq),
        grid=(seq // tk,),
        in_specs=[pl.BlockSpec((tk, seq), lambda i: (i, 0)), pl.BlockSpec((tk, seq), lambda i: (i, 0)),
                  full(seq, 2 * width), full(1, width)],
        out_specs=[ospec, ospec, ospec],
        out_shape=[out, out, out],
        compiler_params=_cparams("parallel"),
        name="filter_spectrum",
    )(c_tab, s_tab, hs, l1)


def _dft_fwd_kernel(c_ref, s_ref, u_ref, a_ref, b_ref, d_ref, yre_ref, yim_ref):
    u = u_ref[0]
    x_re = jnp.dot(c_ref[...], u, preferred_element_type=F32)
    x_im = jnp.dot(s_ref[...], u, preferred_element_type=F32)
    b = b_ref[...]
    yre_ref[0] = (x_re * a_ref[...] - x_im * b).astype(yre_ref.dtype)
    yim_ref[0] = (x_re * b + x_im * d_ref[...]).astype(yim_ref.dtype)


def _dft_inv_kernel(c_ref, st_ref, yre_ref, yim_ref, u_ref, x0_ref, bias_ref, o_ref):
    y = (jnp.dot(c_ref[...], yre_ref[0], preferred_element_type=F32)
         + jnp.dot(st_ref[...], yim_ref[0], preferred_element_type=F32))
    y = y + u_ref[0].astype(F32) * bias_ref[...]
    o_ref[0] = (x0_ref[0].astype(F32) * y).astype(o_ref.dtype)


def _long_conv(u, x0, tabs, spec, bias):
    c_tab, s_tab, st_tab = tabs
    a, b, d = spec
    bsz, seq, width = u.shape
    tk = _tile(seq, 512)
    tab = pl.BlockSpec((tk, seq), lambda j, bb: (j, 0))
    whole = pl.BlockSpec((1, seq, width), lambda j, bb: (bb, 0, 0))
    rows = pl.BlockSpec((1, tk, width), lambda j, bb: (bb, j, 0))
    coef = pl.BlockSpec((tk, width), lambda j, bb: (j, 0))
    spec_out = jax.ShapeDtypeStruct((bsz, seq, width), BF16)
    y_re, y_im = pl.pallas_call(
        _dft_fwd_kernel,
        grid=(seq // tk, bsz),
        in_specs=[tab, tab, whole, coef, coef, coef],
        out_specs=[rows, rows],
        out_shape=[spec_out, spec_out],
        compiler_params=_cparams("parallel", "arbitrary"),
        name="dft_fwd",
    )(c_tab, s_tab, u, a, b, d)
    return pl.pallas_call(
        _dft_inv_kernel,
        grid=(seq // tk, bsz),
        in_specs=[tab, tab, whole, whole, rows, rows, pl.BlockSpec((1, width), lambda j, bb: (0, 0))],
        out_specs=rows,
        out_shape=spec_out,
        compiler_params=_cparams("parallel", "arbitrary"),
        name="dft_inv",
    )(c_tab, st_tab, y_re, y_im, u, x0, bias.reshape(1, width).astype(F32))


def _merge_kernel(ya_ref, yh_ref, yc_ref, g0_ref, g1_ref, x_ref, wb_ref, wo_ref, lg_ref, lb_ref,
                  *rest, alpha, n_experts):
    if n_experts:
        r_ref, o_ref, route_ref, cnt_ref, m_ref, tri_ref = rest
    else:
        o_ref, m_ref = rest
    d = o_ref.shape[1]
    half = d // 2
    g0 = g0_ref[...]
    g1 = g1_ref[...]
    gates = ((g0[:, :half], g0[:, d:], g1[:, half:d]),
             (g0[:, half:d], g1[:, :half], g1[:, d:]))
    branches = (ya_ref[...], yh_ref[...], yc_ref[...])
    for c in range(2):
        acc = None
        for j in range(3):
            y = jnp.dot(branches[j], wb_ref[j, :, c * half:(c + 1) * half], preferred_element_type=F32)
            term = jax.nn.sigmoid(gates[c][j].astype(F32)) * y
            acc = term if acc is None else acc + term
        m_ref[:, c * half:(c + 1) * half] = acc.astype(m_ref.dtype)
    t = jnp.dot(m_ref[...], wo_ref[...], preferred_element_type=F32)
    out = _layer_norm(alpha * x_ref[...] + t, lg_ref[...], lb_ref[...])
    o_ref[...] = out
    if n_experts:
        _route_tile(out, r_ref, route_ref, cnt_ref, tri_ref, n_experts)


def _merge(y_a, y_h, y_c, proj, x, wb, wo, ln_g, ln_b, alpha, gate_col, router=None):
    t, d = x.shape
    bw = y_a.shape[1]
    tm = _tile(t, MOE_TOKEN_TILE)
    row = lambda w: pl.BlockSpec((tm, w), lambda i: (i, 0))
    fixed = lambda *shape: pl.BlockSpec(shape, lambda i: (0,) * len(shape))
    gw = 3 * d // 2
    n_experts = 0 if router is None else router.shape[1]
    in_specs = [row(bw), row(bw), row(bw),
                pl.BlockSpec((tm, gw), lambda i: (i, gate_col)),
                pl.BlockSpec((tm, gw), lambda i: (i, gate_col + 1)),
                row(d), fixed(*wb.shape), fixed(*wo.shape), fixed(1, d), fixed(1, d)]
    args = [y_a, y_h, y_c, proj, proj, x, wb, wo, ln_g.reshape(1, d), ln_b.reshape(1, d)]
    out_specs = row(d)
    out_shape = jax.ShapeDtypeStruct((t, d), F32)
    scratch = [pltpu.VMEM((tm, d), BF16)]
    if n_experts:
        in_specs.append(fixed(d, V7X_LANES))
        args.append(jnp.pad(router.astype(F32), ((0, 0), (0, V7X_LANES - n_experts))))
        out_specs = [out_specs, row(ROUTE_COLS), fixed(1, V7X_LANES)]
        out_shape = [out_shape, jax.ShapeDtypeStruct((t, ROUTE_COLS), F32),
                     jax.ShapeDtypeStruct((1, V7X_LANES), F32)]
        scratch.append(pltpu.VMEM((tm, tm), BF16))
    return pl.pallas_call(
        functools.partial(_merge_kernel, alpha=alpha, n_experts=n_experts),
        grid=(t // tm,),
        in_specs=in_specs,
        out_specs=out_specs,
        out_shape=out_shape,
        scratch_shapes=scratch,
        compiler_params=_cparams("arbitrary" if n_experts else "parallel"),
        name="merge",
    )(*args)


def _swiglu_partial(xb, wg, wu, wd):
    g = jnp.dot(xb, wg, preferred_element_type=F32)
    u = jnp.dot(xb, wu, preferred_element_type=F32)
    return jnp.dot((jax.nn.silu(g) * u).astype(BF16), wd, preferred_element_type=F32)


def _ffn_kernel(x_ref, wg_ref, wu_ref, wd_ref, lg_ref, lb_ref, o_ref, xb_ref, acc_ref, *, alpha):
    f = pl.program_id(1)

    @pl.when(f == 0)
    def _():
        xb_ref[...] = x_ref[...].astype(BF16)
        acc_ref[...] = jnp.zeros_like(acc_ref)

    acc_ref[...] += _swiglu_partial(xb_ref[...], wg_ref[...], wu_ref[...], wd_ref[...])

    @pl.when(f == pl.num_programs(1) - 1)
    def _():
        o_ref[...] = _layer_norm(alpha * x_ref[...] + acc_ref[...], lg_ref[...], lb_ref[...])


def _ffn(x, wg, wu, wd, ln_g, ln_b, alpha):
    t, d = x.shape
    ff = wg.shape[1]
    tm = _tile(t, 1024)
    tf = ff // 2 if (ff // 2) % V7X_LANES == 0 else ff
    vec = pl.BlockSpec((1, d), lambda i, f: (0, 0))
    return pl.pallas_call(
        functools.partial(_ffn_kernel, alpha=alpha),
        grid=(t // tm, ff // tf),
        in_specs=[pl.BlockSpec((tm, d), lambda i, f: (i, 0)),
                  pl.BlockSpec((d, tf), lambda i, f: (0, f)),
                  pl.BlockSpec((d, tf), lambda i, f: (0, f)),
                  pl.BlockSpec((tf, d), lambda i, f: (f, 0)),
                  vec, vec],
        out_specs=pl.BlockSpec((tm, d), lambda i, f: (i, 0)),
        out_shape=jax.ShapeDtypeStruct((t, d), F32),
        scratch_shapes=[pltpu.VMEM((tm, d), BF16), pltpu.VMEM((tm, d), F32)],
        compiler_params=_cparams("parallel", "arbitrary"),
        name="ffn",
    )(x, wg, wu, wd, ln_g.reshape(1, d), ln_b.reshape(1, d))


def _top2(logits, n_experts):
    lane = lax.broadcasted_iota(jnp.int32, logits.shape, 1).astype(F32)
    neg = jnp.float32(-jnp.inf)
    sentinel = jnp.float32(n_experts)
    lg = jnp.where(lane < sentinel, logits, neg)
    m1 = jnp.max(lg, axis=1, keepdims=True)
    i1 = jnp.min(jnp.where(lg == m1, lane, sentinel), axis=1, keepdims=True)
    rest = jnp.where(lane == i1, neg, lg)
    m2 = jnp.max(rest, axis=1, keepdims=True)
    i2 = jnp.min(jnp.where(rest == m2, lane, sentinel), axis=1, keepdims=True)
    e2 = jnp.exp(m2 - m1)
    p1 = 1.0 / (1.0 + e2)
    p2 = e2 / (1.0 + e2)
    return lane == i1, lane == i2, i1, i2, p1, p2


ROUTE_COLS = 8
MOE_TILE_ROWS = 512
MOE_TOKEN_TILE = 512
ROW_DMA_UNROLL = 8


def _route_tile(x, r_ref, route_ref, cnt_ref, tri_ref, n_experts):
    i = pl.program_id(0)
    tm = x.shape[0]

    @pl.when(i == 0)
    def _():
        cnt_ref[...] = jnp.zeros_like(cnt_ref)
        row = lax.broadcasted_iota(jnp.int32, (tm, tm), 0)
        col = lax.broadcasted_iota(jnp.int32, (tm, tm), 1)
        tri_ref[...] = jnp.where(col < row, 1.0, 0.0).astype(tri_ref.dtype)

    logits = jnp.dot(x, r_ref[...], precision=lax.Precision.HIGHEST, preferred_element_type=F32)
    sel1, sel2, i1, i2, p1, p2 = _top2(logits, n_experts)
    chosen = jnp.where(sel1, 1.0, jnp.where(sel2, 1.0, 0.0))
    before = jnp.dot(tri_ref[...], chosen.astype(tri_ref.dtype), preferred_element_type=F32) + cnt_ref[...]
    r1 = jnp.sum(jnp.where(sel1, before, 0.0), axis=1, keepdims=True)
    r2 = jnp.sum(jnp.where(sel2, before, 0.0), axis=1, keepdims=True)
    cnt_ref[...] += jnp.sum(chosen, axis=0, keepdims=True)
    lane = lax.broadcasted_iota(jnp.int32, logits.shape, 1)
    record = jnp.zeros_like(logits)
    for c, val in enumerate((i1, i2, p1, p2, r1, r2)):
        record = jnp.where(lane == c, val, record)
    route_ref[...] = record[:, :ROUTE_COLS]


def _moe_scatter_kernel(pos_ref, pad_ref, tail_ref, x_ref, xs_ref, zero_ref, sem, zsem):
    tm = x_ref.shape[0]
    tr = zero_ref.shape[0]

    @pl.when(pl.program_id(0) == 0)
    def _():
        zero_ref[...] = jnp.zeros_like(zero_ref)

        def pad_copy(first, r):
            return pltpu.make_async_copy(zero_ref.at[pl.ds(0, 1)], xs_ref.at[pl.ds(first + r, 1)], zsem)

        def tile_copy(t):
            rows = pl.ds(pl.multiple_of(t * tr, tr), tr)
            return pltpu.make_async_copy(zero_ref, xs_ref.at[rows], zsem)

        def each(copy_fn, method):
            for e in range(pad_ref.shape[0]):
                first, count = pad_ref[e, 0], pad_ref[e, 1]
                lax.fori_loop(0, count, lambda r, c: (getattr(copy_fn(first, r), method)(), c)[1], 0)

        each(pad_copy, "start")
        lax.fori_loop(tail_ref[0], tail_ref[1], lambda t, c: (tile_copy(t).start(), c)[1], 0)
        each(pad_copy, "wait")
        lax.fori_loop(tail_ref[0], tail_ref[1], lambda t, c: (tile_copy(t).wait(), c)[1], 0)

    def row_copy(r, k):
        slot = pos_ref[0, 0, k * tm + r]
        return pltpu.make_async_copy(x_ref.at[pl.ds(r, 1)], xs_ref.at[pl.ds(slot, 1)], sem)

    def start(r, carry):
        row_copy(r, 0).start()
        row_copy(r, 1).start()
        return carry

    def wait(r, carry):
        row_copy(r, 0).wait()
        row_copy(r, 1).wait()
        return carry

    lax.fori_loop(0, tm, start, 0, unroll=ROW_DMA_UNROLL)
    lax.fori_loop(0, tm, wait, 0, unroll=ROW_DMA_UNROLL)


def _gmm_kernel(te_ref, nu_ref, xs_ref, wg_ref, wu_ref, wd_ref, ys_ref, xb_ref, acc_ref):
    del te_ref
    j = pl.program_id(0)
    f = pl.program_id(1)

    @pl.when(jnp.logical_and(j >= nu_ref[0], f == 0))
    def _():
        ys_ref[...] = jnp.zeros_like(ys_ref)

    @pl.when(j < nu_ref[0])
    def _():
        @pl.when(f == 0)
        def _():
            xb_ref[...] = xs_ref[...].astype(BF16)
            acc_ref[...] = jnp.zeros_like(acc_ref)

        acc_ref[...] += _swiglu_partial(xb_ref[...], wg_ref[...], wu_ref[...], wd_ref[...])

        @pl.when(f == pl.num_programs(1) - 1)
        def _():
            ys_ref[...] = acc_ref[...]


def _moe_combine_kernel(pos_ref, route_ref, x_ref, lg_ref, lb_ref, ys_ref, o_ref, y1_ref, y2_ref, sem,
                        *, alpha):
    tm = x_ref.shape[0]

    def row_copy(r, k):
        slot = pos_ref[0, 0, k * tm + r]
        dst = (y1_ref, y2_ref)[k]
        return pltpu.make_async_copy(ys_ref.at[pl.ds(slot, 1)], dst.at[pl.ds(r, 1)], sem)

    def start(r, carry):
        row_copy(r, 0).start()
        row_copy(r, 1).start()
        return carry

    def wait(r, carry):
        row_copy(r, 0).wait()
        row_copy(r, 1).wait()
        return carry

    lax.fori_loop(0, tm, start, 0, unroll=ROW_DMA_UNROLL)
    lax.fori_loop(0, tm, wait, 0, unroll=ROW_DMA_UNROLL)
    route = route_ref[...]
    mixed = route[:, 2:3] * y1_ref[...] + route[:, 3:4] * y2_ref[...]
    o_ref[...] = _layer_norm(alpha * x_ref[...] + mixed, lg_ref[...], lb_ref[...])


def _moe(x, route, counts, wg, wu, wd, ln_g, ln_b, alpha):
    t, d = x.shape
    n_experts, _, ff = wg.shape
    tm = _tile(t, MOE_TOKEN_TILE)
    tr = MOE_TILE_ROWS
    n_tok_tiles = t // tm

    cnt = counts[0, :n_experts].astype(jnp.int32)
    padded = (cnt + tr - 1) // tr * tr
    ends = jnp.cumsum(padded)
    starts = ends - padded
    expert_ids = jnp.arange(n_experts, dtype=jnp.int32)

    def slots(expert_col, rank_col):
        onehot = route[:, expert_col:expert_col + 1].astype(jnp.int32) == expert_ids[None]
        return jnp.sum(jnp.where(onehot, starts[None], 0), axis=1) + route[:, rank_col].astype(jnp.int32)

    pos = jnp.concatenate([slots(0, 4).reshape(n_tok_tiles, 1, tm), slots(1, 5).reshape(n_tok_tiles, 1, tm)],
                          axis=2)
    n_tiles = (TOP_K * t) // tr + n_experts
    n_used = (ends[-1] // tr).reshape(1)
    tile_start = jnp.arange(n_tiles, dtype=jnp.int32) * tr
    tile_expert = jnp.sum((tile_start[:, None] >= ends[None]).astype(jnp.int32), axis=1)
    last_expert = jnp.sum((ends[-1] - tr >= ends).astype(jnp.int32))
    tile_expert = jnp.where(tile_start < ends[-1], tile_expert, last_expert)

    pos_spec = pl.BlockSpec((1, 1, 2 * tm), lambda i: (i, 0, 0), memory_space=pltpu.SMEM)
    smem_spec = pl.BlockSpec(memory_space=pltpu.SMEM)
    any_spec = pl.BlockSpec(memory_space=pl.ANY)
    pad_rows = jnp.stack([starts + cnt, padded - cnt], axis=1)
    tail_tiles = jnp.concatenate([n_used, jnp.full((1,), n_tiles, jnp.int32)])
    xs = pl.pallas_call(
        _moe_scatter_kernel,
        grid=(n_tok_tiles,),
        in_specs=[pos_spec, smem_spec, smem_spec, pl.BlockSpec((tm, d), lambda i: (i, 0))],
        out_specs=any_spec,
        out_shape=jax.ShapeDtypeStruct((n_tiles * tr, d), F32),
        scratch_shapes=[pltpu.VMEM((tr, d), F32), pltpu.SemaphoreType.DMA(()), pltpu.SemaphoreType.DMA(())],
        compiler_params=_cparams("arbitrary"),
        name="moe_scatter",
    )(pos, pad_rows, tail_tiles, x)

    tf = ff // 2 if (ff // 2) % V7X_LANES == 0 else ff
    nf = ff // tf
    used = lambda j, nu: jnp.minimum(j, nu[0] - 1)
    fcol = lambda j, f, nu: jnp.where(j < nu[0], f, nf - 1)
    ys = pl.pallas_call(
        _gmm_kernel,
        grid_spec=pltpu.PrefetchScalarGridSpec(
            num_scalar_prefetch=2,
            grid=(n_tiles, nf),
            in_specs=[pl.BlockSpec((tr, d), lambda j, f, te, nu: (used(j, nu), 0)),
                      pl.BlockSpec((None, d, tf), lambda j, f, te, nu: (te[j], 0, fcol(j, f, nu))),
                      pl.BlockSpec((None, d, tf), lambda j, f, te, nu: (te[j], 0, fcol(j, f, nu))),
                      pl.BlockSpec((None, tf, d), lambda j, f, te, nu: (te[j], fcol(j, f, nu), 0))],
            out_specs=pl.BlockSpec((tr, d), lambda j, f, te, nu: (j, 0)),
            scratch_shapes=[pltpu.VMEM((tr, d), BF16), pltpu.VMEM((tr, d), F32)]),
        out_shape=jax.ShapeDtypeStruct((n_tiles * tr, d), F32),
        compiler_params=_cparams("arbitrary", "arbitrary"),
        name="moe_experts",
    )(tile_expert, n_used, xs, wg, wu, wd)

    vec = pl.BlockSpec((1, d), lambda i: (0, 0))
    return pl.pallas_call(
        functools.partial(_moe_combine_kernel, alpha=alpha),
        grid=(n_tok_tiles,),
        in_specs=[pos_spec, pl.BlockSpec((tm, ROUTE_COLS), lambda i: (i, 0)),
                  pl.BlockSpec((tm, d), lambda i: (i, 0)), vec, vec, any_spec],
        out_specs=pl.BlockSpec((tm, d), lambda i: (i, 0)),
        out_shape=jax.ShapeDtypeStruct((t, d), F32),
        scratch_shapes=[pltpu.VMEM((tm, d), F32), pltpu.VMEM((tm, d), F32), pltpu.SemaphoreType.DMA(())],
        compiler_params=_cparams("arbitrary"),
        name="moe_combine",
    )(pos, route, x, ln_g.reshape(1, d), ln_b.reshape(1, d), ys)


def _run_trunk(x3, p, tables):
    bsz, seq, d = x3.shape
    depth = p['w_in'].shape[0]
    alpha = (2 * depth) ** 0.25
    att_w = ATT_HEADS * HEAD_W
    width = p['sc_conv_w'].shape[2]
    gate_w = 3 * d // 2
    assert att_w % width == 0 and (3 * att_w + 6 * width) % gate_w == 0, "column blocks must align"
    hy_col = att_w // width
    d_in = p['w_in'].shape[2]
    col_scale = jnp.asarray(np.where(np.arange(d_in) < att_w, LOG2E * ATT_HEAD_DIM ** -0.5, 1.0)
                            .astype(np.float32)[None])
    x = x3.reshape(bsz * seq, d)
    for i in range(depth):
        proj = _proj(x, p['w_in'][i], col_scale)
        proj3 = proj.reshape(bsz, seq, proj.shape[1])
        y_a = _attention(proj3, p['att_lambda'][i], p['att_subln_g'][i], i)
        u, x0, y_c = _convs(proj3, p['hy_short_w'][i], p['sc_conv_w'][i], hy_col=hy_col, sc_col=hy_col + 1)
        spec = _hyena_filter_spectrum(seq, tables, p['hf_w_in'][i], p['hf_b_in'][i], p['hf_w_mid'][i],
                                      p['hf_b_mid'][i], p['hf_freq'][i], p['hf_w_out'][i])
        y_h = _long_conv(u, x0, tables, spec, p['hy_bias'][i])
        flat = lambda a: a.reshape(bsz * seq, a.shape[2])
        j = i // 2
        merged = _merge(flat(y_a), flat(y_h), flat(y_c), proj, x, p['w_branch'][i], p['w_out'][i],
                        p['ln1_g'][i], p['ln1_b'][i], alpha, gate_col=(3 * att_w + 6 * width) // gate_w,
                        router=p['router_w'][j] if i % 2 else None)
        if i % 2 == 0:
            x = _ffn(merged, p['ffn_w_gate'][j], p['ffn_w_up'][j], p['ffn_w_down'][j],
                     p['ln2_g'][i], p['ln2_b'][i], alpha)
        else:
            x, route, counts = merged
            x = _moe(x, route, counts, p['moe_w_gate'][j], p['moe_w_up'][j], p['moe_w_down'][j],
                     p['ln2_g'][i], p['ln2_b'][i], alpha)
    return x.reshape(bsz, seq, d)


def kernel(x_prompt, x_sample, w_in, att_lambda, att_subln_g, hy_short_w, hf_w_in, hf_b_in,
           hf_w_mid, hf_b_mid, hf_freq, hf_w_out, hy_bias, sc_conv_w, w_branch, w_out,
           ln1_g, ln1_b, ln2_g, ln2_b, ffn_w_gate, ffn_w_up, ffn_w_down, router_w,
           moe_w_gate, moe_w_up, moe_w_down):
    bf = lambda a: a.astype(BF16)
    p = dict(w_in=bf(w_in), att_lambda=att_lambda, att_subln_g=att_subln_g, hy_short_w=hy_short_w,
             hf_w_in=hf_w_in, hf_b_in=hf_b_in, hf_w_mid=hf_w_mid, hf_b_mid=hf_b_mid,
             hf_freq=hf_freq, hf_w_out=hf_w_out, hy_bias=hy_bias, sc_conv_w=sc_conv_w,
             w_branch=bf(w_branch), w_out=bf(w_out), ln1_g=ln1_g, ln1_b=ln1_b, ln2_g=ln2_g, ln2_b=ln2_b,
             ffn_w_gate=bf(ffn_w_gate), ffn_w_up=bf(ffn_w_up), ffn_w_down=bf(ffn_w_down),
             router_w=router_w, moe_w_gate=bf(moe_w_gate), moe_w_up=bf(moe_w_up),
             moe_w_down=bf(moe_w_down))
    tables = {}
    outs = []
    for x3 in (x_prompt, x_sample):
        seq = x3.shape[1]
        if seq not in tables:
            tables[seq] = _dft_tables(seq)
        outs.append(_run_trunk(x3, p, tables[seq]))
    return tuple(outs)
```

```python
import functools
import math

import numpy as np
import jax
import jax.numpy as jnp
from jax import lax
from jax.experimental import pallas as pl
from jax.experimental.pallas import tpu as pltpu

F32 = jnp.float32
BF16 = jnp.bfloat16

ATT_HEADS = 4
ATT_HEAD_DIM = 64
HEAD_W = 2 * ATT_HEAD_DIM
HY_BANDS = 16
HY_TARGET = 1e-2
HY_FAST_PCT = 0.3
HY_SLOW_PCT = 1.5
TOP_K = 2
LN_EPS = 1e-5
RMS_EPS = 1e-5
LOG2E = math.log2(math.e)
ATT_Q_TILE = 256
ATT_MASKED = -2.0 ** 100

V7X_LANES = 128
V7X_BF16_SUBLANES = 16
V7X_VMEM_LIMIT = 56 * 1024 * 1024


def _cparams(*sem):
    return pltpu.CompilerParams(dimension_semantics=sem, vmem_limit_bytes=V7X_VMEM_LIMIT)


def _tile(n, pref):
    t = min(n, pref)
    while n % t:
        t //= 2
    return t


def _layer_norm(v, g, b):
    mu = jnp.mean(v, axis=-1, keepdims=True)
    d = v - mu
    var = jnp.mean(d * d, axis=-1, keepdims=True)
    return d * lax.rsqrt(var + LN_EPS) * g + b


def _proj_kernel(x_ref, w_ref, cs_ref, o_ref, xb_ref):
    @pl.when(pl.program_id(1) == 0)
    def _():
        xb_ref[...] = x_ref[...].astype(BF16)

    acc = jnp.dot(xb_ref[...], w_ref[...], preferred_element_type=F32)
    o_ref[...] = (acc * cs_ref[...]).astype(o_ref.dtype)


def _proj(x, w, col_scale):
    t, d = x.shape
    n = w.shape[1]
    tm = _tile(t, 1024)
    tn = _tile(n, 1536)
    return pl.pallas_call(
        _proj_kernel,
        grid=(t // tm, n // tn),
        in_specs=[pl.BlockSpec((tm, d), lambda i, j: (i, 0)),
                  pl.BlockSpec((d, tn), lambda i, j: (0, j)),
                  pl.BlockSpec((1, tn), lambda i, j: (0, j))],
        out_specs=pl.BlockSpec((tm, tn), lambda i, j: (i, j)),
        out_shape=jax.ShapeDtypeStruct((t, n), BF16),
        scratch_shapes=[pltpu.VMEM((tm, d), BF16)],
        compiler_params=_cparams("parallel", "arbitrary"),
        name="proj",
    )(x, w, col_scale)


def _split3(x):
    p1 = x.astype(BF16).astype(F32)
    r1 = x - p1
    p2 = r1.astype(BF16).astype(F32)
    p3 = (r1 - p2).astype(BF16).astype(F32)
    return p1, p2, p3


def _attn_kernel(slopes_ref, lam_ref, g_ref, q_ref, k_ref, v_ref, o_ref,
                 kf_ref, qf_ref, dg_ref, *, lam_init, tq):
    h = pl.program_id(0)
    b = pl.program_id(1)
    qi = pl.program_id(2)
    seq = k_ref.shape[1]

    @pl.when(jnp.logical_and(b == 0, qi == 0))
    def _():
        c = LOG2E * slopes_ref[h]
        lane = lax.broadcasted_iota(jnp.int32, kf_ref.shape, 1)
        jp = lax.broadcasted_iota(jnp.int32, kf_ref.shape, 0) - (seq - tq)
        right = jp >= tq
        a = jnp.where(jp < 0, c * jp.astype(F32), jnp.where(right, -c * jp.astype(F32), ATT_MASKED))
        a1, a2, a3 = _split3(a)
        flag = jnp.where(jnp.logical_and(right, lane < 6), 1.0, 0.0)
        kf_ref[...] = jnp.where(lane == 0, a1, jnp.where(lane == 1, a2, jnp.where(lane == 2, a3, flag))
                                ).astype(kf_ref.dtype)
        lane_q = lax.broadcasted_iota(jnp.int32, qf_ref.shape, 1)
        row_q = lax.broadcasted_iota(jnp.int32, qf_ref.shape, 0).astype(F32)
        g1, g2, g3 = _split3(2.0 * c * row_q)
        qf_ref[...] = jnp.where(lane_q < 3, 1.0, jnp.where(lane_q == 3, g1, jnp.where(lane_q == 4, g2,
                                jnp.where(lane_q == 5, g3, 0.0)))).astype(qf_ref.dtype)
        di = lax.broadcasted_iota(jnp.int32, dg_ref.shape, 0)
        dj = lax.broadcasted_iota(jnp.int32, dg_ref.shape, 1)
        dg_ref[...] = c * (di - jnp.abs(di - dj)).astype(F32)

    start = pl.multiple_of((pl.num_programs(2) - 1 - qi) * tq, tq)
    q0 = pl.multiple_of(qi * tq, tq)
    lf = lam_ref[...].astype(F32)
    lam_full = (jnp.exp(jnp.sum(lf[0:1] * lf[1:2], axis=1, keepdims=True))
                - jnp.exp(jnp.sum(lf[2:3] * lf[3:4], axis=1, keepdims=True)) + lam_init)
    q = q_ref[0]
    k_aug = jnp.concatenate([k_ref[0], kf_ref[pl.ds(start, seq), :]], axis=1)
    v = v_ref[0]
    k_own = k_ref[0, pl.ds(q0, tq), :]
    v_own = v_ref[0, pl.ds(q0, tq), :]
    qf = qf_ref[...]
    own_bias = dg_ref[...]
    lane = lax.broadcasted_iota(jnp.int32, q.shape, 1)
    zero = jnp.zeros_like(q)
    nt = (((1,), (1,)), ((), ()))

    def part(s, values):
        m = jnp.max(s, axis=1, keepdims=True)
        e = jnp.exp2(s - m)
        return m, jnp.sum(e, axis=1, keepdims=True), jnp.dot(e.astype(BF16), values, preferred_element_type=F32)

    def softmax_pv(qm):
        m_far, l_far, o_far = part(lax.dot_general(jnp.concatenate([qm, qf], axis=1), k_aug, nt,
                                                   preferred_element_type=F32), v)
        m_own, l_own, o_own = part(lax.dot_general(qm, k_own, nt, preferred_element_type=F32) + own_bias,
                                   v_own)
        m = jnp.maximum(m_far, m_own)
        w_far, w_own = jnp.exp2(m_far - m), jnp.exp2(m_own - m)
        return (o_far * w_far + o_own * w_own) * (1.0 / (l_far * w_far + l_own * w_own))

    o = (softmax_pv(jnp.where(lane < ATT_HEAD_DIM, q, zero))
         - lam_full * softmax_pv(jnp.where(lane >= ATT_HEAD_DIM, q, zero)))
    o = o * lax.rsqrt(jnp.mean(o * o, axis=-1, keepdims=True) + RMS_EPS) * g_ref[...].astype(F32)
    o_ref[0] = (o * (1.0 - lam_init)).astype(o_ref.dtype)


def _attention(proj3, lam, subln_g, layer):
    bsz, seq, _ = proj3.shape
    tq = _tile(seq, ATT_Q_TILE)
    lam_init = 0.8 - 0.6 * math.exp(-0.3 * layer)
    slopes = jnp.asarray(2.0 ** (-8.0 * np.arange(1, ATT_HEADS + 1, dtype=np.float32) / ATT_HEADS))
    kern = functools.partial(_attn_kernel, lam_init=lam_init, tq=tq)
    return pl.pallas_call(
        kern,
        grid=(ATT_HEADS, bsz, seq // tq),
        in_specs=[pl.BlockSpec(memory_space=pltpu.SMEM),
                  pl.BlockSpec((4, ATT_HEAD_DIM), lambda h, b, i: (0, 0)),
                  pl.BlockSpec((1, HEAD_W), lambda h, b, i: (0, 0)),
                  pl.BlockSpec((1, tq, HEAD_W), lambda h, b, i: (b, i, h)),
                  pl.BlockSpec((1, seq, HEAD_W), lambda h, b, i: (b, 0, ATT_HEADS + h)),
                  pl.BlockSpec((1, seq, HEAD_W), lambda h, b, i: (b, 0, 2 * ATT_HEADS + h))],
        out_specs=pl.BlockSpec((1, tq, HEAD_W), lambda h, b, i: (b, i, h)),
        out_shape=jax.ShapeDtypeStruct((bsz, seq, ATT_HEADS * HEAD_W), BF16),
        scratch_shapes=[pltpu.VMEM((2 * seq - tq, HEAD_W), BF16), pltpu.VMEM((tq, HEAD_W), BF16),
                        pltpu.VMEM((tq, tq), F32)],
        compiler_params=_cparams("arbitrary", "arbitrary", "arbitrary"),
        name="attn",
    )(slopes, lam, subln_g.reshape(1, HEAD_W), proj3, proj3, proj3)


def _conv3(x, prev_row, next_row, w):
    rows = x.shape[0]
    row = lax.broadcasted_iota(jnp.int32, (rows, 1), 0)
    x_prev = jnp.where(row == 0, prev_row, pltpu.roll(x, 1, axis=0))
    x_next = jnp.where(row == rows - 1, next_row, pltpu.roll(x, rows - 1, axis=0))
    return w[0:1] * x_prev + w[1:2] * x + w[2:3] * x_next


def _conv_kernel(hy_ref, hyp_ref, hyn_ref, sc_ref, scp_ref, scn_ref, hw_ref, sw_ref,
                 u_ref, x0_ref, yc_ref):
    i = pl.program_id(1)
    last = pl.num_programs(1) - 1
    width = u_ref.shape[2]
    has_prev = jnp.where(i > 0, 1.0, 0.0)
    has_next = jnp.where(i < last, 1.0, 0.0)
    hsub = hyp_ref.shape[1]

    hy = hy_ref[0].astype(F32)
    hy_prev = hyp_ref[0].astype(F32)[hsub - 1:hsub] * has_prev
    hy_next = hyn_ref[0].astype(F32)[0:1] * has_next
    hyc = _conv3(hy, hy_prev, hy_next, hw_ref[...].astype(F32))
    x0_ref[0] = hyc[:, :width].astype(x0_ref.dtype)
    u_ref[0] = (hyc[:, 2 * width:] * hyc[:, width:2 * width]).astype(u_ref.dtype)

    def gated(blk):
        return blk[:, width:2 * width] * blk[:, 2 * width:]

    sc = sc_ref[0].astype(F32)
    cx_prev = gated(scp_ref[0].astype(F32)[hsub - 1:hsub]) * has_prev
    cx_next = gated(scn_ref[0].astype(F32)[0:1]) * has_next
    conv = _conv3(gated(sc), cx_prev, cx_next, sw_ref[...].astype(F32))
    yc_ref[0] = (sc[:, :width] * conv).astype(yc_ref.dtype)


def _convs(proj3, hy_w, sc_w, hy_col, sc_col):
    bsz, seq, _ = proj3.shape
    width = sc_w.shape[1]
    tl = _tile(seq, 512)
    hs = V7X_BF16_SUBLANES
    nh = tl // hs
    nblk = seq // hs

    def main(col):
        return pl.BlockSpec((1, tl, 3 * width), lambda b, i: (b, i, col))

    def prev(col):
        return pl.BlockSpec((1, hs, 3 * width), lambda b, i: (b, jnp.maximum(i * nh - 1, 0), col))

    def nxt(col):
        return pl.BlockSpec((1, hs, 3 * width), lambda b, i: (b, jnp.minimum((i + 1) * nh, nblk - 1), col))

    out = jax.ShapeDtypeStruct((bsz, seq, width), BF16)
    ospec = pl.BlockSpec((1, tl, width), lambda b, i: (b, i, 0))
    return pl.pallas_call(
        _conv_kernel,
        grid=(bsz, seq // tl),
        in_specs=[main(hy_col), prev(hy_col), nxt(hy_col), main(sc_col), prev(sc_col), nxt(sc_col),
                  pl.BlockSpec((3, 3 * width), lambda b, i: (0, 0)),
                  pl.BlockSpec((3, width), lambda b, i: (0, 0))],
        out_specs=[ospec, ospec, ospec],
        out_shape=[out, out, out],
        compiler_params=_cparams("parallel", "arbitrary"),
        name="conv",
    )(proj3, proj3, proj3, proj3, proj3, proj3, hy_w, sc_w)


def _tab_kernel(c_ref, s_ref, st_ref, cb_ref, sb_ref, *, seq):
    tr = c_ref.shape[0]
    i = pl.program_id(0)
    dr = lax.broadcasted_iota(jnp.int32, (tr, seq), 0)
    c = lax.broadcasted_iota(jnp.int32, (tr, seq), 1)
    r = i * tr + dr

    def angle(prod):
        return (prod & (2 * seq - 1)).astype(F32) * (math.pi / seq)

    @pl.when(i == 0)
    def _():
        base = angle(dr * c)
        cb_ref[...] = jnp.cos(base)
        sb_ref[...] = jnp.sin(base)

    col = lax.broadcasted_iota(jnp.int32, (1, seq), 1)
    lead = angle((i * tr) * col)
    c0, s0 = jnp.cos(lead), jnp.sin(lead)
    cb, sb = cb_ref[...], sb_ref[...]
    cosv = c0 * cb - s0 * sb
    nsin = -(s0 * cb + c0 * sb)
    c_ref[...] = cosv.astype(c_ref.dtype)
    s_ref[...] = jnp.where(r == 0, jnp.where((c & 1) == 0, 1.0, -1.0), nsin).astype(s_ref.dtype)
    st_ref[...] = jnp.where(c == 0, jnp.where((r & 1) == 0, 1.0, -1.0), nsin).astype(st_ref.dtype)


def _dft_tables(seq):
    assert seq & (seq - 1) == 0, "sequence length must be a power of two"
    tr = _tile(seq, 256)
    spec = pl.BlockSpec((tr, seq), lambda i: (i, 0))
    out = jax.ShapeDtypeStruct((seq, seq), BF16)
    return pl.pallas_call(
        functools.partial(_tab_kernel, seq=seq),
        grid=(seq // tr,),
        out_specs=[spec, spec, spec],
        out_shape=[out, out, out],
        scratch_shapes=[pltpu.VMEM((tr, seq), F32), pltpu.VMEM((tr, seq), F32)],
        compiler_params=_cparams("arbitrary"),
        name="dft_tables",
    )()


def _filt_kernel(z_ref, win_ref, bin_ref, wmid_ref, bmid_ref, fr_ref, wout_ref, dl_ref,
                 hs_ref, l1_ref, *, seq):
    i = pl.program_id(0)
    tl = z_ref.shape[0]
    width = dl_ref.shape[1]
    hi = lax.Precision.HIGHEST
    fr = fr_ref[...]
    h = jnp.sin(fr * (jnp.dot(z_ref[...], win_ref[...], precision=hi, preferred_element_type=F32)
                      + bin_ref[...]))
    for j in range(wmid_ref.shape[0]):
        h = jnp.sin(fr * (jnp.dot(h, wmid_ref[j], precision=hi, preferred_element_type=F32)
                          + bmid_ref[j]))
    ho = jnp.dot(h, wout_ref[...], precision=hi, preferred_element_type=F32)
    pos = i * tl + lax.broadcasted_iota(jnp.int32, (tl, 1), 0)
    t = pos.astype(F32) / float(seq - 1)
    window = jnp.exp(-t * dl_ref[...])
    h_fwd = ho[:, :width] * window
    h_bwd = jnp.where(pos == 0, 0.0, ho[:, width:] * window)
    hs_ref[:, :width] = h_fwd + h_bwd
    hs_ref[:, width:] = h_fwd - h_bwd

    @pl.when(i == 0)
    def _():
        l1_ref[...] = jnp.zeros_like(l1_ref)

    l1_ref[...] += jnp.sum(jnp.abs(h_fwd) + jnp.abs(h_bwd), axis=0, keepdims=True)


def _fspec_kernel(c_ref, s_ref, hs_ref, l1_ref, a_ref, b_ref, d_ref, *, seq):
    i = pl.program_id(0)
    tk = c_ref.shape[0]
    width = l1_ref.shape[1]
    hs = hs_ref[...].astype(BF16)
    xc = jnp.dot(c_ref[...], hs[:, :width], preferred_element_type=F32)
    xs = jnp.dot(s_ref[...], hs, preferred_element_type=F32)
    k = i * tk + lax.broadcasted_iota(jnp.int32, (tk, 1), 0)
    inv_l1 = 1.0 / l1_ref[...]
    wk = jnp.where(k == 0, 1.0, 2.0) / float(2 * seq)
    a = xc * wk * inv_l1
    a_ref[...] = a
    b_ref[...] = jnp.where(k == 0, 0.0, xs[:, width:] * wk * inv_l1)
    d_ref[...] = jnp.where(k == 0, xs[:, :width] * wk * inv_l1, a)


def _hyena_filter_spectrum(seq, tabs, w_in, b_in, w_mid, b_mid, freq, w_out):
    c_tab, s_tab, _ = tabs
    emb, fw = w_in.shape
    width = w_out.shape[1] // 2
    n = np.arange(seq, dtype=np.float64)[:, None]
    f = np.linspace(1e-4, HY_BANDS - 1, HY_BANDS)[None]
    w = 2.0 * math.pi * n / seq
    z = np.concatenate([n / (seq - 1), np.cos(f * w), -np.sin(f * w)], -1)
    z = np.pad(z, ((0, 0), (0, V7X_LANES - emb))).astype(np.float32)
    w_in_p = jnp.pad(w_in.astype(F32), ((0, V7X_LANES - emb), (0, 0)))
    deltas = np.abs(np.linspace(math.log(HY_TARGET) / HY_SLOW_PCT, math.log(HY_TARGET) / HY_FAST_PCT,
                                width)).astype(np.float32)[None]
    tl = _tile(seq, 512)
    full = lambda *shape: pl.BlockSpec(shape, lambda i: (0,) * len(shape))
    hs, l1 = pl.pallas_call(
        functools.partial(_filt_kernel, seq=seq),
        grid=(seq // tl,),
        in_specs=[pl.BlockSpec((tl, V7X_LANES), lambda i: (i, 0)),
                  full(V7X_LANES, fw), full(1, fw), full(*w_mid.shape), full(w_mid.shape[0], 1, fw),
                  full(1, fw), full(fw, 2 * width), full(1, width)],
        out_specs=[pl.BlockSpec((tl, 2 * width), lambda i: (i, 0)), full(1, width)],
        out_shape=[jax.ShapeDtypeStruct((seq, 2 * width), F32), jax.ShapeDtypeStruct((1, width), F32)],
        compiler_params=_cparams("arbitrary"),
        name="hyena_filter",
    )(jnp.asarray(z), w_in_p, b_in.reshape(1, fw).astype(F32), w_mid.astype(F32),
      b_mid.reshape(w_mid.shape[0], 1, fw).astype(F32), freq.reshape(1, fw).astype(F32),
      w_out.astype(F32), jnp.asarray(deltas))

    tk = _tile(seq, 256)
    ospec = pl.BlockSpec((tk, width), lambda i: (i, 0))
    out = jax.ShapeDtypeStruct((seq, width), F32)
    return pl.pallas_call(
        functools.partial(_fspec_kernel, seq=seq),
        grid=(seq // tk,),
        in_specs=[pl.BlockSpec((tk, seq), lambda i: (i, 0)), pl.BlockSpec((tk, seq), lambda i: (i, 0)),
                  full(seq, 2 * width), full(1, width)],
        out_specs=[ospec, ospec, ospec],
        out_shape=[out, out, out],
        compiler_params=_cparams("parallel"),
        name="filter_spectrum",
    )(c_tab, s_tab, hs, l1)


def _dft_fwd_kernel(c_ref, s_ref, u_ref, a_ref, b_ref, d_ref, yre_ref, yim_ref):
    u = u_ref[0]
    x_re = jnp.dot(c_ref[...], u, preferred_element_type=F32)
    x_im = jnp.dot(s_ref[...], u, preferred_element_type=F32)
    b = b_ref[...]
    yre_ref[0] = (x_re * a_ref[...] - x_im * b).astype(yre_ref.dtype)
    yim_ref[0] = (x_re * b + x_im * d_ref[...]).astype(yim_ref.dtype)


def _dft_inv_kernel(c_ref, st_ref, yre_ref, yim_ref, u_ref, x0_ref, bias_ref, o_ref):
    y = (jnp.dot(c_ref[...], yre_ref[0], preferred_element_type=F32)
         + jnp.dot(st_ref[...], yim_ref[0], preferred_element_type=F32))
    y = y + u_ref[0].astype(F32) * bias_ref[...]
    o_ref[0] = (x0_ref[0].astype(F32) * y).astype(o_ref.dtype)


def _long_conv(u, x0, tabs, spec, bias):
    c_tab, s_tab, st_tab = tabs
    a, b, d = spec
    bsz, seq, width = u.shape
    tk = _tile(seq, 512)
    tab = pl.BlockSpec((tk, seq), lambda j, bb: (j, 0))
    whole = pl.BlockSpec((1, seq, width), lambda j, bb: (bb, 0, 0))
    rows = pl.BlockSpec((1, tk, width), lambda j, bb: (bb, j, 0))
    coef = pl.BlockSpec((tk, width), lambda j, bb: (j, 0))
    spec_out = jax.ShapeDtypeStruct((bsz, seq, width), BF16)
    y_re, y_im = pl.pallas_call(
        _dft_fwd_kernel,
        grid=(seq // tk, bsz),
        in_specs=[tab, tab, whole, coef, coef, coef],
        out_specs=[rows, rows],
        out_shape=[spec_out, spec_out],
        compiler_params=_cparams("parallel", "arbitrary"),
        name="dft_fwd",
    )(c_tab, s_tab, u, a, b, d)
    return pl.pallas_call(
        _dft_inv_kernel,
        grid=(seq // tk, bsz),
        in_specs=[tab, tab, whole, whole, rows, rows, pl.BlockSpec((1, width), lambda j, bb: (0, 0))],
        out_specs=rows,
        out_shape=spec_out,
        compiler_params=_cparams("parallel", "arbitrary"),
        name="dft_inv",
    )(c_tab, st_tab, y_re, y_im, u, x0, bias.reshape(1, width).astype(F32))


def _merge_kernel(ya_ref, yh_ref, yc_ref, g0_ref, g1_ref, x_ref, wb_ref, wo_ref, lg_ref, lb_ref,
                  o_ref, m_ref, *, alpha):
    d = o_ref.shape[1]
    half = d // 2
    g0 = g0_ref[...]
    g1 = g1_ref[...]
    gates = ((g0[:, :half], g0[:, d:], g1[:, half:d]),
             (g0[:, half:d], g1[:, :half], g1[:, d:]))
    branches = (ya_ref[...], yh_ref[...], yc_ref[...])
    for c in range(2):
        acc = None
        for j in range(3):
            y = jnp.dot(branches[j], wb_ref[j, :, c * half:(c + 1) * half], preferred_element_type=F32)
            term = jax.nn.sigmoid(gates[c][j].astype(F32)) * y
            acc = term if acc is None else acc + term
        m_ref[:, c * half:(c + 1) * half] = acc.astype(m_ref.dtype)
    t = jnp.dot(m_ref[...], wo_ref[...], preferred_element_type=F32)
    o_ref[...] = _layer_norm(alpha * x_ref[...] + t, lg_ref[...], lb_ref[...])


def _merge(y_a, y_h, y_c, proj, x, wb, wo, ln_g, ln_b, alpha, gate_col):
    t, d = x.shape
    bw = y_a.shape[1]
    tm = _tile(t, 512)
    row = lambda w: pl.BlockSpec((tm, w), lambda i: (i, 0))
    fixed = lambda *shape: pl.BlockSpec(shape, lambda i: (0,) * len(shape))
    gw = 3 * d // 2
    return pl.pallas_call(
        functools.partial(_merge_kernel, alpha=alpha),
        grid=(t // tm,),
        in_specs=[row(bw), row(bw), row(bw),
                  pl.BlockSpec((tm, gw), lambda i: (i, gate_col)),
                  pl.BlockSpec((tm, gw), lambda i: (i, gate_col + 1)),
                  row(d), fixed(*wb.shape), fixed(*wo.shape), fixed(1, d), fixed(1, d)],
        out_specs=row(d),
        out_shape=jax.ShapeDtypeStruct((t, d), F32),
        scratch_shapes=[pltpu.VMEM((tm, d), BF16)],
        compiler_params=_cparams("parallel"),
        name="merge",
    )(y_a, y_h, y_c, proj, proj, x, wb, wo, ln_g.reshape(1, d), ln_b.reshape(1, d))


def _swiglu_partial(xb, wg, wu, wd):
    g = jnp.dot(xb, wg, preferred_element_type=F32)
    u = jnp.dot(xb, wu, preferred_element_type=F32)
    return jnp.dot((jax.nn.silu(g) * u).astype(BF16), wd, preferred_element_type=F32)


def _ffn_kernel(x_ref, wg_ref, wu_ref, wd_ref, lg_ref, lb_ref, o_ref, xb_ref, acc_ref, *, alpha):
    f = pl.program_id(1)

    @pl.when(f == 0)
    def _():
        xb_ref[...] = x_ref[...].astype(BF16)
        acc_ref[...] = jnp.zeros_like(acc_ref)

    acc_ref[...] += _swiglu_partial(xb_ref[...], wg_ref[...], wu_ref[...], wd_ref[...])

    @pl.when(f == pl.num_programs(1) - 1)
    def _():
        o_ref[...] = _layer_norm(alpha * x_ref[...] + acc_ref[...], lg_ref[...], lb_ref[...])


def _ffn(x, wg, wu, wd, ln_g, ln_b, alpha):
    t, d = x.shape
    ff = wg.shape[1]
    tm = _tile(t, 1024)
    tf = ff // 2 if (ff // 2) % V7X_LANES == 0 else ff
    vec = pl.BlockSpec((1, d), lambda i, f: (0, 0))
    return pl.pallas_call(
        functools.partial(_ffn_kernel, alpha=alpha),
        grid=(t // tm, ff // tf),
        in_specs=[pl.BlockSpec((tm, d), lambda i, f: (i, 0)),
                  pl.BlockSpec((d, tf), lambda i, f: (0, f)),
                  pl.BlockSpec((d, tf), lambda i, f: (0, f)),
                  pl.BlockSpec((tf, d), lambda i, f: (f, 0)),
                  vec, vec],
        out_specs=pl.BlockSpec((tm, d), lambda i, f: (i, 0)),
        out_shape=jax.ShapeDtypeStruct((t, d), F32),
        scratch_shapes=[pltpu.VMEM((tm, d), BF16), pltpu.VMEM((tm, d), F32)],
        compiler_params=_cparams("parallel", "arbitrary"),
        name="ffn",
    )(x, wg, wu, wd, ln_g.reshape(1, d), ln_b.reshape(1, d))


def _top2(logits, n_experts):
    lane = lax.broadcasted_iota(jnp.int32, logits.shape, 1).astype(F32)
    neg = jnp.float32(-jnp.inf)
    sentinel = jnp.float32(n_experts)
    lg = jnp.where(lane < sentinel, logits, neg)
    m1 = jnp.max(lg, axis=1, keepdims=True)
    i1 = jnp.min(jnp.where(lg == m1, lane, sentinel), axis=1, keepdims=True)
    rest = jnp.where(lane == i1, neg, lg)
    m2 = jnp.max(rest, axis=1, keepdims=True)
    i2 = jnp.min(jnp.where(rest == m2, lane, sentinel), axis=1, keepdims=True)
    e2 = jnp.exp(m2 - m1)
    p1 = 1.0 / (1.0 + e2)
    p2 = e2 / (1.0 + e2)
    return lane == i1, lane == i2, i1, i2, p1, p2


ROUTE_COLS = 8
MOE_TILE_ROWS = 512
MOE_TOKEN_TILE = 512
ROW_DMA_UNROLL = 8


def _router_kernel(x_ref, r_ref, route_ref, cnt_ref, tri_ref, *, n_experts):
    i = pl.program_id(0)
    x = x_ref[...]
    tm = x.shape[0]

    @pl.when(i == 0)
    def _():
        cnt_ref[...] = jnp.zeros_like(cnt_ref)
        row = lax.broadcasted_iota(jnp.int32, (tm, tm), 0)
        col = lax.broadcasted_iota(jnp.int32, (tm, tm), 1)
        tri_ref[...] = jnp.where(col < row, 1.0, 0.0).astype(tri_ref.dtype)

    w = r_ref[...]
    x_hi, w_hi = x.astype(BF16), w.astype(BF16)
    x_lo, w_lo = (x - x_hi.astype(F32)).astype(BF16), (w - w_hi.astype(F32)).astype(BF16)
    logits = (jnp.dot(x_hi, w_hi, preferred_element_type=F32) + jnp.dot(x_lo, w_hi, preferred_element_type=F32)
              + jnp.dot(x_hi, w_lo, preferred_element_type=F32))
    sel1, sel2, i1, i2, p1, p2 = _top2(logits, n_experts)
    chosen = jnp.where(sel1, 1.0, jnp.where(sel2, 1.0, 0.0))
    before = jnp.dot(tri_ref[...], chosen.astype(tri_ref.dtype), preferred_element_type=F32) + cnt_ref[...]
    r1 = jnp.sum(jnp.where(sel1, before, 0.0), axis=1, keepdims=True)
    r2 = jnp.sum(jnp.where(sel2, before, 0.0), axis=1, keepdims=True)
    cnt_ref[...] += jnp.sum(chosen, axis=0, keepdims=True)
    lane = lax.broadcasted_iota(jnp.int32, logits.shape, 1)
    record = jnp.zeros_like(logits)
    for c, val in enumerate((i1, i2, p1, p2, r1, r2)):
        record = jnp.where(lane == c, val, record)
    route_ref[...] = record[:, :ROUTE_COLS]


def _moe_scatter_kernel(pos_ref, pad_ref, tail_ref, x_ref, xs_ref, zero_ref, sem, zsem):
    tm = x_ref.shape[0]
    tr = zero_ref.shape[0]

    @pl.when(pl.program_id(0) == 0)
    def _():
        zero_ref[...] = jnp.zeros_like(zero_ref)

        def pad_copy(first, r):
            return pltpu.make_async_copy(zero_ref.at[pl.ds(0, 1)], xs_ref.at[pl.ds(first + r, 1)], zsem)

        def tile_copy(t):
            rows = pl.ds(pl.multiple_of(t * tr, tr), tr)
            return pltpu.make_async_copy(zero_ref, xs_ref.at[rows], zsem)

        def each(copy_fn, method):
            for e in range(pad_ref.shape[0]):
                first, count = pad_ref[e, 0], pad_ref[e, 1]
                lax.fori_loop(0, count, lambda r, c: (getattr(copy_fn(first, r), method)(), c)[1], 0)

        each(pad_copy, "start")
        lax.fori_loop(tail_ref[0], tail_ref[1], lambda t, c: (tile_copy(t).start(), c)[1], 0)
        each(pad_copy, "wait")
        lax.fori_loop(tail_ref[0], tail_ref[1], lambda t, c: (tile_copy(t).wait(), c)[1], 0)

    def row_copy(r, k):
        slot = pos_ref[0, 0, k * tm + r]
        return pltpu.make_async_copy(x_ref.at[pl.ds(r, 1)], xs_ref.at[pl.ds(slot, 1)], sem)

    def start(r, carry):
        row_copy(r, 0).start()
        row_copy(r, 1).start()
        return carry

    def wait(r, carry):
        row_copy(r, 0).wait()
        row_copy(r, 1).wait()
        return carry

    lax.fori_loop(0, tm, start, 0, unroll=ROW_DMA_UNROLL)
    lax.fori_loop(0, tm, wait, 0, unroll=ROW_DMA_UNROLL)


def _gmm_kernel(te_ref, nu_ref, xs_ref, wg_ref, wu_ref, wd_ref, ys_ref, xb_ref, acc_ref):
    del te_ref
    j = pl.program_id(0)
    f = pl.program_id(1)

    @pl.when(jnp.logical_and(j >= nu_ref[0], f == 0))
    def _():
        ys_ref[...] = jnp.zeros_like(ys_ref)

    @pl.when(j < nu_ref[0])
    def _():
        @pl.when(f == 0)
        def _():
            xb_ref[...] = xs_ref[...].astype(BF16)
            acc_ref[...] = jnp.zeros_like(acc_ref)

        acc_ref[...] += _swiglu_partial(xb_ref[...], wg_ref[...], wu_ref[...], wd_ref[...])

        @pl.when(f == pl.num_programs(1) - 1)
        def _():
            ys_ref[...] = acc_ref[...]


def _moe_combine_kernel(pos_ref, route_ref, x_ref, lg_ref, lb_ref, ys_ref, o_ref, y1_ref, y2_ref, sem,
                        *, alpha):
    tm = x_ref.shape[0]

    def row_copy(r, k):
        slot = pos_ref[0, 0, k * tm + r]
        dst = (y1_ref, y2_ref)[k]
        return pltpu.make_async_copy(ys_ref.at[pl.ds(slot, 1)], dst.at[pl.ds(r, 1)], sem)

    def start(r, carry):
        row_copy(r, 0).start()
        row_copy(r, 1).start()
        return carry

    def wait(r, carry):
        row_copy(r, 0).wait()
        row_copy(r, 1).wait()
        return carry

    lax.fori_loop(0, tm, start, 0, unroll=ROW_DMA_UNROLL)
    lax.fori_loop(0, tm, wait, 0, unroll=ROW_DMA_UNROLL)
    route = route_ref[...]
    mixed = route[:, 2:3] * y1_ref[...] + route[:, 3:4] * y2_ref[...]
    o_ref[...] = _layer_norm(alpha * x_ref[...] + mixed, lg_ref[...], lb_ref[...])


def _moe(x, router, wg, wu, wd, ln_g, ln_b, alpha):
    t, d = x.shape
    n_experts, _, ff = wg.shape
    tm = _tile(t, MOE_TOKEN_TILE)
    tr = MOE_TILE_ROWS
    n_tok_tiles = t // tm

    router_p = jnp.pad(router.astype(F32), ((0, 0), (0, V7X_LANES - n_experts)))
    route, counts = pl.pallas_call(
        functools.partial(_router_kernel, n_experts=n_experts),
        grid=(n_tok_tiles,),
        in_specs=[pl.BlockSpec((tm, d), lambda i: (i, 0)),
                  pl.BlockSpec((d, V7X_LANES), lambda i: (0, 0))],
        out_specs=[pl.BlockSpec((tm, ROUTE_COLS), lambda i: (i, 0)),
                   pl.BlockSpec((1, V7X_LANES), lambda i: (0, 0))],
        out_shape=[jax.ShapeDtypeStruct((t, ROUTE_COLS), F32), jax.ShapeDtypeStruct((1, V7X_LANES), F32)],
        scratch_shapes=[pltpu.VMEM((tm, tm), BF16)],
        compiler_params=_cparams("arbitrary"),
        name="moe_router",
    )(x, router_p)

    cnt = counts[0, :n_experts].astype(jnp.int32)
    padded = (cnt + tr - 1) // tr * tr
    ends = jnp.cumsum(padded)
    starts = ends - padded
    expert_ids = jnp.arange(n_experts, dtype=jnp.int32)

    def slots(expert_col, rank_col):
        onehot = route[:, expert_col:expert_col + 1].astype(jnp.int32) == expert_ids[None]
        return jnp.sum(jnp.where(onehot, starts[None], 0), axis=1) + route[:, rank_col].astype(jnp.int32)

    pos = jnp.concatenate([slots(0, 4).reshape(n_tok_tiles, 1, tm), slots(1, 5).reshape(n_tok_tiles, 1, tm)],
                          axis=2)
    n_tiles = (TOP_K * t) // tr + n_experts
    n_used = (ends[-1] // tr).reshape(1)
    tile_start = jnp.arange(n_tiles, dtype=jnp.int32) * tr
    tile_expert = jnp.sum((tile_start[:, None] >= ends[None]).astype(jnp.int32), axis=1)
    last_expert = jnp.sum((ends[-1] - tr >= ends).astype(jnp.int32))
    tile_expert = jnp.where(tile_start < ends[-1], tile_expert, last_expert)

    pos_spec = pl.BlockSpec((1, 1, 2 * tm), lambda i: (i, 0, 0), memory_space=pltpu.SMEM)
    smem_spec = pl.BlockSpec(memory_space=pltpu.SMEM)
    any_spec = pl.BlockSpec(memory_space=pl.ANY)
    pad_rows = jnp.stack([starts + cnt, padded - cnt], axis=1)
    tail_tiles = jnp.concatenate([n_used, jnp.full((1,), n_tiles, jnp.int32)])
    xs = pl.pallas_call(
        _moe_scatter_kernel,
        grid=(n_tok_tiles,),
        in_specs=[pos_spec, smem_spec, smem_spec, pl.BlockSpec((tm, d), lambda i: (i, 0))],
        out_specs=any_spec,
        out_shape=jax.ShapeDtypeStruct((n_tiles * tr, d), F32),
        scratch_shapes=[pltpu.VMEM((tr, d), F32), pltpu.SemaphoreType.DMA(()), pltpu.SemaphoreType.DMA(())],
        compiler_params=_cparams("arbitrary"),
        name="moe_scatter",
    )(pos, pad_rows, tail_tiles, x)

    tf = ff // 2 if (ff // 2) % V7X_LANES == 0 else ff
    nf = ff // tf
    used = lambda j, nu: jnp.minimum(j, nu[0] - 1)
    fcol = lambda j, f, nu: jnp.where(j < nu[0], f, nf - 1)
    ys = pl.pallas_call(
        _gmm_kernel,
        grid_spec=pltpu.PrefetchScalarGridSpec(
            num_scalar_prefetch=2,
            grid=(n_tiles, nf),
            in_specs=[pl.BlockSpec((tr, d), lambda j, f, te, nu: (used(j, nu), 0)),
                      pl.BlockSpec((None, d, tf), lambda j, f, te, nu: (te[j], 0, fcol(j, f, nu))),
                      pl.BlockSpec((None, d, tf), lambda j, f, te, nu: (te[j], 0, fcol(j, f, nu))),
                      pl.BlockSpec((None, tf, d), lambda j, f, te, nu: (te[j], fcol(j, f, nu), 0))],
            out_specs=pl.BlockSpec((tr, d), lambda j, f, te, nu: (j, 0)),
            scratch_shapes=[pltpu.VMEM((tr, d), BF16), pltpu.VMEM((tr, d), F32)]),
        out_shape=jax.ShapeDtypeStruct((n_tiles * tr, d), F32),
        compiler_params=_cparams("arbitrary", "arbitrary"),
        name="moe_experts",
    )(tile_expert, n_used, xs, wg, wu, wd)

    vec = pl.BlockSpec((1, d), lambda i: (0, 0))
    return pl.pallas_call(
        functools.partial(_moe_combine_kernel, alpha=alpha),
        grid=(n_tok_tiles,),
        in_specs=[pos_spec, pl.BlockSpec((tm, ROUTE_COLS), lambda i: (i, 0)),
                  pl.BlockSpec((tm, d), lambda i: (i, 0)), vec, vec, any_spec],
        out_specs=pl.BlockSpec((tm, d), lambda i: (i, 0)),
        out_shape=jax.ShapeDtypeStruct((t, d), F32),
        scratch_shapes=[pltpu.VMEM((tm, d), F32), pltpu.VMEM((tm, d), F32), pltpu.SemaphoreType.DMA(())],
        compiler_params=_cparams("arbitrary"),
        name="moe_combine",
    )(pos, route, x, ln_g.reshape(1, d), ln_b.reshape(1, d), ys)


def _run_trunk(x3, p, tables):
    bsz, seq, d = x3.shape
    depth = p['w_in'].shape[0]
    alpha = (2 * depth) ** 0.25
    att_w = ATT_HEADS * HEAD_W
    width = p['sc_conv_w'].shape[2]
    gate_w = 3 * d // 2
    assert att_w % width == 0 and (3 * att_w + 6 * width) % gate_w == 0, "column blocks must align"
    hy_col = att_w // width
    d_in = p['w_in'].shape[2]
    col_scale = jnp.asarray(np.where(np.arange(d_in) < att_w, LOG2E * ATT_HEAD_DIM ** -0.5, 1.0)
                            .astype(np.float32)[None])
    x = x3.reshape(bsz * seq, d)
    for i in range(depth):
        proj = _proj(x, p['w_in'][i], col_scale)
        proj3 = proj.reshape(bsz, seq, proj.shape[1])
        y_a = _attention(proj3, p['att_lambda'][i], p['att_subln_g'][i], i)
        u, x0, y_c = _convs(proj3, p['hy_short_w'][i], p['sc_conv_w'][i], hy_col=hy_col, sc_col=hy_col + 1)
        spec = _hyena_filter_spectrum(seq, tables, p['hf_w_in'][i], p['hf_b_in'][i], p['hf_w_mid'][i],
                                      p['hf_b_mid'][i], p['hf_freq'][i], p['hf_w_out'][i])
        y_h = _long_conv(u, x0, tables, spec, p['hy_bias'][i])
        flat = lambda a: a.reshape(bsz * seq, a.shape[2])
        j = i // 2
        x = _merge(flat(y_a), flat(y_h), flat(y_c), proj, x, p['w_branch'][i], p['w_out'][i],
                   p['ln1_g'][i], p['ln1_b'][i], alpha, gate_col=(3 * att_w + 6 * width) // gate_w)
        if i % 2 == 0:
            x = _ffn(x, p['ffn_w_gate'][j], p['ffn_w_up'][j], p['ffn_w_down'][j],
                     p['ln2_g'][i], p['ln2_b'][i], alpha)
        else:
            x = _moe(x, p['router_w'][j], p['moe_w_gate'][j], p['moe_w_up'][j], p['moe_w_down'][j],
                     p['ln2_g'][i], p['ln2_b'][i], alpha)
    return x.reshape(bsz, seq, d)


def kernel(x_prompt, x_sample, w_in, att_lambda, att_subln_g, hy_short_w, hf_w_in, hf_b_in,
           hf_w_mid, hf_b_mid, hf_freq, hf_w_out, hy_bias, sc_conv_w, w_branch, w_out,
           ln1_g, ln1_b, ln2_g, ln2_b, ffn_w_gate, ffn_w_up, ffn_w_down, router_w,
           moe_w_gate, moe_w_up, moe_w_down):
    bf = lambda a: a.astype(BF16)
    p = dict(w_in=bf(w_in), att_lambda=att_lambda, att_subln_g=att_subln_g, hy_short_w=hy_short_w,
             hf_w_in=hf_w_in, hf_b_in=hf_b_in, hf_w_mid=hf_w_mid, hf_b_mid=hf_b_mid,
             hf_freq=hf_freq, hf_w_out=hf_w_out, hy_bias=hy_bias, sc_conv_w=sc_conv_w,
             w_branch=bf(w_branch), w_out=bf(w_out), ln1_g=ln1_g, ln1_b=ln1_b, ln2_g=ln2_g, ln2_b=ln2_b,
             ffn_w_gate=bf(ffn_w_gate), ffn_w_up=bf(ffn_w_up), ffn_w_down=bf(ffn_w_down),
             router_w=router_w, moe_w_gate=bf(moe_w_gate), moe_w_up=bf(moe_w_up),
             moe_w_down=bf(moe_w_down))
    tables = {}
    outs = []
    for x3 in (x_prompt, x_sample):
        seq = x3.shape[1]
        if seq not in tables:
            tables[seq] = _dft_tables(seq)
        outs.append(_run_trunk(x3, p, tables[seq]))
    return tuple(outs)
```

```python
import functools
import math

import numpy as np
import jax
import jax.numpy as jnp
from jax import lax
from jax.experimental import pallas as pl
from jax.experimental.pallas import tpu as pltpu

F32 = jnp.float32
BF16 = jnp.bfloat16

ATT_HEADS = 4
ATT_HEAD_DIM = 64
HEAD_W = 2 * ATT_HEAD_DIM
HY_BANDS = 16
HY_TARGET = 1e-2
HY_FAST_PCT = 0.3
HY_SLOW_PCT = 1.5
TOP_K = 2
LN_EPS = 1e-5
RMS_EPS = 1e-5
LOG2E = math.log2(math.e)
ATT_Q_TILE = 256
V7X_LANES = 128
V7X_BF16_SUBLANES = 16
V7X_VMEM_LIMIT = 56 * 1024 * 1024


def _cparams(*sem):
    return pltpu.CompilerParams(dimension_semantics=sem, vmem_limit_bytes=V7X_VMEM_LIMIT)


def _tile(n, pref):
    t = min(n, pref)
    while n % t:
        t //= 2
    return t


def _layer_norm(v, g, b):
    mu = jnp.mean(v, axis=-1, keepdims=True)
    d = v - mu
    var = jnp.mean(d * d, axis=-1, keepdims=True)
    return d * lax.rsqrt(var + LN_EPS) * g + b


def _proj_kernel(x_ref, w_ref, cs_ref, o_ref, xb_ref):
    @pl.when(pl.program_id(1) == 0)
    def _():
        xb_ref[...] = x_ref[...].astype(BF16)

    acc = jnp.dot(xb_ref[...], w_ref[...], preferred_element_type=F32)
    o_ref[...] = (acc * cs_ref[...]).astype(o_ref.dtype)


def _proj(x, w, col_scale):
    t, d = x.shape
    n = w.shape[1]
    tm = _tile(t, 1024)
    tn = _tile(n, 1536)
    return pl.pallas_call(
        _proj_kernel,
        grid=(t // tm, n // tn),
        in_specs=[pl.BlockSpec((tm, d), lambda i, j: (i, 0)),
                  pl.BlockSpec((d, tn), lambda i, j: (0, j)),
                  pl.BlockSpec((1, tn), lambda i, j: (0, j))],
        out_specs=pl.BlockSpec((tm, tn), lambda i, j: (i, j)),
        out_shape=jax.ShapeDtypeStruct((t, n), BF16),
        scratch_shapes=[pltpu.VMEM((tm, d), BF16)],
        compiler_params=_cparams("parallel", "arbitrary"),
        name="proj",
    )(x, w, col_scale)


def _attn_kernel(slopes_ref, lam_ref, g_ref, q_ref, k_ref, v_ref, o_ref, bias_ref, *, lam_init, tq):
    h = pl.program_id(0)
    b = pl.program_id(1)
    qi = pl.program_id(2)
    seq = k_ref.shape[1]

    @pl.when(jnp.logical_and(b == 0, qi == 0))
    def _():
        r = lax.broadcasted_iota(jnp.int32, bias_ref.shape, 0)
        j = lax.broadcasted_iota(jnp.int32, bias_ref.shape, 1)
        bias_ref[...] = (-LOG2E * slopes_ref[h]) * jnp.abs(r - j + (seq - tq)).astype(F32)

    start = pl.multiple_of((pl.num_programs(2) - 1 - qi) * tq, tq)
    lf = lam_ref[...].astype(F32)
    lam_full = (jnp.exp(jnp.sum(lf[0:1] * lf[1:2], axis=1, keepdims=True))
                - jnp.exp(jnp.sum(lf[2:3] * lf[3:4], axis=1, keepdims=True)) + lam_init)
    q = q_ref[0]
    k = k_ref[0]
    v = v_ref[0]
    bias = bias_ref[:, pl.ds(start, seq)]
    lane = lax.broadcasted_iota(jnp.int32, q.shape, 1)
    zero = jnp.zeros_like(q)

    def softmax_pv(qm):
        s = lax.dot_general(qm, k, (((1,), (1,)), ((), ())), preferred_element_type=F32)
        m = jnp.max(s + bias, axis=1, keepdims=True)
        e = jnp.exp2(s + (bias - m))
        l = jnp.sum(e, axis=1, keepdims=True)
        return jnp.dot(e.astype(BF16), v, preferred_element_type=F32) * (1.0 / l)

    o = (softmax_pv(jnp.where(lane < ATT_HEAD_DIM, q, zero))
         - lam_full * softmax_pv(jnp.where(lane >= ATT_HEAD_DIM, q, zero)))
    o = o * lax.rsqrt(jnp.mean(o * o, axis=-1, keepdims=True) + RMS_EPS) * g_ref[...].astype(F32)
    o_ref[0] = (o * (1.0 - lam_init)).astype(o_ref.dtype)


def _attention(proj3, lam, subln_g, layer):
    bsz, seq, _ = proj3.shape
    tq = _tile(seq, ATT_Q_TILE if layer == 0 else (512 if seq <= 2048 else 128))
    lam_init = 0.8 - 0.6 * math.exp(-0.3 * layer)
    slopes = jnp.asarray(2.0 ** (-8.0 * np.arange(1, ATT_HEADS + 1, dtype=np.float32) / ATT_HEADS))
    kern = functools.partial(_attn_kernel, lam_init=lam_init, tq=tq)
    return pl.pallas_call(
        kern,
        grid=(ATT_HEADS, bsz, seq // tq),
        in_specs=[pl.BlockSpec(memory_space=pltpu.SMEM),
                  pl.BlockSpec((4, ATT_HEAD_DIM), lambda h, b, i: (0, 0)),
                  pl.BlockSpec((1, HEAD_W), lambda h, b, i: (0, 0)),
                  pl.BlockSpec((1, tq, HEAD_W), lambda h, b, i: (b, i, h)),
                  pl.BlockSpec((1, seq, HEAD_W), lambda h, b, i: (b, 0, ATT_HEADS + h)),
                  pl.BlockSpec((1, seq, HEAD_W), lambda h, b, i: (b, 0, 2 * ATT_HEADS + h))],
        out_specs=pl.BlockSpec((1, tq, HEAD_W), lambda h, b, i: (b, i, h)),
        out_shape=jax.ShapeDtypeStruct((bsz, seq, ATT_HEADS * HEAD_W), BF16),
        scratch_shapes=[pltpu.VMEM((tq, 2 * seq - tq), F32)],
        compiler_params=_cparams("arbitrary", "arbitrary", "arbitrary"),
        name="attn",
    )(slopes, lam, subln_g.reshape(1, HEAD_W), proj3, proj3, proj3)


def _conv3(x, prev_row, next_row, w):
    rows = x.shape[0]
    row = lax.broadcasted_iota(jnp.int32, (rows, 1), 0)
    x_prev = jnp.where(row == 0, prev_row, pltpu.roll(x, 1, axis=0))
    x_next = jnp.where(row == rows - 1, next_row, pltpu.roll(x, rows - 1, axis=0))
    return w[0:1] * x_prev + w[1:2] * x + w[2:3] * x_next


def _conv_kernel(hy_ref, hyp_ref, hyn_ref, sc_ref, scp_ref, scn_ref, hw_ref, sw_ref,
                 u_ref, x0_ref, yc_ref):
    i = pl.program_id(1)
    last = pl.num_programs(1) - 1
    width = u_ref.shape[2]
    has_prev = jnp.where(i > 0, 1.0, 0.0)
    has_next = jnp.where(i < last, 1.0, 0.0)
    hsub = hyp_ref.shape[1]

    hy = hy_ref[0].astype(F32)
    hy_prev = hyp_ref[0].astype(F32)[hsub - 1:hsub] * has_prev
    hy_next = hyn_ref[0].astype(F32)[0:1] * has_next
    hyc = _conv3(hy, hy_prev, hy_next, hw_ref[...].astype(F32))
    x0_ref[0] = hyc[:, :width].astype(x0_ref.dtype)
    u_ref[0] = (hyc[:, 2 * width:] * hyc[:, width:2 * width]).astype(u_ref.dtype)

    def gated(blk):
        return blk[:, width:2 * width] * blk[:, 2 * width:]

    sc = sc_ref[0].astype(F32)
    cx_prev = gated(scp_ref[0].astype(F32)[hsub - 1:hsub]) * has_prev
    cx_next = gated(scn_ref[0].astype(F32)[0:1]) * has_next
    conv = _conv3(gated(sc), cx_prev, cx_next, sw_ref[...].astype(F32))
    yc_ref[0] = (sc[:, :width] * conv).astype(yc_ref.dtype)


def _convs(proj3, hy_w, sc_w, hy_col, sc_col):
    bsz, seq, _ = proj3.shape
    width = sc_w.shape[1]
    tl = _tile(seq, 512)
    hs = V7X_BF16_SUBLANES
    nh = tl // hs
    nblk = seq // hs

    def main(col):
        return pl.BlockSpec((1, tl, 3 * width), lambda b, i: (b, i, col))

    def prev(col):
        return pl.BlockSpec((1, hs, 3 * width), lambda b, i: (b, jnp.maximum(i * nh - 1, 0), col))

    def nxt(col):
        return pl.BlockSpec((1, hs, 3 * width), lambda b, i: (b, jnp.minimum((i + 1) * nh, nblk - 1), col))

    out = jax.ShapeDtypeStruct((bsz, seq, width), BF16)
    ospec = pl.BlockSpec((1, tl, width), lambda b, i: (b, i, 0))
    return pl.pallas_call(
        _conv_kernel,
        grid=(bsz, seq // tl),
        in_specs=[main(hy_col), prev(hy_col), nxt(hy_col), main(sc_col), prev(sc_col), nxt(sc_col),
                  pl.BlockSpec((3, 3 * width), lambda b, i: (0, 0)),
                  pl.BlockSpec((3, width), lambda b, i: (0, 0))],
        out_specs=[ospec, ospec, ospec],
        out_shape=[out, out, out],
        compiler_params=_cparams("parallel", "arbitrary"),
        name="conv",
    )(proj3, proj3, proj3, proj3, proj3, proj3, hy_w, sc_w)


def _tab_kernel(c_ref, s_ref, st_ref, cb_ref, sb_ref, *, seq):
    tr = c_ref.shape[0]
    i = pl.program_id(0)
    dr = lax.broadcasted_iota(jnp.int32, (tr, seq), 0)
    c = lax.broadcasted_iota(jnp.int32, (tr, seq), 1)
    r = i * tr + dr

    def angle(prod):
        return (prod & (2 * seq - 1)).astype(F32) * (math.pi / seq)

    @pl.when(i == 0)
    def _():
        base = angle(dr * c)
        cb_ref[...] = jnp.cos(base)
        sb_ref[...] = jnp.sin(base)

    col = lax.broadcasted_iota(jnp.int32, (1, seq), 1)
    lead = angle((i * tr) * col)
    c0, s0 = jnp.cos(lead), jnp.sin(lead)
    cb, sb = cb_ref[...], sb_ref[...]
    cosv = c0 * cb - s0 * sb
    nsin = -(s0 * cb + c0 * sb)
    c_ref[...] = cosv.astype(c_ref.dtype)
    s_ref[...] = jnp.where(r == 0, jnp.where((c & 1) == 0, 1.0, -1.0), nsin).astype(s_ref.dtype)
    st_ref[...] = jnp.where(c == 0, jnp.where((r & 1) == 0, 1.0, -1.0), nsin).astype(st_ref.dtype)


def _dft_tables(seq):
    assert seq & (seq - 1) == 0, "sequence length must be a power of two"
    tr = _tile(seq, 256)
    spec = pl.BlockSpec((tr, seq), lambda i: (i, 0))
    out = jax.ShapeDtypeStruct((seq, seq), BF16)
    return pl.pallas_call(
        functools.partial(_tab_kernel, seq=seq),
        grid=(seq // tr,),
        out_specs=[spec, spec, spec],
        out_shape=[out, out, out],
        scratch_shapes=[pltpu.VMEM((tr, seq), F32), pltpu.VMEM((tr, seq), F32)],
        compiler_params=_cparams("arbitrary"),
        name="dft_tables",
    )()


def _filt_kernel(z_ref, win_ref, bin_ref, wmid_ref, bmid_ref, fr_ref, wout_ref, dl_ref,
                 hs_ref, l1_ref, *, seq):
    i = pl.program_id(0)
    tl = z_ref.shape[0]
    width = dl_ref.shape[1]
    hi = lax.Precision.HIGHEST
    fr = fr_ref[...]
    h = jnp.sin(fr * (jnp.dot(z_ref[...], win_ref[...], precision=hi, preferred_element_type=F32)
                      + bin_ref[...]))
    for j in range(wmid_ref.shape[0]):
        h = jnp.sin(fr * (jnp.dot(h, wmid_ref[j], precision=hi, preferred_element_type=F32)
                          + bmid_ref[j]))
    ho = jnp.dot(h, wout_ref[...], precision=hi, preferred_element_type=F32)
    pos = i * tl + lax.broadcasted_iota(jnp.int32, (tl, 1), 0)
    t = pos.astype(F32) / float(seq - 1)
    window = jnp.exp(-t * dl_ref[...])
    h_fwd = ho[:, :width] * window
    h_bwd = jnp.where(pos == 0, 0.0, ho[:, width:] * window)
    hs_ref[:, :width] = h_fwd + h_bwd
    hs_ref[:, width:] = h_fwd - h_bwd

    @pl.when(i == 0)
    def _():
        l1_ref[...] = jnp.zeros_like(l1_ref)

    l1_ref[...] += jnp.sum(jnp.abs(h_fwd) + jnp.abs(h_bwd), axis=0, keepdims=True)


def _fspec_kernel(c_ref, s_ref, hs_ref, l1_ref, a_ref, b_ref, d_ref, *, seq):
    i = pl.program_id(0)
    tk = c_ref.shape[0]
    width = l1_ref.shape[1]
    hs = hs_ref[...].astype(BF16)
    xc = jnp.dot(c_ref[...], hs[:, :width], preferred_element_type=F32)
    xs = jnp.dot(s_ref[...], hs, preferred_element_type=F32)
    k = i * tk + lax.broadcasted_iota(jnp.int32, (tk, 1), 0)
    inv_l1 = 1.0 / l1_ref[...]
    wk = jnp.where(k == 0, 1.0, 2.0) / float(2 * seq)
    a = xc * wk * inv_l1
    a_ref[...] = a
    b_ref[...] = jnp.where(k == 0, 0.0, xs[:, width:] * wk * inv_l1)
    d_ref[...] = jnp.where(k == 0, xs[:, :width] * wk * inv_l1, a)


def _hyena_filter_spectrum(seq, tabs, w_in, b_in, w_mid, b_mid, freq, w_out):
    c_tab, s_tab, _ = tabs
    emb, fw = w_in.shape
    width = w_out.shape[1] // 2
    n = np.arange(seq, dtype=np.float64)[:, None]
    f = np.linspace(1e-4, HY_BANDS - 1, HY_BANDS)[None]
    w = 2.0 * math.pi * n / seq
    z = np.concatenate([n / (seq - 1), np.cos(f * w), -np.sin(f * w)], -1)
    z = np.pad(z, ((0, 0), (0, V7X_LANES - emb))).astype(np.float32)
    w_in_p = jnp.pad(w_in.astype(F32), ((0, V7X_LANES - emb), (0, 0)))
    deltas = np.abs(np.linspace(math.log(HY_TARGET) / HY_SLOW_PCT, math.log(HY_TARGET) / HY_FAST_PCT,
                                width)).astype(np.float32)[None]
    tl = _tile(seq, 512)
    full = lambda *shape: pl.BlockSpec(shape, lambda i: (0,) * len(shape))
    hs, l1 = pl.pallas_call(
        functools.partial(_filt_kernel, seq=seq),
        grid=(seq // tl,),
        in_specs=[pl.BlockSpec((tl, V7X_LANES), lambda i: (i, 0)),
                  full(V7X_LANES, fw), full(1, fw), full(*w_mid.shape), full(w_mid.shape[0], 1, fw),
                  full(1, fw), full(fw, 2 * width), full(1, width)],
        out_specs=[pl.BlockSpec((tl, 2 * width), lambda i: (i, 0)), full(1, width)],
        out_shape=[jax.ShapeDtypeStruct((seq, 2 * width), F32), jax.ShapeDtypeStruct((1, width), F32)],
        compiler_params=_cparams("arbitrary"),
        name="hyena_filter",
    )(jnp.asarray(z), w_in_p, b_in.reshape(1, fw).astype(F32), w_mid.astype(F32),
      b_mid.reshape(w_mid.shape[0], 1, fw).astype(F32), freq.reshape(1, fw).astype(F32),
      w_out.astype(F32), jnp.asarray(deltas))

    tk = _tile(seq, 256)
    ospec = pl.BlockSpec((tk, width), lambda i: (i, 0))
    out = jax.ShapeDtypeStruct((seq, width), F32)
    return pl.pallas_call(
        functools.partial(_fspec_kernel, seq=seq),
        grid=(seq // tk,),
        in_specs=[pl.BlockSpec((tk, seq), lambda i: (i, 0)), pl.BlockSpec((tk, seq), lambda i: (i, 0)),
                  full(seq, 2 * width), full(1, width)],
        out_specs=[ospec, ospec, ospec],
        out_shape=[out, out, out],
        compiler_params=_cparams("parallel"),
        name="filter_spectrum",
    )(c_tab, s_tab, hs, l1)


def _dft_fwd_kernel(c_ref, s_ref, u_ref, a_ref, b_ref, d_ref, yre_ref, yim_ref):
    u = u_ref[0]
    x_re = jnp.dot(c_ref[...], u, preferred_element_type=F32)
    x_im = jnp.dot(s_ref[...], u, preferred_element_type=F32)
    b = b_ref[...]
    yre_ref[0] = (x_re * a_ref[...] - x_im * b).astype(yre_ref.dtype)
    yim_ref[0] = (x_re * b + x_im * d_ref[...]).astype(yim_ref.dtype)


def _dft_inv_kernel(c_ref, st_ref, yre_ref, yim_ref, u_ref, x0_ref, bias_ref, o_ref):
    y = (jnp.dot(c_ref[...], yre_ref[0], preferred_element_type=F32)
         + jnp.dot(st_ref[...], yim_ref[0], preferred_element_type=F32))
    y = y + u_ref[0].astype(F32) * bias_ref[...]
    o_ref[0] = (x0_ref[0].astype(F32) * y).astype(o_ref.dtype)


def _long_conv(u, x0, tabs, spec, bias):
    c_tab, s_tab, st_tab = tabs
    a, b, d = spec
    bsz, seq, width = u.shape
    tk = _tile(seq, 512)
    tab = pl.BlockSpec((tk, seq), lambda j, bb: (j, 0))
    whole = pl.BlockSpec((1, seq, width), lambda j, bb: (bb, 0, 0))
    rows = pl.BlockSpec((1, tk, width), lambda j, bb: (bb, j, 0))
    coef = pl.BlockSpec((tk, width), lambda j, bb: (j, 0))
    spec_out = jax.ShapeDtypeStruct((bsz, seq, width), BF16)
    y_re, y_im = pl.pallas_call(
        _dft_fwd_kernel,
        grid=(seq // tk, bsz),
        in_specs=[tab, tab, whole, coef, coef, coef],
        out_specs=[rows, rows],
        out_shape=[spec_out, spec_out],
        compiler_params=_cparams("parallel", "arbitrary"),
        name="dft_fwd",
    )(c_tab, s_tab, u, a, b, d)
    return pl.pallas_call(
        _dft_inv_kernel,
        grid=(seq // tk, bsz),
        in_specs=[tab, tab, whole, whole, rows, rows, pl.BlockSpec((1, width), lambda j, bb: (0, 0))],
        out_specs=rows,
        out_shape=spec_out,
        compiler_params=_cparams("parallel", "arbitrary"),
        name="dft_inv",
    )(c_tab, st_tab, y_re, y_im, u, x0, bias.reshape(1, width).astype(F32))


def _merge_kernel(ya_ref, yh_ref, yc_ref, g0_ref, g1_ref, x_ref, wb_ref, wo_ref, lg_ref, lb_ref,
                  o_ref, m_ref, *, alpha):
    d = o_ref.shape[1]
    half = d // 2
    g0 = g0_ref[...]
    g1 = g1_ref[...]
    gates = ((g0[:, :half], g0[:, d:], g1[:, half:d]),
             (g0[:, half:d], g1[:, :half], g1[:, d:]))
    branches = (ya_ref[...], yh_ref[...], yc_ref[...])
    for c in range(2):
        acc = None
        for j in range(3):
            y = jnp.dot(branches[j], wb_ref[j, :, c * half:(c + 1) * half], preferred_element_type=F32)
            term = jax.nn.sigmoid(gates[c][j].astype(F32)) * y
            acc = term if acc is None else acc + term
        m_ref[:, c * half:(c + 1) * half] = acc.astype(m_ref.dtype)
    t = jnp.dot(m_ref[...], wo_ref[...], preferred_element_type=F32)
    o_ref[...] = _layer_norm(alpha * x_ref[...] + t, lg_ref[...], lb_ref[...])


def _merge(y_a, y_h, y_c, proj, x, wb, wo, ln_g, ln_b, alpha, gate_col):
    t, d = x.shape
    bw = y_a.shape[1]
    tm = _tile(t, 512)
    row = lambda w: pl.BlockSpec((tm, w), lambda i: (i, 0))
    fixed = lambda *shape: pl.BlockSpec(shape, lambda i: (0,) * len(shape))
    gw = 3 * d // 2
    return pl.pallas_call(
        functools.partial(_merge_kernel, alpha=alpha),
        grid=(t // tm,),
        in_specs=[row(bw), row(bw), row(bw),
                  pl.BlockSpec((tm, gw), lambda i: (i, gate_col)),
                  pl.BlockSpec((tm, gw), lambda i: (i, gate_col + 1)),
                  row(d), fixed(*wb.shape), fixed(*wo.shape), fixed(1, d), fixed(1, d)],
        out_specs=row(d),
        out_shape=jax.ShapeDtypeStruct((t, d), F32),
        scratch_shapes=[pltpu.VMEM((tm, d), BF16)],
        compiler_params=_cparams("parallel"),
        name="merge",
    )(y_a, y_h, y_c, proj, proj, x, wb, wo, ln_g.reshape(1, d), ln_b.reshape(1, d))


def _swiglu_partial(xb, wg, wu, wd):
    g = jnp.dot(xb, wg, preferred_element_type=F32)
    u = jnp.dot(xb, wu, preferred_element_type=F32)
    return jnp.dot((jax.nn.silu(g) * u).astype(BF16), wd, preferred_element_type=F32)


def _ffn_kernel(x_ref, wg_ref, wu_ref, wd_ref, lg_ref, lb_ref, o_ref, xb_ref, acc_ref, *, alpha):
    f = pl.program_id(1)

    @pl.when(f == 0)
    def _():
        xb_ref[...] = x_ref[...].astype(BF16)
        acc_ref[...] = jnp.zeros_like(acc_ref)

    acc_ref[...] += _swiglu_partial(xb_ref[...], wg_ref[...], wu_ref[...], wd_ref[...])

    @pl.when(f == pl.num_programs(1) - 1)
    def _():
        o_ref[...] = _layer_norm(alpha * x_ref[...] + acc_ref[...], lg_ref[...], lb_ref[...])


def _ffn(x, wg, wu, wd, ln_g, ln_b, alpha):
    t, d = x.shape
    ff = wg.shape[1]
    tm = _tile(t, 1024)
    tf = ff // 2 if (ff // 2) % V7X_LANES == 0 else ff
    vec = pl.BlockSpec((1, d), lambda i, f: (0, 0))
    return pl.pallas_call(
        functools.partial(_ffn_kernel, alpha=alpha),
        grid=(t // tm, ff // tf),
        in_specs=[pl.BlockSpec((tm, d), lambda i, f: (i, 0)),
                  pl.BlockSpec((d, tf), lambda i, f: (0, f)),
                  pl.BlockSpec((d, tf), lambda i, f: (0, f)),
                  pl.BlockSpec((tf, d), lambda i, f: (f, 0)),
                  vec, vec],
        out_specs=pl.BlockSpec((tm, d), lambda i, f: (i, 0)),
        out_shape=jax.ShapeDtypeStruct((t, d), F32),
        scratch_shapes=[pltpu.VMEM((tm, d), BF16), pltpu.VMEM((tm, d), F32)],
        compiler_params=_cparams("parallel", "arbitrary"),
        name="ffn",
    )(x, wg, wu, wd, ln_g.reshape(1, d), ln_b.reshape(1, d))


def _top2(logits, n_experts):
    lane = lax.broadcasted_iota(jnp.int32, logits.shape, 1).astype(F32)
    neg = jnp.float32(-jnp.inf)
    sentinel = jnp.float32(n_experts)
    lg = jnp.where(lane < sentinel, logits, neg)
    m1 = jnp.max(lg, axis=1, keepdims=True)
    i1 = jnp.min(jnp.where(lg == m1, lane, sentinel), axis=1, keepdims=True)
    rest = jnp.where(lane == i1, neg, lg)
    m2 = jnp.max(rest, axis=1, keepdims=True)
    i2 = jnp.min(jnp.where(rest == m2, lane, sentinel), axis=1, keepdims=True)
    e2 = jnp.exp(m2 - m1)
    p1 = 1.0 / (1.0 + e2)
    p2 = e2 / (1.0 + e2)
    return lane == i1, lane == i2, i1, i2, p1, p2


ROUTE_COLS = 8
MOE_TILE_ROWS = 512
MOE_TOKEN_TILE = 512
ROW_DMA_UNROLL = 8


def _router_kernel(x_ref, r_ref, route_ref, cnt_ref, tri_ref, *, n_experts):
    i = pl.program_id(0)
    x = x_ref[...]
    tm = x.shape[0]

    @pl.when(i == 0)
    def _():
        cnt_ref[...] = jnp.zeros_like(cnt_ref)
        row = lax.broadcasted_iota(jnp.int32, (tm, tm), 0)
        col = lax.broadcasted_iota(jnp.int32, (tm, tm), 1)
        tri_ref[...] = jnp.where(col < row, 1.0, 0.0).astype(tri_ref.dtype)

    w = r_ref[...]
    x_hi, w_hi = x.astype(BF16), w.astype(BF16)
    x_lo, w_lo = (x - x_hi.astype(F32)).astype(BF16), (w - w_hi.astype(F32)).astype(BF16)
    logits = (jnp.dot(x_hi, w_hi, preferred_element_type=F32) + jnp.dot(x_lo, w_hi, preferred_element_type=F32)
              + jnp.dot(x_hi, w_lo, preferred_element_type=F32))
    sel1, sel2, i1, i2, p1, p2 = _top2(logits, n_experts)
    chosen = jnp.where(sel1, 1.0, jnp.where(sel2, 1.0, 0.0))
    before = jnp.dot(tri_ref[...], chosen.astype(tri_ref.dtype), preferred_element_type=F32) + cnt_ref[...]
    r1 = jnp.sum(jnp.where(sel1, before, 0.0), axis=1, keepdims=True)
    r2 = jnp.sum(jnp.where(sel2, before, 0.0), axis=1, keepdims=True)
    cnt_ref[...] += jnp.sum(chosen, axis=0, keepdims=True)
    lane = lax.broadcasted_iota(jnp.int32, logits.shape, 1)
    record = jnp.zeros_like(logits)
    for c, val in enumerate((i1, i2, p1, p2, r1, r2)):
        record = jnp.where(lane == c, val, record)
    route_ref[...] = record[:, :ROUTE_COLS]


def _moe_scatter_kernel(pos_ref, pad_ref, tail_ref, x_ref, xs_ref, zero_ref, sem, zsem):
    tm = x_ref.shape[0]
    tr = zero_ref.shape[0]

    @pl.when(pl.program_id(0) == 0)
    def _():
        zero_ref[...] = jnp.zeros_like(zero_ref)

        def pad_copy(first, r):
            return pltpu.make_async_copy(zero_ref.at[pl.ds(0, 1)], xs_ref.at[pl.ds(first + r, 1)], zsem)

        def tile_copy(t):
            rows = pl.ds(pl.multiple_of(t * tr, tr), tr)
            return pltpu.make_async_copy(zero_ref, xs_ref.at[rows], zsem)

        def each(copy_fn, method):
            for e in range(pad_ref.shape[0]):
                first, count = pad_ref[e, 0], pad_ref[e, 1]
                lax.fori_loop(0, count, lambda r, c: (getattr(copy_fn(first, r), method)(), c)[1], 0)

        each(pad_copy, "start")
        lax.fori_loop(tail_ref[0], tail_ref[1], lambda t, c: (tile_copy(t).start(), c)[1], 0)
        each(pad_copy, "wait")
        lax.fori_loop(tail_ref[0], tail_ref[1], lambda t, c: (tile_copy(t).wait(), c)[1], 0)

    def row_copy(r, k):
        slot = pos_ref[0, 0, k * tm + r]
        return pltpu.make_async_copy(x_ref.at[pl.ds(r, 1)], xs_ref.at[pl.ds(slot, 1)], sem)

    def start(r, carry):
        row_copy(r, 0).start()
        row_copy(r, 1).start()
        return carry

    def wait(r, carry):
        row_copy(r, 0).wait()
        row_copy(r, 1).wait()
        return carry

    lax.fori_loop(0, tm, start, 0, unroll=ROW_DMA_UNROLL)
    lax.fori_loop(0, tm, wait, 0, unroll=ROW_DMA_UNROLL)


def _gmm_kernel(te_ref, nu_ref, xs_ref, wg_ref, wu_ref, wd_ref, ys_ref, xb_ref, acc_ref):
    del te_ref
    j = pl.program_id(0)
    f = pl.program_id(1)

    @pl.when(jnp.logical_and(j >= nu_ref[0], f == 0))
    def _():
        ys_ref[...] = jnp.zeros_like(ys_ref)

    @pl.when(j < nu_ref[0])
    def _():
        @pl.when(f == 0)
        def _():
            xb_ref[...] = xs_ref[...].astype(BF16)
            acc_ref[...] = jnp.zeros_like(acc_ref)

        acc_ref[...] += _swiglu_partial(xb_ref[...], wg_ref[...], wu_ref[...], wd_ref[...])

        @pl.when(f == pl.num_programs(1) - 1)
        def _():
            ys_ref[...] = acc_ref[...]


def _moe_combine_kernel(pos_ref, route_ref, x_ref, lg_ref, lb_ref, ys_ref, o_ref, y1_ref, y2_ref, sem,
                        *, alpha):
    tm = x_ref.shape[0]

    def row_copy(r, k):
        slot = pos_ref[0, 0, k * tm + r]
        dst = (y1_ref, y2_ref)[k]
        return pltpu.make_async_copy(ys_ref.at[pl.ds(slot, 1)], dst.at[pl.ds(r, 1)], sem)

    def start(r, carry):
        row_copy(r, 0).start()
        row_copy(r, 1).start()
        return carry

    def wait(r, carry):
        row_copy(r, 0).wait()
        row_copy(r, 1).wait()
        return carry

    lax.fori_loop(0, tm, start, 0, unroll=ROW_DMA_UNROLL)
    lax.fori_loop(0, tm, wait, 0, unroll=ROW_DMA_UNROLL)
    route = route_ref[...]
    mixed = route[:, 2:3] * y1_ref[...] + route[:, 3:4] * y2_ref[...]
    o_ref[...] = _layer_norm(alpha * x_ref[...] + mixed, lg_ref[...], lb_ref[...])


def _moe(x, router, wg, wu, wd, ln_g, ln_b, alpha):
    t, d = x.shape
    n_experts, _, ff = wg.shape
    tm = _tile(t, MOE_TOKEN_TILE)
    tr = MOE_TILE_ROWS
    n_tok_tiles = t // tm

    router_p = jnp.pad(router.astype(F32), ((0, 0), (0, V7X_LANES - n_experts)))
    route, counts = pl.pallas_call(
        functools.partial(_router_kernel, n_experts=n_experts),
        grid=(n_tok_tiles,),
        in_specs=[pl.BlockSpec((tm, d), lambda i: (i, 0)),
                  pl.BlockSpec((d, V7X_LANES), lambda i: (0, 0))],
        out_specs=[pl.BlockSpec((tm, ROUTE_COLS), lambda i: (i, 0)),
                   pl.BlockSpec((1, V7X_LANES), lambda i: (0, 0))],
        out_shape=[jax.ShapeDtypeStruct((t, ROUTE_COLS), F32), jax.ShapeDtypeStruct((1, V7X_LANES), F32)],
        scratch_shapes=[pltpu.VMEM((tm, tm), BF16)],
        compiler_params=_cparams("arbitrary"),
        name="moe_router",
    )(x, router_p)

    cnt = counts[0, :n_experts].astype(jnp.int32)
    padded = (cnt + tr - 1) // tr * tr
    ends = jnp.cumsum(padded)
    starts = ends - padded
    expert_ids = jnp.arange(n_experts, dtype=jnp.int32)

    def slots(expert_col, rank_col):
        onehot = route[:, expert_col:expert_col + 1].astype(jnp.int32) == expert_ids[None]
        return jnp.sum(jnp.where(onehot, starts[None], 0), axis=1) + route[:, rank_col].astype(jnp.int32)

    pos = jnp.concatenate([slots(0, 4).reshape(n_tok_tiles, 1, tm), slots(1, 5).reshape(n_tok_tiles, 1, tm)],
                          axis=2)
    n_tiles = (TOP_K * t) // tr + n_experts
    n_used = (ends[-1] // tr).reshape(1)
    tile_start = jnp.arange(n_tiles, dtype=jnp.int32) * tr
    tile_expert = jnp.sum((tile_start[:, None] >= ends[None]).astype(jnp.int32), axis=1)
    last_expert = jnp.sum((ends[-1] - tr >= ends).astype(jnp.int32))
    tile_expert = jnp.where(tile_start < ends[-1], tile_expert, last_expert)

    pos_spec = pl.BlockSpec((1, 1, 2 * tm), lambda i: (i, 0, 0), memory_space=pltpu.SMEM)
    smem_spec = pl.BlockSpec(memory_space=pltpu.SMEM)
    any_spec = pl.BlockSpec(memory_space=pl.ANY)
    pad_rows = jnp.stack([starts + cnt, padded - cnt], axis=1)
    tail_tiles = jnp.concatenate([n_used, jnp.full((1,), n_tiles, jnp.int32)])
    xs = pl.pallas_call(
        _moe_scatter_kernel,
        grid=(n_tok_tiles,),
        in_specs=[pos_spec, smem_spec, smem_spec, pl.BlockSpec((tm, d), lambda i: (i, 0))],
        out_specs=any_spec,
        out_shape=jax.ShapeDtypeStruct((n_tiles * tr, d), F32),
        scratch_shapes=[pltpu.VMEM((tr, d), F32), pltpu.SemaphoreType.DMA(()), pltpu.SemaphoreType.DMA(())],
        compiler_params=_cparams("arbitrary"),
        name="moe_scatter",
    )(pos, pad_rows, tail_tiles, x)

    tf = ff // 2 if (ff // 2) % V7X_LANES == 0 else ff
    nf = ff // tf
    used = lambda j, nu: jnp.minimum(j, nu[0] - 1)
    fcol = lambda j, f, nu: jnp.where(j < nu[0], f, nf - 1)
    ys = pl.pallas_call(
        _gmm_kernel,
        grid_spec=pltpu.PrefetchScalarGridSpec(
            num_scalar_prefetch=2,
            grid=(n_tiles, nf),
            in_specs=[pl.BlockSpec((tr, d), lambda j, f, te, nu: (used(j, nu), 0)),
                      pl.BlockSpec((None, d, tf), lambda j, f, te, nu: (te[j], 0, fcol(j, f, nu))),
                      pl.BlockSpec((None, d, tf), lambda j, f, te, nu: (te[j], 0, fcol(j, f, nu))),
                      pl.BlockSpec((None, tf, d), lambda j, f, te, nu: (te[j], fcol(j, f, nu), 0))],
            out_specs=pl.BlockSpec((tr, d), lambda j, f, te, nu: (j, 0)),
            scratch_shapes=[pltpu.VMEM((tr, d), BF16), pltpu.VMEM((tr, d), F32)]),
        out_shape=jax.ShapeDtypeStruct((n_tiles * tr, d), F32),
        compiler_params=_cparams("arbitrary", "arbitrary"),
        name="moe_experts",
    )(tile_expert, n_used, xs, wg, wu, wd)

    vec = pl.BlockSpec((1, d), lambda i: (0, 0))
    return pl.pallas_call(
        functools.partial(_moe_combine_kernel, alpha=alpha),
        grid=(n_tok_tiles,),
        in_specs=[pos_spec, pl.BlockSpec((tm, ROUTE_COLS), lambda i: (i, 0)),
                  pl.BlockSpec((tm, d), lambda i: (i, 0)), vec, vec, any_spec],
        out_specs=pl.BlockSpec((tm, d), lambda i: (i, 0)),
        out_shape=jax.ShapeDtypeStruct((t, d), F32),
        scratch_shapes=[pltpu.VMEM((tm, d), F32), pltpu.VMEM((tm, d), F32), pltpu.SemaphoreType.DMA(())],
        compiler_params=_cparams("arbitrary"),
        name="moe_combine",
    )(pos, route, x, ln_g.reshape(1, d), ln_b.reshape(1, d), ys)


def _run_trunk(x3, p, tables):
    bsz, seq, d = x3.shape
    depth = p['w_in'].shape[0]
    alpha = (2 * depth) ** 0.25
    att_w = ATT_HEADS * HEAD_W
    width = p['sc_conv_w'].shape[2]
    gate_w = 3 * d // 2
    assert att_w % width == 0 and (3 * att_w + 6 * width) % gate_w == 0, "column blocks must align"
    hy_col = att_w // width
    d_in = p['w_in'].shape[2]
    col_scale = jnp.asarray(np.where(np.arange(d_in) < att_w, LOG2E * ATT_HEAD_DIM ** -0.5, 1.0)
                            .astype(np.float32)[None])
    x = x3.reshape(bsz * seq, d)
    for i in range(depth):
        proj = _proj(x, p['w_in'][i], col_scale)
        proj3 = proj.reshape(bsz, seq, proj.shape[1])
        y_a = _attention(proj3, p['att_lambda'][i], p['att_subln_g'][i], i)
        u, x0, y_c = _convs(proj3, p['hy_short_w'][i], p['sc_conv_w'][i], hy_col=hy_col, sc_col=hy_col + 1)
        spec = _hyena_filter_spectrum(seq, tables, p['hf_w_in'][i], p['hf_b_in'][i], p['hf_w_mid'][i],
                                      p['hf_b_mid'][i], p['hf_freq'][i], p['hf_w_out'][i])
        y_h = _long_conv(u, x0, tables, spec, p['hy_bias'][i])
        flat = lambda a: a.reshape(bsz * seq, a.shape[2])
        j = i // 2
        x = _merge(flat(y_a), flat(y_h), flat(y_c), proj, x, p['w_branch'][i], p['w_out'][i],
                   p['ln1_g'][i], p['ln1_b'][i], alpha, gate_col=(3 * att_w + 6 * width) // gate_w)
        if i % 2 == 0:
            x = _ffn(x, p['ffn_w_gate'][j], p['ffn_w_up'][j], p['ffn_w_down'][j],
                     p['ln2_g'][i], p['ln2_b'][i], alpha)
        else:
            x = _moe(x, p['router_w'][j], p['moe_w_gate'][j], p['moe_w_up'][j], p['moe_w_down'][j],
                     p['ln2_g'][i], p['ln2_b'][i], alpha)
    return x.reshape(bsz, seq, d)


def kernel(x_prompt, x_sample, w_in, att_lambda, att_subln_g, hy_short_w, hf_w_in, hf_b_in,
           hf_w_mid, hf_b_mid, hf_freq, hf_w_out, hy_bias, sc_conv_w, w_branch, w_out,
           ln1_g, ln1_b, ln2_g, ln2_b, ffn_w_gate, ffn_w_up, ffn_w_down, router_w,
           moe_w_gate, moe_w_up, moe_w_down):
    bf = lambda a: a.astype(BF16)
    p = dict(w_in=bf(w_in), att_lambda=att_lambda, att_subln_g=att_subln_g, hy_short_w=hy_short_w,
             hf_w_in=hf_w_in, hf_b_in=hf_b_in, hf_w_mid=hf_w_mid, hf_b_mid=hf_b_mid,
             hf_freq=hf_freq, hf_w_out=hf_w_out, hy_bias=hy_bias, sc_conv_w=sc_conv_w,
             w_branch=bf(w_branch), w_out=bf(w_out), ln1_g=ln1_g, ln1_b=ln1_b, ln2_g=ln2_g, ln2_b=ln2_b,
             ffn_w_gate=bf(ffn_w_gate), ffn_w_up=bf(ffn_w_up), ffn_w_down=bf(ffn_w_down),
             router_w=router_w, moe_w_gate=bf(moe_w_gate), moe_w_up=bf(moe_w_up),
             moe_w_down=bf(moe_w_down))
    tables = {}
    outs = []
    for x3 in (x_prompt, x_sample):
        seq = x3.shape[1]
        if seq not in tables:
            tables[seq] = _dft_tables(seq)
        outs.append(_run_trunk(x3, p, tables[seq]))
    return tuple(outs)
```

```python
import functools
import math

import numpy as np
import jax
import jax.numpy as jnp
from jax import lax
from jax.experimental import pallas as pl
from jax.experimental.pallas import tpu as pltpu

F32 = jnp.float32
BF16 = jnp.bfloat16

ATT_HEADS = 4
ATT_HEAD_DIM = 64
HEAD_W = 2 * ATT_HEAD_DIM
HY_BANDS = 16
HY_TARGET = 1e-2
HY_FAST_PCT = 0.3
HY_SLOW_PCT = 1.5
TOP_K = 2
LN_EPS = 1e-5
RMS_EPS = 1e-5
LOG2E = math.log2(math.e)
ATT_SCORE_TILE = 1 << 21
V7X_LANES = 128
V7X_BF16_SUBLANES = 16
V7X_VMEM_LIMIT = 56 * 1024 * 1024


def _cparams(*sem):
    return pltpu.CompilerParams(dimension_semantics=sem, vmem_limit_bytes=V7X_VMEM_LIMIT)


def _tile(n, pref):
    t = min(n, pref)
    while n % t:
        t //= 2
    return t


def _layer_norm(v, g, b):
    mu = jnp.mean(v, axis=-1, keepdims=True)
    d = v - mu
    var = jnp.mean(d * d, axis=-1, keepdims=True)
    return d * lax.rsqrt(var + LN_EPS) * g + b


def _proj_kernel(x_ref, w_ref, cs_ref, o_ref, xb_ref):
    @pl.when(pl.program_id(1) == 0)
    def _():
        xb_ref[...] = x_ref[...].astype(BF16)

    acc = jnp.dot(xb_ref[...], w_ref[...], preferred_element_type=F32)
    o_ref[...] = (acc * cs_ref[...]).astype(o_ref.dtype)


def _proj(x, w, col_scale):
    t, d = x.shape
    n = w.shape[1]
    tm = _tile(t, 1024)
    tn = _tile(n, 1536)
    return pl.pallas_call(
        _proj_kernel,
        grid=(t // tm, n // tn),
        in_specs=[pl.BlockSpec((tm, d), lambda i, j: (i, 0)),
                  pl.BlockSpec((d, tn), lambda i, j: (0, j)),
                  pl.BlockSpec((1, tn), lambda i, j: (0, j))],
        out_specs=pl.BlockSpec((tm, tn), lambda i, j: (i, j)),
        out_shape=jax.ShapeDtypeStruct((t, n), BF16),
        scratch_shapes=[pltpu.VMEM((tm, d), BF16)],
        compiler_params=_cparams("parallel", "arbitrary"),
        name="proj",
    )(x, w, col_scale)


def _attn_kernel(slopes_ref, lam_ref, g_ref, q_ref, k_ref, v_ref, o_ref, bias_ref, *, lam_init, tq):
    h = pl.program_id(0)
    b = pl.program_id(1)
    qi = pl.program_id(2)
    seq = k_ref.shape[1]

    @pl.when(jnp.logical_and(b == 0, qi == 0))
    def _():
        r = lax.broadcasted_iota(jnp.int32, bias_ref.shape, 0)
        j = lax.broadcasted_iota(jnp.int32, bias_ref.shape, 1)
        bias_ref[...] = (-LOG2E * slopes_ref[h]) * jnp.abs(r - j + (seq - tq)).astype(F32)

    start = pl.multiple_of((pl.num_programs(2) - 1 - qi) * tq, tq)
    lf = lam_ref[...].astype(F32)
    lam_full = (jnp.exp(jnp.sum(lf[0:1] * lf[1:2], axis=1, keepdims=True))
                - jnp.exp(jnp.sum(lf[2:3] * lf[3:4], axis=1, keepdims=True)) + lam_init)
    q = q_ref[0]
    k = k_ref[0]
    v = v_ref[0]
    bias = bias_ref[:, pl.ds(start, seq)]
    lane = lax.broadcasted_iota(jnp.int32, q.shape, 1)
    zero = jnp.zeros_like(q)

    def softmax_pv(qm):
        s = lax.dot_general(qm, k, (((1,), (1,)), ((), ())), preferred_element_type=F32)
        m = jnp.max(s + bias, axis=1, keepdims=True)
        e = jnp.exp2(s + (bias - m))
        l = jnp.sum(e, axis=1, keepdims=True)
        return jnp.dot(e.astype(BF16), v, preferred_element_type=F32) * (1.0 / l)

    o = (softmax_pv(jnp.where(lane < ATT_HEAD_DIM, q, zero))
         - lam_full * softmax_pv(jnp.where(lane >= ATT_HEAD_DIM, q, zero)))
    o = o * lax.rsqrt(jnp.mean(o * o, axis=-1, keepdims=True) + RMS_EPS) * g_ref[...].astype(F32)
    o_ref[0] = (o * (1.0 - lam_init)).astype(o_ref.dtype)


def _attention(proj3, lam, subln_g, layer):
    bsz, seq, _ = proj3.shape
    tq = _tile(seq, 512 if layer == 0 else ATT_SCORE_TILE // seq)
    lam_init = 0.8 - 0.6 * math.exp(-0.3 * layer)
    slopes = jnp.asarray(2.0 ** (-8.0 * np.arange(1, ATT_HEADS + 1, dtype=np.float32) / ATT_HEADS))
    kern = functools.partial(_attn_kernel, lam_init=lam_init, tq=tq)
    return pl.pallas_call(
        kern,
        grid=(ATT_HEADS, bsz, seq // tq),
        in_specs=[pl.BlockSpec(memory_space=pltpu.SMEM),
                  pl.BlockSpec((4, ATT_HEAD_DIM), lambda h, b, i: (0, 0)),
                  pl.BlockSpec((1, HEAD_W), lambda h, b, i: (0, 0)),
                  pl.BlockSpec((1, tq, HEAD_W), lambda h, b, i: (b, i, h)),
                  pl.BlockSpec((1, seq, HEAD_W), lambda h, b, i: (b, 0, ATT_HEADS + h)),
                  pl.BlockSpec((1, seq, HEAD_W), lambda h, b, i: (b, 0, 2 * ATT_HEADS + h))],
        out_specs=pl.BlockSpec((1, tq, HEAD_W), lambda h, b, i: (b, i, h)),
        out_shape=jax.ShapeDtypeStruct((bsz, seq, ATT_HEADS * HEAD_W), BF16),
        scratch_shapes=[pltpu.VMEM((tq, 2 * seq - tq), F32)],
        compiler_params=_cparams("arbitrary", "arbitrary", "arbitrary"),
        name="attn",
    )(slopes, lam, subln_g.reshape(1, HEAD_W), proj3, proj3, proj3)


def _conv3(x, prev_row, next_row, w):
    rows = x.shape[0]
    row = lax.broadcasted_iota(jnp.int32, (rows, 1), 0)
    x_prev = jnp.where(row == 0, prev_row, pltpu.roll(x, 1, axis=0))
    x_next = jnp.where(row == rows - 1, next_row, pltpu.roll(x, rows - 1, axis=0))
    return w[0:1] * x_prev + w[1:2] * x + w[2:3] * x_next


def _conv_kernel(hy_ref, hyp_ref, hyn_ref, sc_ref, scp_ref, scn_ref, hw_ref, sw_ref,
                 u_ref, x0_ref, yc_ref):
    i = pl.program_id(1)
    last = pl.num_programs(1) - 1
    width = u_ref.shape[2]
    has_prev = jnp.where(i > 0, 1.0, 0.0)
    has_next = jnp.where(i < last, 1.0, 0.0)
    hsub = hyp_ref.shape[1]

    hy = hy_ref[0].astype(F32)
    hy_prev = hyp_ref[0].astype(F32)[hsub - 1:hsub] * has_prev
    hy_next = hyn_ref[0].astype(F32)[0:1] * has_next
    hyc = _conv3(hy, hy_prev, hy_next, hw_ref[...].astype(F32))
    x0_ref[0] = hyc[:, :width].astype(x0_ref.dtype)
    u_ref[0] = (hyc[:, 2 * width:] * hyc[:, width:2 * width]).astype(u_ref.dtype)

    def gated(blk):
        return blk[:, width:2 * width] * blk[:, 2 * width:]

    sc = sc_ref[0].astype(F32)
    cx_prev = gated(scp_ref[0].astype(F32)[hsub - 1:hsub]) * has_prev
    cx_next = gated(scn_ref[0].astype(F32)[0:1]) * has_next
    conv = _conv3(gated(sc), cx_prev, cx_next, sw_ref[...].astype(F32))
    yc_ref[0] = (sc[:, :width] * conv).astype(yc_ref.dtype)


def _convs(proj3, hy_w, sc_w, hy_col, sc_col):
    bsz, seq, _ = proj3.shape
    width = sc_w.shape[1]
    tl = _tile(seq, 512)
    hs = V7X_BF16_SUBLANES
    nh = tl // hs
    nblk = seq // hs

    def main(col):
        return pl.BlockSpec((1, tl, 3 * width), lambda b, i: (b, i, col))

    def prev(col):
        return pl.BlockSpec((1, hs, 3 * width), lambda b, i: (b, jnp.maximum(i * nh - 1, 0), col))

    def nxt(col):
        return pl.BlockSpec((1, hs, 3 * width), lambda b, i: (b, jnp.minimum((i + 1) * nh, nblk - 1), col))

    out = jax.ShapeDtypeStruct((bsz, seq, width), BF16)
    ospec = pl.BlockSpec((1, tl, width), lambda b, i: (b, i, 0))
    return pl.pallas_call(
        _conv_kernel,
        grid=(bsz, seq // tl),
        in_specs=[main(hy_col), prev(hy_col), nxt(hy_col), main(sc_col), prev(sc_col), nxt(sc_col),
                  pl.BlockSpec((3, 3 * width), lambda b, i: (0, 0)),
                  pl.BlockSpec((3, width), lambda b, i: (0, 0))],
        out_specs=[ospec, ospec, ospec],
        out_shape=[out, out, out],
        compiler_params=_cparams("parallel", "arbitrary"),
        name="conv",
    )(proj3, proj3, proj3, proj3, proj3, proj3, hy_w, sc_w)


def _tab_kernel(c_ref, s_ref, st_ref, cb_ref, sb_ref, *, seq):
    tr = c_ref.shape[0]
    i = pl.program_id(0)
    dr = lax.broadcasted_iota(jnp.int32, (tr, seq), 0)
    c = lax.broadcasted_iota(jnp.int32, (tr, seq), 1)
    r = i * tr + dr

    def angle(prod):
        return (prod & (2 * seq - 1)).astype(F32) * (math.pi / seq)

    @pl.when(i == 0)
    def _():
        base = angle(dr * c)
        cb_ref[...] = jnp.cos(base)
        sb_ref[...] = jnp.sin(base)

    col = lax.broadcasted_iota(jnp.int32, (1, seq), 1)
    lead = angle((i * tr) * col)
    c0, s0 = jnp.cos(lead), jnp.sin(lead)
    cb, sb = cb_ref[...], sb_ref[...]
    cosv = c0 * cb - s0 * sb
    nsin = -(s0 * cb + c0 * sb)
    c_ref[...] = cosv.astype(c_ref.dtype)
    s_ref[...] = jnp.where(r == 0, jnp.where((c & 1) == 0, 1.0, -1.0), nsin).astype(s_ref.dtype)
    st_ref[...] = jnp.where(c == 0, jnp.where((r & 1) == 0, 1.0, -1.0), nsin).astype(st_ref.dtype)


def _dft_tables(seq):
    assert seq & (seq - 1) == 0, "sequence length must be a power of two"
    tr = _tile(seq, 256)
    spec = pl.BlockSpec((tr, seq), lambda i: (i, 0))
    out = jax.ShapeDtypeStruct((seq, seq), BF16)
    return pl.pallas_call(
        functools.partial(_tab_kernel, seq=seq),
        grid=(seq // tr,),
        out_specs=[spec, spec, spec],
        out_shape=[out, out, out],
        scratch_shapes=[pltpu.VMEM((tr, seq), F32), pltpu.VMEM((tr, seq), F32)],
        compiler_params=_cparams("arbitrary"),
        name="dft_tables",
    )()


def _filt_kernel(z_ref, win_ref, bin_ref, wmid_ref, bmid_ref, fr_ref, wout_ref, dl_ref,
                 hs_ref, l1_ref, *, seq):
    i = pl.program_id(0)
    tl = z_ref.shape[0]
    width = dl_ref.shape[1]
    hi = lax.Precision.HIGHEST
    fr = fr_ref[...]
    h = jnp.sin(fr * (jnp.dot(z_ref[...], win_ref[...], precision=hi, preferred_element_type=F32)
                      + bin_ref[...]))
    for j in range(wmid_ref.shape[0]):
        h = jnp.sin(fr * (jnp.dot(h, wmid_ref[j], precision=hi, preferred_element_type=F32)
                          + bmid_ref[j]))
    ho = jnp.dot(h, wout_ref[...], precision=hi, preferred_element_type=F32)
    pos = i * tl + lax.broadcasted_iota(jnp.int32, (tl, 1), 0)
    t = pos.astype(F32) / float(seq - 1)
    window = jnp.exp(-t * dl_ref[...])
    h_fwd = ho[:, :width] * window
    h_bwd = jnp.where(pos == 0, 0.0, ho[:, width:] * window)
    hs_ref[:, :width] = h_fwd + h_bwd
    hs_ref[:, width:] = h_fwd - h_bwd

    @pl.when(i == 0)
    def _():
        l1_ref[...] = jnp.zeros_like(l1_ref)

    l1_ref[...] += jnp.sum(jnp.abs(h_fwd) + jnp.abs(h_bwd), axis=0, keepdims=True)


def _fspec_kernel(c_ref, s_ref, hs_ref, l1_ref, a_ref, b_ref, d_ref, *, seq):
    i = pl.program_id(0)
    tk = c_ref.shape[0]
    width = l1_ref.shape[1]
    hs = hs_ref[...].astype(BF16)
    xc = jnp.dot(c_ref[...], hs[:, :width], preferred_element_type=F32)
    xs = jnp.dot(s_ref[...], hs, preferred_element_type=F32)
    k = i * tk + lax.broadcasted_iota(jnp.int32, (tk, 1), 0)
    inv_l1 = 1.0 / l1_ref[...]
    wk = jnp.where(k == 0, 1.0, 2.0) / float(2 * seq)
    a = xc * wk * inv_l1
    a_ref[...] = a
    b_ref[...] = jnp.where(k == 0, 0.0, xs[:, width:] * wk * inv_l1)
    d_ref[...] = jnp.where(k == 0, xs[:, :width] * wk * inv_l1, a)


def _hyena_filter_spectrum(seq, tabs, w_in, b_in, w_mid, b_mid, freq, w_out):
    c_tab, s_tab, _ = tabs
    emb, fw = w_in.shape
    width = w_out.shape[1] // 2
    n = np.arange(seq, dtype=np.float64)[:, None]
    f = np.linspace(1e-4, HY_BANDS - 1, HY_BANDS)[None]
    w = 2.0 * math.pi * n / seq
    z = np.concatenate([n / (seq - 1), np.cos(f * w), -np.sin(f * w)], -1)
    z = np.pad(z, ((0, 0), (0, V7X_LANES - emb))).astype(np.float32)
    w_in_p = jnp.pad(w_in.astype(F32), ((0, V7X_LANES - emb), (0, 0)))
    deltas = np.abs(np.linspace(math.log(HY_TARGET) / HY_SLOW_PCT, math.log(HY_TARGET) / HY_FAST_PCT,
                                width)).astype(np.float32)[None]
    tl = _tile(seq, 512)
    full = lambda *shape: pl.BlockSpec(shape, lambda i: (0,) * len(shape))
    hs, l1 = pl.pallas_call(
        functools.partial(_filt_kernel, seq=seq),
        grid=(seq // tl,),
        in_specs=[pl.BlockSpec((tl, V7X_LANES), lambda i: (i, 0)),
                  full(V7X_LANES, fw), full(1, fw), full(*w_mid.shape), full(w_mid.shape[0], 1, fw),
                  full(1, fw), full(fw, 2 * width), full(1, width)],
        out_specs=[pl.BlockSpec((tl, 2 * width), lambda i: (i, 0)), full(1, width)],
        out_shape=[jax.ShapeDtypeStruct((seq, 2 * width), F32), jax.ShapeDtypeStruct((1, width), F32)],
        compiler_params=_cparams("arbitrary"),
        name="hyena_filter",
    )(jnp.asarray(z), w_in_p, b_in.reshape(1, fw).astype(F32), w_mid.astype(F32),
      b_mid.reshape(w_mid.shape[0], 1, fw).astype(F32), freq.reshape(1, fw).astype(F32),
      w_out.astype(F32), jnp.asarray(deltas))

    tk = _tile(seq, 256)
    ospec = pl.BlockSpec((tk, width), lambda i: (i, 0))
    out = jax.ShapeDtypeStruct((seq, width), F32)
    return pl.pallas_call(
        functools.partial(_fspec_kernel, seq=seq),
        grid=(seq // tk,),
        in_specs=[pl.BlockSpec((tk, seq), lambda i: (i, 0)), pl.BlockSpec((tk, seq), lambda i: (i, 0)),
                  full(seq, 2 * width), full(1, width)],
        out_specs=[ospec, ospec, ospec],
        out_shape=[out, out, out],
        compiler_params=_cparams("parallel"),
        name="filter_spectrum",
    )(c_tab, s_tab, hs, l1)


def _dft_fwd_kernel(c_ref, s_ref, u_ref, a_ref, b_ref, d_ref, yre_ref, yim_ref):
    u = u_ref[0]
    x_re = jnp.dot(c_ref[...], u, preferred_element_type=F32)
    x_im = jnp.dot(s_ref[...], u, preferred_element_type=F32)
    b = b_ref[...]
    yre_ref[0] = (x_re * a_ref[...] - x_im * b).astype(yre_ref.dtype)
    yim_ref[0] = (x_re * b + x_im * d_ref[...]).astype(yim_ref.dtype)


def _dft_inv_kernel(c_ref, st_ref, yre_ref, yim_ref, u_ref, x0_ref, bias_ref, o_ref):
    y = (jnp.dot(c_ref[...], yre_ref[0], preferred_element_type=F32)
         + jnp.dot(st_ref[...], yim_ref[0], preferred_element_type=F32))
    y = y + u_ref[0].astype(F32) * bias_ref[...]
    o_ref[0] = (x0_ref[0].astype(F32) * y).astype(o_ref.dtype)


def _long_conv(u, x0, tabs, spec, bias):
    c_tab, s_tab, st_tab = tabs
    a, b, d = spec
    bsz, seq, width = u.shape
    tk = _tile(seq, 512)
    tab = pl.BlockSpec((tk, seq), lambda j, bb: (j, 0))
    whole = pl.BlockSpec((1, seq, width), lambda j, bb: (bb, 0, 0))
    rows = pl.BlockSpec((1, tk, width), lambda j, bb: (bb, j, 0))
    coef = pl.BlockSpec((tk, width), lambda j, bb: (j, 0))
    spec_out = jax.ShapeDtypeStruct((bsz, seq, width), BF16)
    y_re, y_im = pl.pallas_call(
        _dft_fwd_kernel,
        grid=(seq // tk, bsz),
        in_specs=[tab, tab, whole, coef, coef, coef],
        out_specs=[rows, rows],
        out_shape=[spec_out, spec_out],
        compiler_params=_cparams("parallel", "arbitrary"),
        name="dft_fwd",
    )(c_tab, s_tab, u, a, b, d)
    return pl.pallas_call(
        _dft_inv_kernel,
        grid=(seq // tk, bsz),
        in_specs=[tab, tab, whole, whole, rows, rows, pl.BlockSpec((1, width), lambda j, bb: (0, 0))],
        out_specs=rows,
        out_shape=spec_out,
        compiler_params=_cparams("parallel", "arbitrary"),
        name="dft_inv",
    )(c_tab, st_tab, y_re, y_im, u, x0, bias.reshape(1, width).astype(F32))


def _merge_kernel(ya_ref, yh_ref, yc_ref, g0_ref, g1_ref, x_ref, wb_ref, wo_ref, lg_ref, lb_ref,
                  o_ref, m_ref, *, alpha):
    d = o_ref.shape[1]
    half = d // 2
    g0 = g0_ref[...]
    g1 = g1_ref[...]
    gates = ((g0[:, :half], g0[:, d:], g1[:, half:d]),
             (g0[:, half:d], g1[:, :half], g1[:, d:]))
    branches = (ya_ref[...], yh_ref[...], yc_ref[...])
    for c in range(2):
        acc = None
        for j in range(3):
            y = jnp.dot(branches[j], wb_ref[j, :, c * half:(c + 1) * half], preferred_element_type=F32)
            term = jax.nn.sigmoid(gates[c][j].astype(F32)) * y
            acc = term if acc is None else acc + term
        m_ref[:, c * half:(c + 1) * half] = acc.astype(m_ref.dtype)
    t = jnp.dot(m_ref[...], wo_ref[...], preferred_element_type=F32)
    o_ref[...] = _layer_norm(alpha * x_ref[...] + t, lg_ref[...], lb_ref[...])


def _merge(y_a, y_h, y_c, proj, x, wb, wo, ln_g, ln_b, alpha, gate_col):
    t, d = x.shape
    bw = y_a.shape[1]
    tm = _tile(t, 512)
    row = lambda w: pl.BlockSpec((tm, w), lambda i: (i, 0))
    fixed = lambda *shape: pl.BlockSpec(shape, lambda i: (0,) * len(shape))
    gw = 3 * d // 2
    return pl.pallas_call(
        functools.partial(_merge_kernel, alpha=alpha),
        grid=(t // tm,),
        in_specs=[row(bw), row(bw), row(bw),
                  pl.BlockSpec((tm, gw), lambda i: (i, gate_col)),
                  pl.BlockSpec((tm, gw), lambda i: (i, gate_col + 1)),
                  row(d), fixed(*wb.shape), fixed(*wo.shape), fixed(1, d), fixed(1, d)],
        out_specs=row(d),
        out_shape=jax.ShapeDtypeStruct((t, d), F32),
        scratch_shapes=[pltpu.VMEM((tm, d), BF16)],
        compiler_params=_cparams("parallel"),
        name="merge",
    )(y_a, y_h, y_c, proj, proj, x, wb, wo, ln_g.reshape(1, d), ln_b.reshape(1, d))


def _swiglu_partial(xb, wg, wu, wd):
    g = jnp.dot(xb, wg, preferred_element_type=F32)
    u = jnp.dot(xb, wu, preferred_element_type=F32)
    return jnp.dot((jax.nn.silu(g) * u).astype(BF16), wd, preferred_element_type=F32)


def _ffn_kernel(x_ref, wg_ref, wu_ref, wd_ref, lg_ref, lb_ref, o_ref, xb_ref, acc_ref, *, alpha):
    f = pl.program_id(1)

    @pl.when(f == 0)
    def _():
        xb_ref[...] = x_ref[...].astype(BF16)
        acc_ref[...] = jnp.zeros_like(acc_ref)

    acc_ref[...] += _swiglu_partial(xb_ref[...], wg_ref[...], wu_ref[...], wd_ref[...])

    @pl.when(f == pl.num_programs(1) - 1)
    def _():
        o_ref[...] = _layer_norm(alpha * x_ref[...] + acc_ref[...], lg_ref[...], lb_ref[...])


def _ffn(x, wg, wu, wd, ln_g, ln_b, alpha):
    t, d = x.shape
    ff = wg.shape[1]
    tm = _tile(t, 1024)
    tf = ff // 2 if (ff // 2) % V7X_LANES == 0 else ff
    vec = pl.BlockSpec((1, d), lambda i, f: (0, 0))
    return pl.pallas_call(
        functools.partial(_ffn_kernel, alpha=alpha),
        grid=(t // tm, ff // tf),
        in_specs=[pl.BlockSpec((tm, d), lambda i, f: (i, 0)),
                  pl.BlockSpec((d, tf), lambda i, f: (0, f)),
                  pl.BlockSpec((d, tf), lambda i, f: (0, f)),
                  pl.BlockSpec((tf, d), lambda i, f: (f, 0)),
                  vec, vec],
        out_specs=pl.BlockSpec((tm, d), lambda i, f: (i, 0)),
        out_shape=jax.ShapeDtypeStruct((t, d), F32),
        scratch_shapes=[pltpu.VMEM((tm, d), BF16), pltpu.VMEM((tm, d), F32)],
        compiler_params=_cparams("parallel", "arbitrary"),
        name="ffn",
    )(x, wg, wu, wd, ln_g.reshape(1, d), ln_b.reshape(1, d))


def _top2(logits, n_experts):
    lane = lax.broadcasted_iota(jnp.int32, logits.shape, 1).astype(F32)
    neg = jnp.float32(-jnp.inf)
    sentinel = jnp.float32(n_experts)
    lg = jnp.where(lane < sentinel, logits, neg)
    m1 = jnp.max(lg, axis=1, keepdims=True)
    i1 = jnp.min(jnp.where(lg == m1, lane, sentinel), axis=1, keepdims=True)
    rest = jnp.where(lane == i1, neg, lg)
    m2 = jnp.max(rest, axis=1, keepdims=True)
    i2 = jnp.min(jnp.where(rest == m2, lane, sentinel), axis=1, keepdims=True)
    e2 = jnp.exp(m2 - m1)
    p1 = 1.0 / (1.0 + e2)
    p2 = e2 / (1.0 + e2)
    return lane == i1, lane == i2, i1, i2, p1, p2


ROUTE_COLS = 8
MOE_TILE_ROWS = 512
MOE_TOKEN_TILE = 512
ROW_DMA_UNROLL = 8


def _router_kernel(x_ref, r_ref, route_ref, cnt_ref, tri_ref, *, n_experts):
    i = pl.program_id(0)
    x = x_ref[...]
    tm = x.shape[0]

    @pl.when(i == 0)
    def _():
        cnt_ref[...] = jnp.zeros_like(cnt_ref)
        row = lax.broadcasted_iota(jnp.int32, (tm, tm), 0)
        col = lax.broadcasted_iota(jnp.int32, (tm, tm), 1)
        tri_ref[...] = jnp.where(col < row, 1.0, 0.0).astype(tri_ref.dtype)

    w = r_ref[...]
    x_hi, w_hi = x.astype(BF16), w.astype(BF16)
    x_lo, w_lo = (x - x_hi.astype(F32)).astype(BF16), (w - w_hi.astype(F32)).astype(BF16)
    logits = (jnp.dot(x_hi, w_hi, preferred_element_type=F32) + jnp.dot(x_lo, w_hi, preferred_element_type=F32)
              + jnp.dot(x_hi, w_lo, preferred_element_type=F32))
    sel1, sel2, i1, i2, p1, p2 = _top2(logits, n_experts)
    chosen = jnp.where(sel1, 1.0, jnp.where(sel2, 1.0, 0.0))
    before = jnp.dot(tri_ref[...], chosen.astype(tri_ref.dtype), preferred_element_type=F32) + cnt_ref[...]
    r1 = jnp.sum(jnp.where(sel1, before, 0.0), axis=1, keepdims=True)
    r2 = jnp.sum(jnp.where(sel2, before, 0.0), axis=1, keepdims=True)
    cnt_ref[...] += jnp.sum(chosen, axis=0, keepdims=True)
    lane = lax.broadcasted_iota(jnp.int32, logits.shape, 1)
    record = jnp.zeros_like(logits)
    for c, val in enumerate((i1, i2, p1, p2, r1, r2)):
        record = jnp.where(lane == c, val, record)
    route_ref[...] = record[:, :ROUTE_COLS]


def _moe_scatter_kernel(pos_ref, pad_ref, tail_ref, x_ref, xs_ref, zero_ref, sem, zsem):
    tm = x_ref.shape[0]
    tr = zero_ref.shape[0]

    @pl.when(pl.program_id(0) == 0)
    def _():
        zero_ref[...] = jnp.zeros_like(zero_ref)

        def pad_copy(first, r):
            return pltpu.make_async_copy(zero_ref.at[pl.ds(0, 1)], xs_ref.at[pl.ds(first + r, 1)], zsem)

        def tile_copy(t):
            rows = pl.ds(pl.multiple_of(t * tr, tr), tr)
            return pltpu.make_async_copy(zero_ref, xs_ref.at[rows], zsem)

        def each(copy_fn, method):
            for e in range(pad_ref.shape[0]):
                first, count = pad_ref[e, 0], pad_ref[e, 1]
                lax.fori_loop(0, count, lambda r, c: (getattr(copy_fn(first, r), method)(), c)[1], 0)

        each(pad_copy, "start")
        lax.fori_loop(tail_ref[0], tail_ref[1], lambda t, c: (tile_copy(t).start(), c)[1], 0)
        each(pad_copy, "wait")
        lax.fori_loop(tail_ref[0], tail_ref[1], lambda t, c: (tile_copy(t).wait(), c)[1], 0)

    def row_copy(r, k):
        slot = pos_ref[0, 0, k * tm + r]
        return pltpu.make_async_copy(x_ref.at[pl.ds(r, 1)], xs_ref.at[pl.ds(slot, 1)], sem)

    def start(r, carry):
        row_copy(r, 0).start()
        row_copy(r, 1).start()
        return carry

    def wait(r, carry):
        row_copy(r, 0).wait()
        row_copy(r, 1).wait()
        return carry

    lax.fori_loop(0, tm, start, 0, unroll=ROW_DMA_UNROLL)
    lax.fori_loop(0, tm, wait, 0, unroll=ROW_DMA_UNROLL)


def _gmm_kernel(te_ref, nu_ref, xs_ref, wg_ref, wu_ref, wd_ref, ys_ref, xb_ref, acc_ref):
    del te_ref
    j = pl.program_id(0)
    f = pl.program_id(1)

    @pl.when(jnp.logical_and(j >= nu_ref[0], f == 0))
    def _():
        ys_ref[...] = jnp.zeros_like(ys_ref)

    @pl.when(j < nu_ref[0])
    def _():
        @pl.when(f == 0)
        def _():
            xb_ref[...] = xs_ref[...].astype(BF16)
            acc_ref[...] = jnp.zeros_like(acc_ref)

        acc_ref[...] += _swiglu_partial(xb_ref[...], wg_ref[...], wu_ref[...], wd_ref[...])

        @pl.when(f == pl.num_programs(1) - 1)
        def _():
            ys_ref[...] = acc_ref[...]


def _moe_combine_kernel(pos_ref, route_ref, x_ref, lg_ref, lb_ref, ys_ref, o_ref, y1_ref, y2_ref, sem,
                        *, alpha):
    tm = x_ref.shape[0]

    def row_copy(r, k):
        slot = pos_ref[0, 0, k * tm + r]
        dst = (y1_ref, y2_ref)[k]
        return pltpu.make_async_copy(ys_ref.at[pl.ds(slot, 1)], dst.at[pl.ds(r, 1)], sem)

    def start(r, carry):
        row_copy(r, 0).start()
        row_copy(r, 1).start()
        return carry

    def wait(r, carry):
        row_copy(r, 0).wait()
        row_copy(r, 1).wait()
        return carry

    lax.fori_loop(0, tm, start, 0, unroll=ROW_DMA_UNROLL)
    lax.fori_loop(0, tm, wait, 0, unroll=ROW_DMA_UNROLL)
    route = route_ref[...]
    mixed = route[:, 2:3] * y1_ref[...] + route[:, 3:4] * y2_ref[...]
    o_ref[...] = _layer_norm(alpha * x_ref[...] + mixed, lg_ref[...], lb_ref[...])


def _moe(x, router, wg, wu, wd, ln_g, ln_b, alpha):
    t, d = x.shape
    n_experts, _, ff = wg.shape
    tm = _tile(t, MOE_TOKEN_TILE)
    tr = MOE_TILE_ROWS
    n_tok_tiles = t // tm

    router_p = jnp.pad(router.astype(F32), ((0, 0), (0, V7X_LANES - n_experts)))
    route, counts = pl.pallas_call(
        functools.partial(_router_kernel, n_experts=n_experts),
        grid=(n_tok_tiles,),
        in_specs=[pl.BlockSpec((tm, d), lambda i: (i, 0)),
                  pl.BlockSpec((d, V7X_LANES), lambda i: (0, 0))],
        out_specs=[pl.BlockSpec((tm, ROUTE_COLS), lambda i: (i, 0)),
                   pl.BlockSpec((1, V7X_LANES), lambda i: (0, 0))],
        out_shape=[jax.ShapeDtypeStruct((t, ROUTE_COLS), F32), jax.ShapeDtypeStruct((1, V7X_LANES), F32)],
        scratch_shapes=[pltpu.VMEM((tm, tm), BF16)],
        compiler_params=_cparams("arbitrary"),
        name="moe_router",
    )(x, router_p)

    cnt = counts[0, :n_experts].astype(jnp.int32)
    padded = (cnt + tr - 1) // tr * tr
    ends = jnp.cumsum(padded)
    starts = ends - padded
    expert_ids = jnp.arange(n_experts, dtype=jnp.int32)

    def slots(expert_col, rank_col):
        onehot = route[:, expert_col:expert_col + 1].astype(jnp.int32) == expert_ids[None]
        return jnp.sum(jnp.where(onehot, starts[None], 0), axis=1) + route[:, rank_col].astype(jnp.int32)

    pos = jnp.concatenate([slots(0, 4).reshape(n_tok_tiles, 1, tm), slots(1, 5).reshape(n_tok_tiles, 1, tm)],
                          axis=2)
    n_tiles = (TOP_K * t) // tr + n_experts
    n_used = (ends[-1] // tr).reshape(1)
    tile_start = jnp.arange(n_tiles, dtype=jnp.int32) * tr
    tile_expert = jnp.sum((tile_start[:, None] >= ends[None]).astype(jnp.int32), axis=1)
    last_expert = jnp.sum((ends[-1] - tr >= ends).astype(jnp.int32))
    tile_expert = jnp.where(tile_start < ends[-1], tile_expert, last_expert)

    pos_spec = pl.BlockSpec((1, 1, 2 * tm), lambda i: (i, 0, 0), memory_space=pltpu.SMEM)
    smem_spec = pl.BlockSpec(memory_space=pltpu.SMEM)
    any_spec = pl.BlockSpec(memory_space=pl.ANY)
    pad_rows = jnp.stack([starts + cnt, padded - cnt], axis=1)
    tail_tiles = jnp.concatenate([n_used, jnp.full((1,), n_tiles, jnp.int32)])
    xs = pl.pallas_call(
        _moe_scatter_kernel,
        grid=(n_tok_tiles,),
        in_specs=[pos_spec, smem_spec, smem_spec, pl.BlockSpec((tm, d), lambda i: (i, 0))],
        out_specs=any_spec,
        out_shape=jax.ShapeDtypeStruct((n_tiles * tr, d), F32),
        scratch_shapes=[pltpu.VMEM((tr, d), F32), pltpu.SemaphoreType.DMA(()), pltpu.SemaphoreType.DMA(())],
        compiler_params=_cparams("arbitrary"),
        name="moe_scatter",
    )(pos, pad_rows, tail_tiles, x)

    tf = ff // 2 if (ff // 2) % V7X_LANES == 0 else ff
    nf = ff // tf
    used = lambda j, nu: jnp.minimum(j, nu[0] - 1)
    fcol = lambda j, f, nu: jnp.where(j < nu[0], f, nf - 1)
    ys = pl.pallas_call(
        _gmm_kernel,
        grid_spec=pltpu.PrefetchScalarGridSpec(
            num_scalar_prefetch=2,
            grid=(n_tiles, nf),
            in_specs=[pl.BlockSpec((tr, d), lambda j, f, te, nu: (used(j, nu), 0)),
                      pl.BlockSpec((None, d, tf), lambda j, f, te, nu: (te[j], 0, fcol(j, f, nu))),
                      pl.BlockSpec((None, d, tf), lambda j, f, te, nu: (te[j], 0, fcol(j, f, nu))),
                      pl.BlockSpec((None, tf, d), lambda j, f, te, nu: (te[j], fcol(j, f, nu), 0))],
            out_specs=pl.BlockSpec((tr, d), lambda j, f, te, nu: (j, 0)),
            scratch_shapes=[pltpu.VMEM((tr, d), BF16), pltpu.VMEM((tr, d), F32)]),
        out_shape=jax.ShapeDtypeStruct((n_tiles * tr, d), F32),
        compiler_params=_cparams("arbitrary", "arbitrary"),
        name="moe_experts",
    )(tile_expert, n_used, xs, wg, wu, wd)

    vec = pl.BlockSpec((1, d), lambda i: (0, 0))
    return pl.pallas_call(
        functools.partial(_moe_combine_kernel, alpha=alpha),
        grid=(n_tok_tiles,),
        in_specs=[pos_spec, pl.BlockSpec((tm, ROUTE_COLS), lambda i: (i, 0)),
                  pl.BlockSpec((tm, d), lambda i: (i, 0)), vec, vec, any_spec],
        out_specs=pl.BlockSpec((tm, d), lambda i: (i, 0)),
        out_shape=jax.ShapeDtypeStruct((t, d), F32),
        scratch_shapes=[pltpu.VMEM((tm, d), F32), pltpu.VMEM((tm, d), F32), pltpu.SemaphoreType.DMA(())],
        compiler_params=_cparams("arbitrary"),
        name="moe_combine",
    )(pos, route, x, ln_g.reshape(1, d), ln_b.reshape(1, d), ys)


def _run_trunk(x3, p, tables):
    bsz, seq, d = x3.shape
    depth = p['w_in'].shape[0]
    alpha = (2 * depth) ** 0.25
    att_w = ATT_HEADS * HEAD_W
    width = p['sc_conv_w'].shape[2]
    gate_w = 3 * d // 2
    assert att_w % width == 0 and (3 * att_w + 6 * width) % gate_w == 0, "column blocks must align"
    hy_col = att_w // width
    d_in = p['w_in'].shape[2]
    col_scale = jnp.asarray(np.where(np.arange(d_in) < att_w, LOG2E * ATT_HEAD_DIM ** -0.5, 1.0)
                            .astype(np.float32)[None])
    x = x3.reshape(bsz * seq, d)
    for i in range(depth):
        proj = _proj(x, p['w_in'][i], col_scale)
        proj3 = proj.reshape(bsz, seq, proj.shape[1])
        y_a = _attention(proj3, p['att_lambda'][i], p['att_subln_g'][i], i)
        u, x0, y_c = _convs(proj3, p['hy_short_w'][i], p['sc_conv_w'][i], hy_col=hy_col, sc_col=hy_col + 1)
        spec = _hyena_filter_spectrum(seq, tables, p['hf_w_in'][i], p['hf_b_in'][i], p['hf_w_mid'][i],
                                      p['hf_b_mid'][i], p['hf_freq'][i], p['hf_w_out'][i])
        y_h = _long_conv(u, x0, tables, spec, p['hy_bias'][i])
        flat = lambda a: a.reshape(bsz * seq, a.shape[2])
        j = i // 2
        x = _merge(flat(y_a), flat(y_h), flat(y_c), proj, x, p['w_branch'][i], p['w_out'][i],
                   p['ln1_g'][i], p['ln1_b'][i], alpha, gate_col=(3 * att_w + 6 * width) // gate_w)
        if i % 2 == 0:
            x = _ffn(x, p['ffn_w_gate'][j], p['ffn_w_up'][j], p['ffn_w_down'][j],
                     p['ln2_g'][i], p['ln2_b'][i], alpha)
        else:
            x = _moe(x, p['router_w'][j], p['moe_w_gate'][j], p['moe_w_up'][j], p['moe_w_down'][j],
                     p['ln2_g'][i], p['ln2_b'][i], alpha)
    return x.reshape(bsz, seq, d)


def kernel(x_prompt, x_sample, w_in, att_lambda, att_subln_g, hy_short_w, hf_w_in, hf_b_in,
           hf_w_mid, hf_b_mid, hf_freq, hf_w_out, hy_bias, sc_conv_w, w_branch, w_out,
           ln1_g, ln1_b, ln2_g, ln2_b, ffn_w_gate, ffn_w_up, ffn_w_down, router_w,
           moe_w_gate, moe_w_up, moe_w_down):
    bf = lambda a: a.astype(BF16)
    p = dict(w_in=bf(w_in), att_lambda=att_lambda, att_subln_g=att_subln_g, hy_short_w=hy_short_w,
             hf_w_in=hf_w_in, hf_b_in=hf_b_in, hf_w_mid=hf_w_mid, hf_b_mid=hf_b_mid,
             hf_freq=hf_freq, hf_w_out=hf_w_out, hy_bias=hy_bias, sc_conv_w=sc_conv_w,
             w_branch=bf(w_branch), w_out=bf(w_out), ln1_g=ln1_g, ln1_b=ln1_b, ln2_g=ln2_g, ln2_b=ln2_b,
             ffn_w_gate=bf(ffn_w_gate), ffn_w_up=bf(ffn_w_up), ffn_w_down=bf(ffn_w_down),
             router_w=router_w, moe_w_gate=bf(moe_w_gate), moe_w_up=bf(moe_w_up),
             moe_w_down=bf(moe_w_down))
    tables = {}
    outs = []
    for x3 in (x_prompt, x_sample):
        seq = x3.shape[1]
        if seq not in tables:
            tables[seq] = _dft_tables(seq)
        outs.append(_run_trunk(x3, p, tables[seq]))
    return tuple(outs)
```

```python
import functools
import math

import numpy as np
import jax
import jax.numpy as jnp
from jax import lax
from jax.experimental import pallas as pl
from jax.experimental.pallas import tpu as pltpu

F32 = jnp.float32
BF16 = jnp.bfloat16

ATT_HEADS = 4
ATT_HEAD_DIM = 64
HEAD_W = 2 * ATT_HEAD_DIM
HY_BANDS = 16
HY_TARGET = 1e-2
HY_FAST_PCT = 0.3
HY_SLOW_PCT = 1.5
TOP_K = 2
LN_EPS = 1e-5
RMS_EPS = 1e-5
LOG2E = math.log2(math.e)
ATT_MIN_Q_TILE = 256
ATT_TILE_BUDGET = 1 << 32
V7X_LANES = 128
V7X_BF16_SUBLANES = 16
V7X_VMEM_LIMIT = 56 * 1024 * 1024


def _cparams(*sem):
    return pltpu.CompilerParams(dimension_semantics=sem, vmem_limit_bytes=V7X_VMEM_LIMIT)


def _tile(n, pref):
    t = min(n, pref)
    while n % t:
        t //= 2
    return t


def _layer_norm(v, g, b):
    mu = jnp.mean(v, axis=-1, keepdims=True)
    d = v - mu
    var = jnp.mean(d * d, axis=-1, keepdims=True)
    return d * lax.rsqrt(var + LN_EPS) * g + b


def _proj_kernel(x_ref, w_ref, cs_ref, o_ref, xb_ref):
    @pl.when(pl.program_id(1) == 0)
    def _():
        xb_ref[...] = x_ref[...].astype(BF16)

    acc = jnp.dot(xb_ref[...], w_ref[...], preferred_element_type=F32)
    o_ref[...] = (acc * cs_ref[...]).astype(o_ref.dtype)


def _proj(x, w, col_scale, tn_pref=1536):
    t, d = x.shape
    n = w.shape[1]
    tm = _tile(t, 1024)
    tn = _tile(n, tn_pref)
    return pl.pallas_call(
        _proj_kernel,
        grid=(t // tm, n // tn),
        in_specs=[pl.BlockSpec((tm, d), lambda i, j: (i, 0)),
                  pl.BlockSpec((d, tn), lambda i, j: (0, j)),
                  pl.BlockSpec((1, tn), lambda i, j: (0, j))],
        out_specs=pl.BlockSpec((tm, tn), lambda i, j: (i, j)),
        out_shape=jax.ShapeDtypeStruct((t, n), BF16),
        scratch_shapes=[pltpu.VMEM((tm, d), BF16)],
        compiler_params=_cparams("parallel", "arbitrary"),
        name="proj",
    )(x, w, col_scale)


def _attn_kernel(slopes_ref, lam_ref, g_ref, q_ref, k_ref, v_ref, o_ref, bias_ref, *, lam_init, tq):
    h = pl.program_id(0)
    b = pl.program_id(1)
    qi = pl.program_id(2)
    seq = k_ref.shape[1]

    @pl.when(jnp.logical_and(b == 0, qi == 0))
    def _():
        r = lax.broadcasted_iota(jnp.int32, bias_ref.shape, 0)
        j = lax.broadcasted_iota(jnp.int32, bias_ref.shape, 1)
        bias_ref[...] = (-LOG2E * slopes_ref[h]) * jnp.abs(r - j + (seq - tq)).astype(F32)

    start = pl.multiple_of((pl.num_programs(2) - 1 - qi) * tq, tq)
    lf = lam_ref[...].astype(F32)
    lam_full = (jnp.exp(jnp.sum(lf[0:1] * lf[1:2], axis=1, keepdims=True))
                - jnp.exp(jnp.sum(lf[2:3] * lf[3:4], axis=1, keepdims=True)) + lam_init)
    q = q_ref[0]
    k = k_ref[0]
    v = v_ref[0]
    bias = bias_ref[:, pl.ds(start, seq)]
    lane = lax.broadcasted_iota(jnp.int32, q.shape, 1)
    zero = jnp.zeros_like(q)

    def softmax_pv(qm):
        s = lax.dot_general(qm, k, (((1,), (1,)), ((), ())), preferred_element_type=F32)
        m = jnp.max(s + bias, axis=1, keepdims=True)
        e = jnp.exp2(s + (bias - m))
        l = jnp.sum(e, axis=1, keepdims=True)
        return jnp.dot(e.astype(BF16), v, preferred_element_type=F32) * (1.0 / l)

    o = (softmax_pv(jnp.where(lane < ATT_HEAD_DIM, q, zero))
         - lam_full * softmax_pv(jnp.where(lane >= ATT_HEAD_DIM, q, zero)))
    o = o * lax.rsqrt(jnp.mean(o * o, axis=-1, keepdims=True) + RMS_EPS) * g_ref[...].astype(F32)
    o_ref[0] = (o * (1.0 - lam_init)).astype(o_ref.dtype)


def _attention(proj3, lam, subln_g, layer):
    bsz, seq, _ = proj3.shape
    tq = _tile(seq, max(ATT_MIN_Q_TILE, ATT_TILE_BUDGET // (seq * seq)))
    lam_init = 0.8 - 0.6 * math.exp(-0.3 * layer)
    slopes = jnp.asarray(2.0 ** (-8.0 * np.arange(1, ATT_HEADS + 1, dtype=np.float32) / ATT_HEADS))
    kern = functools.partial(_attn_kernel, lam_init=lam_init, tq=tq)
    return pl.pallas_call(
        kern,
        grid=(ATT_HEADS, bsz, seq // tq),
        in_specs=[pl.BlockSpec(memory_space=pltpu.SMEM),
                  pl.BlockSpec((4, ATT_HEAD_DIM), lambda h, b, i: (0, 0)),
                  pl.BlockSpec((1, HEAD_W), lambda h, b, i: (0, 0)),
                  pl.BlockSpec((1, tq, HEAD_W), lambda h, b, i: (b, i, h)),
                  pl.BlockSpec((1, seq, HEAD_W), lambda h, b, i: (b, 0, ATT_HEADS + h)),
                  pl.BlockSpec((1, seq, HEAD_W), lambda h, b, i: (b, 0, 2 * ATT_HEADS + h))],
        out_specs=pl.BlockSpec((1, tq, HEAD_W), lambda h, b, i: (b, i, h)),
        out_shape=jax.ShapeDtypeStruct((bsz, seq, ATT_HEADS * HEAD_W), BF16),
        scratch_shapes=[pltpu.VMEM((tq, 2 * seq - tq), F32)],
        compiler_params=_cparams("arbitrary", "arbitrary", "arbitrary"),
        name="attn",
    )(slopes, lam, subln_g.reshape(1, HEAD_W), proj3, proj3, proj3)


def _conv3(x, prev_row, next_row, w):
    rows = x.shape[0]
    row = lax.broadcasted_iota(jnp.int32, (rows, 1), 0)
    x_prev = jnp.where(row == 0, prev_row, pltpu.roll(x, 1, axis=0))
    x_next = jnp.where(row == rows - 1, next_row, pltpu.roll(x, rows - 1, axis=0))
    return w[0:1] * x_prev + w[1:2] * x + w[2:3] * x_next


def _conv_kernel(hy_ref, hyp_ref, hyn_ref, sc_ref, scp_ref, scn_ref, hw_ref, sw_ref,
                 u_ref, x0_ref, yc_ref):
    i = pl.program_id(1)
    last = pl.num_programs(1) - 1
    width = u_ref.shape[2]
    has_prev = jnp.where(i > 0, 1.0, 0.0)
    has_next = jnp.where(i < last, 1.0, 0.0)
    hsub = hyp_ref.shape[1]

    hy = hy_ref[0].astype(F32)
    hy_prev = hyp_ref[0].astype(F32)[hsub - 1:hsub] * has_prev
    hy_next = hyn_ref[0].astype(F32)[0:1] * has_next
    hyc = _conv3(hy, hy_prev, hy_next, hw_ref[...].astype(F32))
    x0_ref[0] = hyc[:, :width].astype(x0_ref.dtype)
    u_ref[0] = (hyc[:, 2 * width:] * hyc[:, width:2 * width]).astype(u_ref.dtype)

    def gated(blk):
        return blk[:, width:2 * width] * blk[:, 2 * width:]

    sc = sc_ref[0].astype(F32)
    cx_prev = gated(scp_ref[0].astype(F32)[hsub - 1:hsub]) * has_prev
    cx_next = gated(scn_ref[0].astype(F32)[0:1]) * has_next
    conv = _conv3(gated(sc), cx_prev, cx_next, sw_ref[...].astype(F32))
    yc_ref[0] = (sc[:, :width] * conv).astype(yc_ref.dtype)


def _convs(proj3, hy_w, sc_w, hy_col, sc_col, tl_pref=512):
    bsz, seq, _ = proj3.shape
    width = sc_w.shape[1]
    tl = _tile(seq, tl_pref)
    hs = V7X_BF16_SUBLANES
    nh = tl // hs
    nblk = seq // hs

    def main(col):
        return pl.BlockSpec((1, tl, 3 * width), lambda b, i: (b, i, col))

    def prev(col):
        return pl.BlockSpec((1, hs, 3 * width), lambda b, i: (b, jnp.maximum(i * nh - 1, 0), col))

    def nxt(col):
        return pl.BlockSpec((1, hs, 3 * width), lambda b, i: (b, jnp.minimum((i + 1) * nh, nblk - 1), col))

    out = jax.ShapeDtypeStruct((bsz, seq, width), BF16)
    ospec = pl.BlockSpec((1, tl, width), lambda b, i: (b, i, 0))
    return pl.pallas_call(
        _conv_kernel,
        grid=(bsz, seq // tl),
        in_specs=[main(hy_col), prev(hy_col), nxt(hy_col), main(sc_col), prev(sc_col), nxt(sc_col),
                  pl.BlockSpec((3, 3 * width), lambda b, i: (0, 0)),
                  pl.BlockSpec((3, width), lambda b, i: (0, 0))],
        out_specs=[ospec, ospec, ospec],
        out_shape=[out, out, out],
        compiler_params=_cparams("parallel", "arbitrary"),
        name="conv",
    )(proj3, proj3, proj3, proj3, proj3, proj3, hy_w, sc_w)


def _tab_kernel(c_ref, s_ref, st_ref, cb_ref, sb_ref, *, seq):
    tr = c_ref.shape[0]
    i = pl.program_id(0)
    dr = lax.broadcasted_iota(jnp.int32, (tr, seq), 0)
    c = lax.broadcasted_iota(jnp.int32, (tr, seq), 1)
    r = i * tr + dr

    def angle(prod):
        return (prod & (2 * seq - 1)).astype(F32) * (math.pi / seq)

    @pl.when(i == 0)
    def _():
        base = angle(dr * c)
        cb_ref[...] = jnp.cos(base)
        sb_ref[...] = jnp.sin(base)

    col = lax.broadcasted_iota(jnp.int32, (1, seq), 1)
    lead = angle((i * tr) * col)
    c0, s0 = jnp.cos(lead), jnp.sin(lead)
    cb, sb = cb_ref[...], sb_ref[...]
    cosv = c0 * cb - s0 * sb
    nsin = -(s0 * cb + c0 * sb)
    c_ref[...] = cosv.astype(c_ref.dtype)
    s_ref[...] = jnp.where(r == 0, jnp.where((c & 1) == 0, 1.0, -1.0), nsin).astype(s_ref.dtype)
    st_ref[...] = jnp.where(c == 0, jnp.where((r & 1) == 0, 1.0, -1.0), nsin).astype(st_ref.dtype)


def _dft_tables(seq):
    assert seq & (seq - 1) == 0, "sequence length must be a power of two"
    tr = _tile(seq, 256)
    spec = pl.BlockSpec((tr, seq), lambda i: (i, 0))
    out = jax.ShapeDtypeStruct((seq, seq), BF16)
    return pl.pallas_call(
        functools.partial(_tab_kernel, seq=seq),
        grid=(seq // tr,),
        out_specs=[spec, spec, spec],
        out_shape=[out, out, out],
        scratch_shapes=[pltpu.VMEM((tr, seq), F32), pltpu.VMEM((tr, seq), F32)],
        compiler_params=_cparams("arbitrary"),
        name="dft_tables",
    )()


def _filt_kernel(z_ref, win_ref, bin_ref, wmid_ref, bmid_ref, fr_ref, wout_ref, dl_ref,
                 hs_ref, l1_ref, *, seq):
    i = pl.program_id(0)
    tl = z_ref.shape[0]
    width = dl_ref.shape[1]
    hi = lax.Precision.HIGHEST
    fr = fr_ref[...]
    h = jnp.sin(fr * (jnp.dot(z_ref[...], win_ref[...], precision=hi, preferred_element_type=F32)
                      + bin_ref[...]))
    for j in range(wmid_ref.shape[0]):
        h = jnp.sin(fr * (jnp.dot(h, wmid_ref[j], precision=hi, preferred_element_type=F32)
                          + bmid_ref[j]))
    ho = jnp.dot(h, wout_ref[...], precision=hi, preferred_element_type=F32)
    pos = i * tl + lax.broadcasted_iota(jnp.int32, (tl, 1), 0)
    t = pos.astype(F32) / float(seq - 1)
    window = jnp.exp(-t * dl_ref[...])
    h_fwd = ho[:, :width] * window
    h_bwd = jnp.where(pos == 0, 0.0, ho[:, width:] * window)
    hs_ref[:, :width] = h_fwd + h_bwd
    hs_ref[:, width:] = h_fwd - h_bwd

    @pl.when(i == 0)
    def _():
        l1_ref[...] = jnp.zeros_like(l1_ref)

    l1_ref[...] += jnp.sum(jnp.abs(h_fwd) + jnp.abs(h_bwd), axis=0, keepdims=True)


def _fspec_kernel(c_ref, s_ref, hs_ref, l1_ref, a_ref, b_ref, d_ref, *, seq):
    i = pl.program_id(0)
    tk = c_ref.shape[0]
    width = l1_ref.shape[1]
    hs = hs_ref[...].astype(BF16)
    xc = jnp.dot(c_ref[...], hs[:, :width], preferred_element_type=F32)
    xs = jnp.dot(s_ref[...], hs, preferred_element_type=F32)
    k = i * tk + lax.broadcasted_iota(jnp.int32, (tk, 1), 0)
    inv_l1 = 1.0 / l1_ref[...]
    wk = jnp.where(k == 0, 1.0, 2.0) / float(2 * seq)
    a = xc * wk * inv_l1
    a_ref[...] = a
    b_ref[...] = jnp.where(k == 0, 0.0, xs[:, width:] * wk * inv_l1)
    d_ref[...] = jnp.where(k == 0, xs[:, :width] * wk * inv_l1, a)


def _hyena_filter_spectrum(seq, tabs, w_in, b_in, w_mid, b_mid, freq, w_out):
    c_tab, s_tab, _ = tabs
    emb, fw = w_in.shape
    width = w_out.shape[1] // 2
    n = np.arange(seq, dtype=np.float64)[:, None]
    f = np.linspace(1e-4, HY_BANDS - 1, HY_BANDS)[None]
    w = 2.0 * math.pi * n / seq
    z = np.concatenate([n / (seq - 1), np.cos(f * w), -np.sin(f * w)], -1)
    z = np.pad(z, ((0, 0), (0, V7X_LANES - emb))).astype(np.float32)
    w_in_p = jnp.pad(w_in.astype(F32), ((0, V7X_LANES - emb), (0, 0)))
    deltas = np.abs(np.linspace(math.log(HY_TARGET) / HY_SLOW_PCT, math.log(HY_TARGET) / HY_FAST_PCT,
                                width)).astype(np.float32)[None]
    tl = _tile(seq, 512)
    full = lambda *shape: pl.BlockSpec(shape, lambda i: (0,) * len(shape))
    hs, l1 = pl.pallas_call(
        functools.partial(_filt_kernel, seq=seq),
        grid=(seq // tl,),
        in_specs=[pl.BlockSpec((tl, V7X_LANES), lambda i: (i, 0)),
                  full(V7X_LANES, fw), full(1, fw), full(*w_mid.shape), full(w_mid.shape[0], 1, fw),
                  full(1, fw), full(fw, 2 * width), full(1, width)],
        out_specs=[pl.BlockSpec((tl, 2 * width), lambda i: (i, 0)), full(1, width)],
        out_shape=[jax.ShapeDtypeStruct((seq, 2 * width), F32), jax.ShapeDtypeStruct((1, width), F32)],
        compiler_params=_cparams("arbitrary"),
        name="hyena_filter",
    )(jnp.asarray(z), w_in_p, b_in.reshape(1, fw).astype(F32), w_mid.astype(F32),
      b_mid.reshape(w_mid.shape[0], 1, fw).astype(F32), freq.reshape(1, fw).astype(F32),
      w_out.astype(F32), jnp.asarray(deltas))

    tk = _tile(seq, 256)
    ospec = pl.BlockSpec((tk, width), lambda i: (i, 0))
    out = jax.ShapeDtypeStruct((seq, width), F32)
    return pl.pallas_call(
        functools.partial(_fspec_kernel, seq=seq),
        grid=(seq // tk,),
        in_specs=[pl.BlockSpec((tk, seq), lambda i: (i, 0)), pl.BlockSpec((tk, seq), lambda i: (i, 0)),
                  full(seq, 2 * width), full(1, width)],
        out_specs=[ospec, ospec, ospec],
        out_shape=[out, out, out],
        compiler_params=_cparams("parallel"),
        name="filter_spectrum",
    )(c_tab, s_tab, hs, l1)


def _dft_fwd_kernel(c_ref, s_ref, u_ref, a_ref, b_ref, d_ref, yre_ref, yim_ref):
    u = u_ref[0]
    x_re = jnp.dot(c_ref[...], u, preferred_element_type=F32)
    x_im = jnp.dot(s_ref[...], u, preferred_element_type=F32)
    b = b_ref[...]
    yre_ref[0] = (x_re * a_ref[...] - x_im * b).astype(yre_ref.dtype)
    yim_ref[0] = (x_re * b + x_im * d_ref[...]).astype(yim_ref.dtype)


def _dft_inv_kernel(c_ref, st_ref, yre_ref, yim_ref, u_ref, x0_ref, bias_ref, o_ref):
    y = (jnp.dot(c_ref[...], yre_ref[0], preferred_element_type=F32)
         + jnp.dot(st_ref[...], yim_ref[0], preferred_element_type=F32))
    y = y + u_ref[0].astype(F32) * bias_ref[...]
    o_ref[0] = (x0_ref[0].astype(F32) * y).astype(o_ref.dtype)


def _long_conv(u, x0, tabs, spec, bias, tk_pref=512):
    c_tab, s_tab, st_tab = tabs
    a, b, d = spec
    bsz, seq, width = u.shape
    tk = _tile(seq, tk_pref)
    tab = pl.BlockSpec((tk, seq), lambda j, bb: (j, 0))
    whole = pl.BlockSpec((1, seq, width), lambda j, bb: (bb, 0, 0))
    rows = pl.BlockSpec((1, tk, width), lambda j, bb: (bb, j, 0))
    coef = pl.BlockSpec((tk, width), lambda j, bb: (j, 0))
    spec_out = jax.ShapeDtypeStruct((bsz, seq, width), BF16)
    y_re, y_im = pl.pallas_call(
        _dft_fwd_kernel,
        grid=(seq // tk, bsz),
        in_specs=[tab, tab, whole, coef, coef, coef],
        out_specs=[rows, rows],
        out_shape=[spec_out, spec_out],
        compiler_params=_cparams("parallel", "arbitrary"),
        name="dft_fwd",
    )(c_tab, s_tab, u, a, b, d)
    return pl.pallas_call(
        _dft_inv_kernel,
        grid=(seq // tk, bsz),
        in_specs=[tab, tab, whole, whole, rows, rows, pl.BlockSpec((1, width), lambda j, bb: (0, 0))],
        out_specs=rows,
        out_shape=spec_out,
        compiler_params=_cparams("parallel", "arbitrary"),
        name="dft_inv",
    )(c_tab, st_tab, y_re, y_im, u, x0, bias.reshape(1, width).astype(F32))


def _merge_kernel(ya_ref, yh_ref, yc_ref, g0_ref, g1_ref, x_ref, wb_ref, wo_ref, lg_ref, lb_ref,
                  o_ref, m_ref, *, alpha):
    d = o_ref.shape[1]
    half = d // 2
    g0 = g0_ref[...]
    g1 = g1_ref[...]
    gates = ((g0[:, :half], g0[:, d:], g1[:, half:d]),
             (g0[:, half:d], g1[:, :half], g1[:, d:]))
    branches = (ya_ref[...], yh_ref[...], yc_ref[...])
    for c in range(2):
        acc = None
        for j in range(3):
            y = jnp.dot(branches[j], wb_ref[j, :, c * half:(c + 1) * half], preferred_element_type=F32)
            term = jax.nn.sigmoid(gates[c][j].astype(F32)) * y
            acc = term if acc is None else acc + term
        m_ref[:, c * half:(c + 1) * half] = acc.astype(m_ref.dtype)
    t = jnp.dot(m_ref[...], wo_ref[...], preferred_element_type=F32)
    o_ref[...] = _layer_norm(alpha * x_ref[...] + t, lg_ref[...], lb_ref[...])


def _merge(y_a, y_h, y_c, proj, x, wb, wo, ln_g, ln_b, alpha, gate_col, tm_pref=512):
    t, d = x.shape
    bw = y_a.shape[1]
    tm = _tile(t, tm_pref)
    row = lambda w: pl.BlockSpec((tm, w), lambda i: (i, 0))
    fixed = lambda *shape: pl.BlockSpec(shape, lambda i: (0,) * len(shape))
    gw = 3 * d // 2
    return pl.pallas_call(
        functools.partial(_merge_kernel, alpha=alpha),
        grid=(t // tm,),
        in_specs=[row(bw), row(bw), row(bw),
                  pl.BlockSpec((tm, gw), lambda i: (i, gate_col)),
                  pl.BlockSpec((tm, gw), lambda i: (i, gate_col + 1)),
                  row(d), fixed(*wb.shape), fixed(*wo.shape), fixed(1, d), fixed(1, d)],
        out_specs=row(d),
        out_shape=jax.ShapeDtypeStruct((t, d), F32),
        scratch_shapes=[pltpu.VMEM((tm, d), BF16)],
        compiler_params=_cparams("parallel"),
        name="merge",
    )(y_a, y_h, y_c, proj, proj, x, wb, wo, ln_g.reshape(1, d), ln_b.reshape(1, d))


def _swiglu_partial(xb, wg, wu, wd):
    g = jnp.dot(xb, wg, preferred_element_type=F32)
    u = jnp.dot(xb, wu, preferred_element_type=F32)
    return jnp.dot((jax.nn.silu(g) * u).astype(BF16), wd, preferred_element_type=F32)


def _ffn_kernel(x_ref, wg_ref, wu_ref, wd_ref, lg_ref, lb_ref, o_ref, xb_ref, acc_ref, *, alpha):
    f = pl.program_id(1)

    @pl.when(f == 0)
    def _():
        xb_ref[...] = x_ref[...].astype(BF16)
        acc_ref[...] = jnp.zeros_like(acc_ref)

    acc_ref[...] += _swiglu_partial(xb_ref[...], wg_ref[...], wu_ref[...], wd_ref[...])

    @pl.when(f == pl.num_programs(1) - 1)
    def _():
        o_ref[...] = _layer_norm(alpha * x_ref[...] + acc_ref[...], lg_ref[...], lb_ref[...])


def _ffn(x, wg, wu, wd, ln_g, ln_b, alpha):
    t, d = x.shape
    ff = wg.shape[1]
    tm = _tile(t, 1024)
    tf = ff // 2 if (ff // 2) % V7X_LANES == 0 else ff
    vec = pl.BlockSpec((1, d), lambda i, f: (0, 0))
    return pl.pallas_call(
        functools.partial(_ffn_kernel, alpha=alpha),
        grid=(t // tm, ff // tf),
        in_specs=[pl.BlockSpec((tm, d), lambda i, f: (i, 0)),
                  pl.BlockSpec((d, tf), lambda i, f: (0, f)),
                  pl.BlockSpec((d, tf), lambda i, f: (0, f)),
                  pl.BlockSpec((tf, d), lambda i, f: (f, 0)),
                  vec, vec],
        out_specs=pl.BlockSpec((tm, d), lambda i, f: (i, 0)),
        out_shape=jax.ShapeDtypeStruct((t, d), F32),
        scratch_shapes=[pltpu.VMEM((tm, d), BF16), pltpu.VMEM((tm, d), F32)],
        compiler_params=_cparams("parallel", "arbitrary"),
        name="ffn",
    )(x, wg, wu, wd, ln_g.reshape(1, d), ln_b.reshape(1, d))


def _top2(logits, n_experts):
    lane = lax.broadcasted_iota(jnp.int32, logits.shape, 1).astype(F32)
    neg = jnp.float32(-jnp.inf)
    sentinel = jnp.float32(n_experts)
    lg = jnp.where(lane < sentinel, logits, neg)
    m1 = jnp.max(lg, axis=1, keepdims=True)
    i1 = jnp.min(jnp.where(lg == m1, lane, sentinel), axis=1, keepdims=True)
    rest = jnp.where(lane == i1, neg, lg)
    m2 = jnp.max(rest, axis=1, keepdims=True)
    i2 = jnp.min(jnp.where(rest == m2, lane, sentinel), axis=1, keepdims=True)
    e2 = jnp.exp(m2 - m1)
    p1 = 1.0 / (1.0 + e2)
    p2 = e2 / (1.0 + e2)
    return lane == i1, lane == i2, i1, i2, p1, p2


ROUTE_COLS = 8
MOE_TILE_ROWS = 512
MOE_TOKEN_TILE = 512
ROW_DMA_UNROLL = 8


def _router_kernel(x_ref, r_ref, route_ref, cnt_ref, tri_ref, *, n_experts):
    i = pl.program_id(0)
    x = x_ref[...]
    tm = x.shape[0]

    @pl.when(i == 0)
    def _():
        cnt_ref[...] = jnp.zeros_like(cnt_ref)
        row = lax.broadcasted_iota(jnp.int32, (tm, tm), 0)
        col = lax.broadcasted_iota(jnp.int32, (tm, tm), 1)
        tri_ref[...] = jnp.where(col < row, 1.0, 0.0).astype(tri_ref.dtype)

    w = r_ref[...]
    x_hi, w_hi = x.astype(BF16), w.astype(BF16)
    x_lo, w_lo = (x - x_hi.astype(F32)).astype(BF16), (w - w_hi.astype(F32)).astype(BF16)
    logits = (jnp.dot(x_hi, w_hi, preferred_element_type=F32) + jnp.dot(x_lo, w_hi, preferred_element_type=F32)
              + jnp.dot(x_hi, w_lo, preferred_element_type=F32))
    sel1, sel2, i1, i2, p1, p2 = _top2(logits, n_experts)
    chosen = jnp.where(sel1, 1.0, jnp.where(sel2, 1.0, 0.0))
    before = jnp.dot(tri_ref[...], chosen.astype(tri_ref.dtype), preferred_element_type=F32) + cnt_ref[...]
    r1 = jnp.sum(jnp.where(sel1, before, 0.0), axis=1, keepdims=True)
    r2 = jnp.sum(jnp.where(sel2, before, 0.0), axis=1, keepdims=True)
    cnt_ref[...] += jnp.sum(chosen, axis=0, keepdims=True)
    lane = lax.broadcasted_iota(jnp.int32, logits.shape, 1)
    record = jnp.zeros_like(logits)
    for c, val in enumerate((i1, i2, p1, p2, r1, r2)):
        record = jnp.where(lane == c, val, record)
    route_ref[...] = record[:, :ROUTE_COLS]


def _moe_scatter_kernel(pos_ref, pad_ref, tail_ref, x_ref, xs_ref, zero_ref, sem, zsem):
    tm = x_ref.shape[0]
    tr = zero_ref.shape[0]

    @pl.when(pl.program_id(0) == 0)
    def _():
        zero_ref[...] = jnp.zeros_like(zero_ref)

        def pad_copy(first, r):
            return pltpu.make_async_copy(zero_ref.at[pl.ds(0, 1)], xs_ref.at[pl.ds(first + r, 1)], zsem)

        def tile_copy(t):
            rows = pl.ds(pl.multiple_of(t * tr, tr), tr)
            return pltpu.make_async_copy(zero_ref, xs_ref.at[rows], zsem)

        def each(copy_fn, method):
            for e in range(pad_ref.shape[0]):
                first, count = pad_ref[e, 0], pad_ref[e, 1]
                lax.fori_loop(0, count, lambda r, c: (getattr(copy_fn(first, r), method)(), c)[1], 0)

        each(pad_copy, "start")
        lax.fori_loop(tail_ref[0], tail_ref[1], lambda t, c: (tile_copy(t).start(), c)[1], 0)
        each(pad_copy, "wait")
        lax.fori_loop(tail_ref[0], tail_ref[1], lambda t, c: (tile_copy(t).wait(), c)[1], 0)

    def row_copy(r, k):
        slot = pos_ref[0, 0, k * tm + r]
        return pltpu.make_async_copy(x_ref.at[pl.ds(r, 1)], xs_ref.at[pl.ds(slot, 1)], sem)

    def start(r, carry):
        row_copy(r, 0).start()
        row_copy(r, 1).start()
        return carry

    def wait(r, carry):
        row_copy(r, 0).wait()
        row_copy(r, 1).wait()
        return carry

    lax.fori_loop(0, tm, start, 0, unroll=ROW_DMA_UNROLL)
    lax.fori_loop(0, tm, wait, 0, unroll=ROW_DMA_UNROLL)


def _gmm_kernel(te_ref, nu_ref, xs_ref, wg_ref, wu_ref, wd_ref, ys_ref, xb_ref, acc_ref):
    del te_ref
    j = pl.program_id(0)
    f = pl.program_id(1)

    @pl.when(jnp.logical_and(j >= nu_ref[0], f == 0))
    def _():
        ys_ref[...] = jnp.zeros_like(ys_ref)

    @pl.when(j < nu_ref[0])
    def _():
        @pl.when(f == 0)
        def _():
            xb_ref[...] = xs_ref[...].astype(BF16)
            acc_ref[...] = jnp.zeros_like(acc_ref)

        acc_ref[...] += _swiglu_partial(xb_ref[...], wg_ref[...], wu_ref[...], wd_ref[...])

        @pl.when(f == pl.num_programs(1) - 1)
        def _():
            ys_ref[...] = acc_ref[...]


def _moe_combine_kernel(pos_ref, route_ref, x_ref, lg_ref, lb_ref, ys_ref, o_ref, y1_ref, y2_ref, sem,
                        *, alpha):
    tm = x_ref.shape[0]

    def row_copy(r, k):
        slot = pos_ref[0, 0, k * tm + r]
        dst = (y1_ref, y2_ref)[k]
        return pltpu.make_async_copy(ys_ref.at[pl.ds(slot, 1)], dst.at[pl.ds(r, 1)], sem)

    def start(r, carry):
        row_copy(r, 0).start()
        row_copy(r, 1).start()
        return carry

    def wait(r, carry):
        row_copy(r, 0).wait()
        row_copy(r, 1).wait()
        return carry

    lax.fori_loop(0, tm, start, 0, unroll=ROW_DMA_UNROLL)
    lax.fori_loop(0, tm, wait, 0, unroll=ROW_DMA_UNROLL)
    route = route_ref[...]
    mixed = route[:, 2:3] * y1_ref[...] + route[:, 3:4] * y2_ref[...]
    o_ref[...] = _layer_norm(alpha * x_ref[...] + mixed, lg_ref[...], lb_ref[...])


def _moe(x, router, wg, wu, wd, ln_g, ln_b, alpha):
    t, d = x.shape
    n_experts, _, ff = wg.shape
    tm = _tile(t, MOE_TOKEN_TILE)
    tr = MOE_TILE_ROWS
    n_tok_tiles = t // tm

    router_p = jnp.pad(router.astype(F32), ((0, 0), (0, V7X_LANES - n_experts)))
    route, counts = pl.pallas_call(
        functools.partial(_router_kernel, n_experts=n_experts),
        grid=(n_tok_tiles,),
        in_specs=[pl.BlockSpec((tm, d), lambda i: (i, 0)),
                  pl.BlockSpec((d, V7X_LANES), lambda i: (0, 0))],
        out_specs=[pl.BlockSpec((tm, ROUTE_COLS), lambda i: (i, 0)),
                   pl.BlockSpec((1, V7X_LANES), lambda i: (0, 0))],
        out_shape=[jax.ShapeDtypeStruct((t, ROUTE_COLS), F32), jax.ShapeDtypeStruct((1, V7X_LANES), F32)],
        scratch_shapes=[pltpu.VMEM((tm, tm), BF16)],
        compiler_params=_cparams("arbitrary"),
        name="moe_router",
    )(x, router_p)

    cnt = counts[0, :n_experts].astype(jnp.int32)
    padded = (cnt + tr - 1) // tr * tr
    ends = jnp.cumsum(padded)
    starts = ends - padded
    expert_ids = jnp.arange(n_experts, dtype=jnp.int32)

    def slots(expert_col, rank_col):
        onehot = route[:, expert_col:expert_col + 1].astype(jnp.int32) == expert_ids[None]
        return jnp.sum(jnp.where(onehot, starts[None], 0), axis=1) + route[:, rank_col].astype(jnp.int32)

    pos = jnp.concatenate([slots(0, 4).reshape(n_tok_tiles, 1, tm), slots(1, 5).reshape(n_tok_tiles, 1, tm)],
                          axis=2)
    n_tiles = (TOP_K * t) // tr + n_experts
    n_used = (ends[-1] // tr).reshape(1)
    tile_start = jnp.arange(n_tiles, dtype=jnp.int32) * tr
    tile_expert = jnp.sum((tile_start[:, None] >= ends[None]).astype(jnp.int32), axis=1)
    last_expert = jnp.sum((ends[-1] - tr >= ends).astype(jnp.int32))
    tile_expert = jnp.where(tile_start < ends[-1], tile_expert, last_expert)

    pos_spec = pl.BlockSpec((1, 1, 2 * tm), lambda i: (i, 0, 0), memory_space=pltpu.SMEM)
    smem_spec = pl.BlockSpec(memory_space=pltpu.SMEM)
    any_spec = pl.BlockSpec(memory_space=pl.ANY)
    pad_rows = jnp.stack([starts + cnt, padded - cnt], axis=1)
    tail_tiles = jnp.concatenate([n_used, jnp.full((1,), n_tiles, jnp.int32)])
    xs = pl.pallas_call(
        _moe_scatter_kernel,
        grid=(n_tok_tiles,),
        in_specs=[pos_spec, smem_spec, smem_spec, pl.BlockSpec((tm, d), lambda i: (i, 0))],
        out_specs=any_spec,
        out_shape=jax.ShapeDtypeStruct((n_tiles * tr, d), F32),
        scratch_shapes=[pltpu.VMEM((tr, d), F32), pltpu.SemaphoreType.DMA(()), pltpu.SemaphoreType.DMA(())],
        compiler_params=_cparams("arbitrary"),
        name="moe_scatter",
    )(pos, pad_rows, tail_tiles, x)

    tf = ff // 2 if (ff // 2) % V7X_LANES == 0 else ff
    nf = ff // tf
    used = lambda j, nu: jnp.minimum(j, nu[0] - 1)
    fcol = lambda j, f, nu: jnp.where(j < nu[0], f, nf - 1)
    ys = pl.pallas_call(
        _gmm_kernel,
        grid_spec=pltpu.PrefetchScalarGridSpec(
            num_scalar_prefetch=2,
            grid=(n_tiles, nf),
            in_specs=[pl.BlockSpec((tr, d), lambda j, f, te, nu: (used(j, nu), 0)),
                      pl.BlockSpec((None, d, tf), lambda j, f, te, nu: (te[j], 0, fcol(j, f, nu))),
                      pl.BlockSpec((None, d, tf), lambda j, f, te, nu: (te[j], 0, fcol(j, f, nu))),
                      pl.BlockSpec((None, tf, d), lambda j, f, te, nu: (te[j], fcol(j, f, nu), 0))],
            out_specs=pl.BlockSpec((tr, d), lambda j, f, te, nu: (j, 0)),
            scratch_shapes=[pltpu.VMEM((tr, d), BF16), pltpu.VMEM((tr, d), F32)]),
        out_shape=jax.ShapeDtypeStruct((n_tiles * tr, d), F32),
        compiler_params=_cparams("arbitrary", "arbitrary"),
        name="moe_experts",
    )(tile_expert, n_used, xs, wg, wu, wd)

    vec = pl.BlockSpec((1, d), lambda i: (0, 0))
    return pl.pallas_call(
        functools.partial(_moe_combine_kernel, alpha=alpha),
        grid=(n_tok_tiles,),
        in_specs=[pos_spec, pl.BlockSpec((tm, ROUTE_COLS), lambda i: (i, 0)),
                  pl.BlockSpec((tm, d), lambda i: (i, 0)), vec, vec, any_spec],
        out_specs=pl.BlockSpec((tm, d), lambda i: (i, 0)),
        out_shape=jax.ShapeDtypeStruct((t, d), F32),
        scratch_shapes=[pltpu.VMEM((tm, d), F32), pltpu.VMEM((tm, d), F32), pltpu.SemaphoreType.DMA(())],
        compiler_params=_cparams("arbitrary"),
        name="moe_combine",
    )(pos, route, x, ln_g.reshape(1, d), ln_b.reshape(1, d), ys)


def _run_trunk(x3, p, tables):
    bsz, seq, d = x3.shape
    depth = p['w_in'].shape[0]
    alpha = (2 * depth) ** 0.25
    att_w = ATT_HEADS * HEAD_W
    width = p['sc_conv_w'].shape[2]
    gate_w = 3 * d // 2
    assert att_w % width == 0 and (3 * att_w + 6 * width) % gate_w == 0, "column blocks must align"
    hy_col = att_w // width
    d_in = p['w_in'].shape[2]
    col_scale = jnp.asarray(np.where(np.arange(d_in) < att_w, LOG2E * ATT_HEAD_DIM ** -0.5, 1.0)
                            .astype(np.float32)[None])
    x = x3.reshape(bsz * seq, d)
    for i in range(depth):
        alt = i > 0
        proj = _proj(x, p['w_in'][i], col_scale, 2560 if alt else 1536)
        proj3 = proj.reshape(bsz, seq, proj.shape[1])
        y_a = _attention(proj3, p['att_lambda'][i], p['att_subln_g'][i], i)
        u, x0, y_c = _convs(proj3, p['hy_short_w'][i], p['sc_conv_w'][i], hy_col=hy_col, sc_col=hy_col + 1,
                            tl_pref=1024 if alt else 512)
        spec = _hyena_filter_spectrum(seq, tables, p['hf_w_in'][i], p['hf_b_in'][i], p['hf_w_mid'][i],
                                      p['hf_b_mid'][i], p['hf_freq'][i], p['hf_w_out'][i])
        y_h = _long_conv(u, x0, tables, spec, p['hy_bias'][i], 256 if alt else 512)
        flat = lambda a: a.reshape(bsz * seq, a.shape[2])
        j = i // 2
        x = _merge(flat(y_a), flat(y_h), flat(y_c), proj, x, p['w_branch'][i], p['w_out'][i],
                   p['ln1_g'][i], p['ln1_b'][i], alpha, gate_col=(3 * att_w + 6 * width) // gate_w,
                   tm_pref=256 if alt else 512)
        if i % 2 == 0:
            x = _ffn(x, p['ffn_w_gate'][j], p['ffn_w_up'][j], p['ffn_w_down'][j],
                     p['ln2_g'][i], p['ln2_b'][i], alpha)
        else:
            x = _moe(x, p['router_w'][j], p['moe_w_gate'][j], p['moe_w_up'][j], p['moe_w_down'][j],
                     p['ln2_g'][i], p['ln2_b'][i], alpha)
    return x.reshape(bsz, seq, d)


def kernel(x_prompt, x_sample, w_in, att_lambda, att_subln_g, hy_short_w, hf_w_in, hf_b_in,
           hf_w_mid, hf_b_mid, hf_freq, hf_w_out, hy_bias, sc_conv_w, w_branch, w_out,
           ln1_g, ln1_b, ln2_g, ln2_b, ffn_w_gate, ffn_w_up, ffn_w_down, router_w,
           moe_w_gate, moe_w_up, moe_w_down):
    bf = lambda a: a.astype(BF16)
    p = dict(w_in=bf(w_in), att_lambda=att_lambda, att_subln_g=att_subln_g, hy_short_w=hy_short_w,
             hf_w_in=hf_w_in, hf_b_in=hf_b_in, hf_w_mid=hf_w_mid, hf_b_mid=hf_b_mid,
             hf_freq=hf_freq, hf_w_out=hf_w_out, hy_bias=hy_bias, sc_conv_w=sc_conv_w,
             w_branch=bf(w_branch), w_out=bf(w_out), ln1_g=ln1_g, ln1_b=ln1_b, ln2_g=ln2_g, ln2_b=ln2_b,
             ffn_w_gate=bf(ffn_w_gate), ffn_w_up=bf(ffn_w_up), ffn_w_down=bf(ffn_w_down),
             router_w=router_w, moe_w_gate=bf(moe_w_gate), moe_w_up=bf(moe_w_up),
             moe_w_down=bf(moe_w_down))
    tables = {}
    outs = []
    for x3 in (x_prompt, x_sample):
        seq = x3.shape[1]
        if seq not in tables:
            tables[seq] = _dft_tables(seq)
        outs.append(_run_trunk(x3, p, tables[seq]))
    return tuple(outs)
```

```python
import functools
import math

import numpy as np
import jax
import jax.numpy as jnp
from jax import lax
from jax.experimental import pallas as pl
from jax.experimental.pallas import tpu as pltpu

F32 = jnp.float32
BF16 = jnp.bfloat16

ATT_HEADS = 4
ATT_HEAD_DIM = 64
HEAD_W = 2 * ATT_HEAD_DIM
HY_BANDS = 16
HY_TARGET = 1e-2
HY_FAST_PCT = 0.3
HY_SLOW_PCT = 1.5
TOP_K = 2
LN_EPS = 1e-5
RMS_EPS = 1e-5
LOG2E = math.log2(math.e)
ATT_MIN_Q_TILE = 256
ATT_TILE_BUDGET = 1 << 32
V7X_LANES = 128
V7X_BF16_SUBLANES = 16
V7X_VMEM_LIMIT = 56 * 1024 * 1024


def _cparams(*sem):
    return pltpu.CompilerParams(dimension_semantics=sem, vmem_limit_bytes=V7X_VMEM_LIMIT)


def _tile(n, pref):
    t = min(n, pref)
    while n % t:
        t //= 2
    return t


def _layer_norm(v, g, b):
    mu = jnp.mean(v, axis=-1, keepdims=True)
    d = v - mu
    var = jnp.mean(d * d, axis=-1, keepdims=True)
    return d * lax.rsqrt(var + LN_EPS) * g + b


def _proj_kernel(x_ref, w_ref, cs_ref, o_ref, xb_ref):
    @pl.when(pl.program_id(1) == 0)
    def _():
        xb_ref[...] = x_ref[...].astype(BF16)

    acc = jnp.dot(xb_ref[...], w_ref[...], preferred_element_type=F32)
    o_ref[...] = (acc * cs_ref[...]).astype(o_ref.dtype)


def _proj(x, w, col_scale):
    t, d = x.shape
    n = w.shape[1]
    tm = _tile(t, 1024)
    tn = _tile(n, 2560)
    return pl.pallas_call(
        _proj_kernel,
        grid=(t // tm, n // tn),
        in_specs=[pl.BlockSpec((tm, d), lambda i, j: (i, 0)),
                  pl.BlockSpec((d, tn), lambda i, j: (0, j)),
                  pl.BlockSpec((1, tn), lambda i, j: (0, j))],
        out_specs=pl.BlockSpec((tm, tn), lambda i, j: (i, j)),
        out_shape=jax.ShapeDtypeStruct((t, n), BF16),
        scratch_shapes=[pltpu.VMEM((tm, d), BF16)],
        compiler_params=_cparams("parallel", "arbitrary"),
        name="proj",
    )(x, w, col_scale)


def _attn_kernel(slopes_ref, lam_ref, g_ref, q_ref, k_ref, v_ref, o_ref, bias_ref, *, lam_init, tq):
    h = pl.program_id(0)
    b = pl.program_id(1)
    qi = pl.program_id(2)
    seq = k_ref.shape[1]

    @pl.when(jnp.logical_and(b == 0, qi == 0))
    def _():
        r = lax.broadcasted_iota(jnp.int32, bias_ref.shape, 0)
        j = lax.broadcasted_iota(jnp.int32, bias_ref.shape, 1)
        bias_ref[...] = (-LOG2E * slopes_ref[h]) * jnp.abs(r - j + (seq - tq)).astype(F32)

    start = pl.multiple_of((pl.num_programs(2) - 1 - qi) * tq, tq)
    lf = lam_ref[...].astype(F32)
    lam_full = (jnp.exp(jnp.sum(lf[0:1] * lf[1:2], axis=1, keepdims=True))
                - jnp.exp(jnp.sum(lf[2:3] * lf[3:4], axis=1, keepdims=True)) + lam_init)
    q = q_ref[0]
    k = k_ref[0]
    v = v_ref[0]
    bias = bias_ref[:, pl.ds(start, seq)]
    lane = lax.broadcasted_iota(jnp.int32, q.shape, 1)
    zero = jnp.zeros_like(q)

    def softmax_pv(qm):
        s = lax.dot_general(qm, k, (((1,), (1,)), ((), ())), preferred_element_type=F32)
        m = jnp.max(s + bias, axis=1, keepdims=True)
        e = jnp.exp2(s + (bias - m))
        l = jnp.sum(e, axis=1, keepdims=True)
        return jnp.dot(e.astype(BF16), v, preferred_element_type=F32) * (1.0 / l)

    o = (softmax_pv(jnp.where(lane < ATT_HEAD_DIM, q, zero))
         - lam_full * softmax_pv(jnp.where(lane >= ATT_HEAD_DIM, q, zero)))
    o = o * lax.rsqrt(jnp.mean(o * o, axis=-1, keepdims=True) + RMS_EPS) * g_ref[...].astype(F32)
    o_ref[0] = (o * (1.0 - lam_init)).astype(o_ref.dtype)


def _attention(proj3, lam, subln_g, layer):
    bsz, seq, _ = proj3.shape
    tq = _tile(seq, max(ATT_MIN_Q_TILE, ATT_TILE_BUDGET // (seq * seq)))
    lam_init = 0.8 - 0.6 * math.exp(-0.3 * layer)
    slopes = jnp.asarray(2.0 ** (-8.0 * np.arange(1, ATT_HEADS + 1, dtype=np.float32) / ATT_HEADS))
    kern = functools.partial(_attn_kernel, lam_init=lam_init, tq=tq)
    return pl.pallas_call(
        kern,
        grid=(ATT_HEADS, bsz, seq // tq),
        in_specs=[pl.BlockSpec(memory_space=pltpu.SMEM),
                  pl.BlockSpec((4, ATT_HEAD_DIM), lambda h, b, i: (0, 0)),
                  pl.BlockSpec((1, HEAD_W), lambda h, b, i: (0, 0)),
                  pl.BlockSpec((1, tq, HEAD_W), lambda h, b, i: (b, i, h)),
                  pl.BlockSpec((1, seq, HEAD_W), lambda h, b, i: (b, 0, ATT_HEADS + h)),
                  pl.BlockSpec((1, seq, HEAD_W), lambda h, b, i: (b, 0, 2 * ATT_HEADS + h))],
        out_specs=pl.BlockSpec((1, tq, HEAD_W), lambda h, b, i: (b, i, h)),
        out_shape=jax.ShapeDtypeStruct((bsz, seq, ATT_HEADS * HEAD_W), BF16),
        scratch_shapes=[pltpu.VMEM((tq, 2 * seq - tq), F32)],
        compiler_params=_cparams("arbitrary", "arbitrary", "arbitrary"),
        name="attn",
    )(slopes, lam, subln_g.reshape(1, HEAD_W), proj3, proj3, proj3)


def _conv3(x, prev_row, next_row, w):
    rows = x.shape[0]
    row = lax.broadcasted_iota(jnp.int32, (rows, 1), 0)
    x_prev = jnp.where(row == 0, prev_row, pltpu.roll(x, 1, axis=0))
    x_next = jnp.where(row == rows - 1, next_row, pltpu.roll(x, rows - 1, axis=0))
    return w[0:1] * x_prev + w[1:2] * x + w[2:3] * x_next


def _conv_kernel(hy_ref, hyp_ref, hyn_ref, sc_ref, scp_ref, scn_ref, hw_ref, sw_ref,
                 u_ref, x0_ref, yc_ref):
    i = pl.program_id(1)
    last = pl.num_programs(1) - 1
    width = u_ref.shape[2]
    has_prev = jnp.where(i > 0, 1.0, 0.0)
    has_next = jnp.where(i < last, 1.0, 0.0)
    hsub = hyp_ref.shape[1]

    hy = hy_ref[0].astype(F32)
    hy_prev = hyp_ref[0].astype(F32)[hsub - 1:hsub] * has_prev
    hy_next = hyn_ref[0].astype(F32)[0:1] * has_next
    hyc = _conv3(hy, hy_prev, hy_next, hw_ref[...].astype(F32))
    x0_ref[0] = hyc[:, :width].astype(x0_ref.dtype)
    u_ref[0] = (hyc[:, 2 * width:] * hyc[:, width:2 * width]).astype(u_ref.dtype)

    def gated(blk):
        return blk[:, width:2 * width] * blk[:, 2 * width:]

    sc = sc_ref[0].astype(F32)
    cx_prev = gated(scp_ref[0].astype(F32)[hsub - 1:hsub]) * has_prev
    cx_next = gated(scn_ref[0].astype(F32)[0:1]) * has_next
    conv = _conv3(gated(sc), cx_prev, cx_next, sw_ref[...].astype(F32))
    yc_ref[0] = (sc[:, :width] * conv).astype(yc_ref.dtype)


def _convs(proj3, hy_w, sc_w, hy_col, sc_col):
    bsz, seq, _ = proj3.shape
    width = sc_w.shape[1]
    tl = _tile(seq, 1024)
    hs = V7X_BF16_SUBLANES
    nh = tl // hs
    nblk = seq // hs

    def main(col):
        return pl.BlockSpec((1, tl, 3 * width), lambda b, i: (b, i, col))

    def prev(col):
        return pl.BlockSpec((1, hs, 3 * width), lambda b, i: (b, jnp.maximum(i * nh - 1, 0), col))

    def nxt(col):
        return pl.BlockSpec((1, hs, 3 * width), lambda b, i: (b, jnp.minimum((i + 1) * nh, nblk - 1), col))

    out = jax.ShapeDtypeStruct((bsz, seq, width), BF16)
    ospec = pl.BlockSpec((1, tl, width), lambda b, i: (b, i, 0))
    return pl.pallas_call(
        _conv_kernel,
        grid=(bsz, seq // tl),
        in_specs=[main(hy_col), prev(hy_col), nxt(hy_col), main(sc_col), prev(sc_col), nxt(sc_col),
                  pl.BlockSpec((3, 3 * width), lambda b, i: (0, 0)),
                  pl.BlockSpec((3, width), lambda b, i: (0, 0))],
        out_specs=[ospec, ospec, ospec],
        out_shape=[out, out, out],
        compiler_params=_cparams("parallel", "arbitrary"),
        name="conv",
    )(proj3, proj3, proj3, proj3, proj3, proj3, hy_w, sc_w)


def _tab_kernel(c_ref, s_ref, st_ref, cb_ref, sb_ref, *, seq):
    tr = c_ref.shape[0]
    i = pl.program_id(0)
    dr = lax.broadcasted_iota(jnp.int32, (tr, seq), 0)
    c = lax.broadcasted_iota(jnp.int32, (tr, seq), 1)
    r = i * tr + dr

    def angle(prod):
        return (prod & (2 * seq - 1)).astype(F32) * (math.pi / seq)

    @pl.when(i == 0)
    def _():
        base = angle(dr * c)
        cb_ref[...] = jnp.cos(base)
        sb_ref[...] = jnp.sin(base)

    col = lax.broadcasted_iota(jnp.int32, (1, seq), 1)
    lead = angle((i * tr) * col)
    c0, s0 = jnp.cos(lead), jnp.sin(lead)
    cb, sb = cb_ref[...], sb_ref[...]
    cosv = c0 * cb - s0 * sb
    nsin = -(s0 * cb + c0 * sb)
    c_ref[...] = cosv.astype(c_ref.dtype)
    s_ref[...] = jnp.where(r == 0, jnp.where((c & 1) == 0, 1.0, -1.0), nsin).astype(s_ref.dtype)
    st_ref[...] = jnp.where(c == 0, jnp.where((r & 1) == 0, 1.0, -1.0), nsin).astype(st_ref.dtype)


def _dft_tables(seq):
    assert seq & (seq - 1) == 0, "sequence length must be a power of two"
    tr = _tile(seq, 256)
    spec = pl.BlockSpec((tr, seq), lambda i: (i, 0))
    out = jax.ShapeDtypeStruct((seq, seq), BF16)
    return pl.pallas_call(
        functools.partial(_tab_kernel, seq=seq),
        grid=(seq // tr,),
        out_specs=[spec, spec, spec],
        out_shape=[out, out, out],
        scratch_shapes=[pltpu.VMEM((tr, seq), F32), pltpu.VMEM((tr, seq), F32)],
        compiler_params=_cparams("arbitrary"),
        name="dft_tables",
    )()


def _filt_kernel(z_ref, win_ref, bin_ref, wmid_ref, bmid_ref, fr_ref, wout_ref, dl_ref,
                 hs_ref, l1_ref, *, seq):
    i = pl.program_id(0)
    tl = z_ref.shape[0]
    width = dl_ref.shape[1]
    hi = lax.Precision.HIGHEST
    fr = fr_ref[...]
    h = jnp.sin(fr * (jnp.dot(z_ref[...], win_ref[...], precision=hi, preferred_element_type=F32)
                      + bin_ref[...]))
    for j in range(wmid_ref.shape[0]):
        h = jnp.sin(fr * (jnp.dot(h, wmid_ref[j], precision=hi, preferred_element_type=F32)
                          + bmid_ref[j]))
    ho = jnp.dot(h, wout_ref[...], precision=hi, preferred_element_type=F32)
    pos = i * tl + lax.broadcasted_iota(jnp.int32, (tl, 1), 0)
    t = pos.astype(F32) / float(seq - 1)
    window = jnp.exp(-t * dl_ref[...])
    h_fwd = ho[:, :width] * window
    h_bwd = jnp.where(pos == 0, 0.0, ho[:, width:] * window)
    hs_ref[:, :width] = h_fwd + h_bwd
    hs_ref[:, width:] = h_fwd - h_bwd

    @pl.when(i == 0)
    def _():
        l1_ref[...] = jnp.zeros_like(l1_ref)

    l1_ref[...] += jnp.sum(jnp.abs(h_fwd) + jnp.abs(h_bwd), axis=0, keepdims=True)


def _fspec_kernel(c_ref, s_ref, hs_ref, l1_ref, a_ref, b_ref, d_ref, *, seq):
    i = pl.program_id(0)
    tk = c_ref.shape[0]
    width = l1_ref.shape[1]
    hs = hs_ref[...].astype(BF16)
    xc = jnp.dot(c_ref[...], hs[:, :width], preferred_element_type=F32)
    xs = jnp.dot(s_ref[...], hs, preferred_element_type=F32)
    k = i * tk + lax.broadcasted_iota(jnp.int32, (tk, 1), 0)
    inv_l1 = 1.0 / l1_ref[...]
    wk = jnp.where(k == 0, 1.0, 2.0) / float(2 * seq)
    a = xc * wk * inv_l1
    a_ref[...] = a
    b_ref[...] = jnp.where(k == 0, 0.0, xs[:, width:] * wk * inv_l1)
    d_ref[...] = jnp.where(k == 0, xs[:, :width] * wk * inv_l1, a)


def _hyena_filter_spectrum(seq, tabs, w_in, b_in, w_mid, b_mid, freq, w_out):
    c_tab, s_tab, _ = tabs
    emb, fw = w_in.shape
    width = w_out.shape[1] // 2
    n = np.arange(seq, dtype=np.float64)[:, None]
    f = np.linspace(1e-4, HY_BANDS - 1, HY_BANDS)[None]
    w = 2.0 * math.pi * n / seq
    z = np.concatenate([n / (seq - 1), np.cos(f * w), -np.sin(f * w)], -1)
    z = np.pad(z, ((0, 0), (0, V7X_LANES - emb))).astype(np.float32)
    w_in_p = jnp.pad(w_in.astype(F32), ((0, V7X_LANES - emb), (0, 0)))
    deltas = np.abs(np.linspace(math.log(HY_TARGET) / HY_SLOW_PCT, math.log(HY_TARGET) / HY_FAST_PCT,
                                width)).astype(np.float32)[None]
    tl = _tile(seq, 512)
    full = lambda *shape: pl.BlockSpec(shape, lambda i: (0,) * len(shape))
    hs, l1 = pl.pallas_call(
        functools.partial(_filt_kernel, seq=seq),
        grid=(seq // tl,),
        in_specs=[pl.BlockSpec((tl, V7X_LANES), lambda i: (i, 0)),
                  full(V7X_LANES, fw), full(1, fw), full(*w_mid.shape), full(w_mid.shape[0], 1, fw),
                  full(1, fw), full(fw, 2 * width), full(1, width)],
        out_specs=[pl.BlockSpec((tl, 2 * width), lambda i: (i, 0)), full(1, width)],
        out_shape=[jax.ShapeDtypeStruct((seq, 2 * width), F32), jax.ShapeDtypeStruct((1, width), F32)],
        compiler_params=_cparams("arbitrary"),
        name="hyena_filter",
    )(jnp.asarray(z), w_in_p, b_in.reshape(1, fw).astype(F32), w_mid.astype(F32),
      b_mid.reshape(w_mid.shape[0], 1, fw).astype(F32), freq.reshape(1, fw).astype(F32),
      w_out.astype(F32), jnp.asarray(deltas))

    tk = _tile(seq, 256)
    ospec = pl.BlockSpec((tk, width), lambda i: (i, 0))
    out = jax.ShapeDtypeStruct((seq, width), F32)
    return pl.pallas_call(
        functools.partial(_fspec_kernel, seq=seq),
        grid=(seq // tk,),
        in_specs=[pl.BlockSpec((tk, seq), lambda i: (i, 0)), pl.BlockSpec((tk, seq), lambda i: (i, 0)),
                  full(seq, 2 * width), full(1, width)],
        out_specs=[ospec, ospec, ospec],
        out_shape=[out, out, out],
        compiler_params=_cparams("parallel"),
        name="filter_spectrum",
    )(c_tab, s_tab, hs, l1)


def _dft_fwd_kernel(c_ref, s_ref, u_ref, a_ref, b_ref, d_ref, yre_ref, yim_ref):
    u = u_ref[0]
    x_re = jnp.dot(c_ref[...], u, preferred_element_type=F32)
    x_im = jnp.dot(s_ref[...], u, preferred_element_type=F32)
    b = b_ref[...]
    yre_ref[0] = (x_re * a_ref[...] - x_im * b).astype(yre_ref.dtype)
    yim_ref[0] = (x_re * b + x_im * d_ref[...]).astype(yim_ref.dtype)


def _dft_inv_kernel(c_ref, st_ref, yre_ref, yim_ref, u_ref, x0_ref, bias_ref, o_ref):
    y = (jnp.dot(c_ref[...], yre_ref[0], preferred_element_type=F32)
         + jnp.dot(st_ref[...], yim_ref[0], preferred_element_type=F32))
    y = y + u_ref[0].astype(F32) * bias_ref[...]
    o_ref[0] = (x0_ref[0].astype(F32) * y).astype(o_ref.dtype)


def _long_conv(u, x0, tabs, spec, bias):
    c_tab, s_tab, st_tab = tabs
    a, b, d = spec
    bsz, seq, width = u.shape
    tk = _tile(seq, 512)
    tab = pl.BlockSpec((tk, seq), lambda j, bb: (j, 0))
    whole = pl.BlockSpec((1, seq, width), lambda j, bb: (bb, 0, 0))
    rows = pl.BlockSpec((1, tk, width), lambda j, bb: (bb, j, 0))
    coef = pl.BlockSpec((tk, width), lambda j, bb: (j, 0))
    spec_out = jax.ShapeDtypeStruct((bsz, seq, width), BF16)
    y_re, y_im = pl.pallas_call(
        _dft_fwd_kernel,
        grid=(seq // tk, bsz),
        in_specs=[tab, tab, whole, coef, coef, coef],
        out_specs=[rows, rows],
        out_shape=[spec_out, spec_out],
        compiler_params=_cparams("parallel", "arbitrary"),
        name="dft_fwd",
    )(c_tab, s_tab, u, a, b, d)
    return pl.pallas_call(
        _dft_inv_kernel,
        grid=(seq // tk, bsz),
        in_specs=[tab, tab, whole, whole, rows, rows, pl.BlockSpec((1, width), lambda j, bb: (0, 0))],
        out_specs=rows,
        out_shape=spec_out,
        compiler_params=_cparams("parallel", "arbitrary"),
        name="dft_inv",
    )(c_tab, st_tab, y_re, y_im, u, x0, bias.reshape(1, width).astype(F32))


def _merge_kernel(ya_ref, yh_ref, yc_ref, g0_ref, g1_ref, x_ref, wb_ref, wo_ref, lg_ref, lb_ref,
                  o_ref, m_ref, *, alpha):
    d = o_ref.shape[1]
    half = d // 2
    g0 = g0_ref[...]
    g1 = g1_ref[...]
    gates = ((g0[:, :half], g0[:, d:], g1[:, half:d]),
             (g0[:, half:d], g1[:, :half], g1[:, d:]))
    branches = (ya_ref[...], yh_ref[...], yc_ref[...])
    for c in range(2):
        acc = None
        for j in range(3):
            y = jnp.dot(branches[j], wb_ref[j, :, c * half:(c + 1) * half], preferred_element_type=F32)
            term = jax.nn.sigmoid(gates[c][j].astype(F32)) * y
            acc = term if acc is None else acc + term
        m_ref[:, c * half:(c + 1) * half] = acc.astype(m_ref.dtype)
    t = jnp.dot(m_ref[...], wo_ref[...], preferred_element_type=F32)
    o_ref[...] = _layer_norm(alpha * x_ref[...] + t, lg_ref[...], lb_ref[...])


def _merge(y_a, y_h, y_c, proj, x, wb, wo, ln_g, ln_b, alpha, gate_col):
    t, d = x.shape
    bw = y_a.shape[1]
    tm = _tile(t, 512)
    row = lambda w: pl.BlockSpec((tm, w), lambda i: (i, 0))
    fixed = lambda *shape: pl.BlockSpec(shape, lambda i: (0,) * len(shape))
    gw = 3 * d // 2
    return pl.pallas_call(
        functools.partial(_merge_kernel, alpha=alpha),
        grid=(t // tm,),
        in_specs=[row(bw), row(bw), row(bw),
                  pl.BlockSpec((tm, gw), lambda i: (i, gate_col)),
                  pl.BlockSpec((tm, gw), lambda i: (i, gate_col + 1)),
                  row(d), fixed(*wb.shape), fixed(*wo.shape), fixed(1, d), fixed(1, d)],
        out_specs=row(d),
        out_shape=jax.ShapeDtypeStruct((t, d), F32),
        scratch_shapes=[pltpu.VMEM((tm, d), BF16)],
        compiler_params=_cparams("parallel"),
        name="merge",
    )(y_a, y_h, y_c, proj, proj, x, wb, wo, ln_g.reshape(1, d), ln_b.reshape(1, d))


def _swiglu_partial(xb, wg, wu, wd):
    g = jnp.dot(xb, wg, preferred_element_type=F32)
    u = jnp.dot(xb, wu, preferred_element_type=F32)
    return jnp.dot((jax.nn.silu(g) * u).astype(BF16), wd, preferred_element_type=F32)


def _ffn_kernel(x_ref, wg_ref, wu_ref, wd_ref, lg_ref, lb_ref, o_ref, xb_ref, acc_ref, *, alpha):
    f = pl.program_id(1)

    @pl.when(f == 0)
    def _():
        xb_ref[...] = x_ref[...].astype(BF16)
        acc_ref[...] = jnp.zeros_like(acc_ref)

    acc_ref[...] += _swiglu_partial(xb_ref[...], wg_ref[...], wu_ref[...], wd_ref[...])

    @pl.when(f == pl.num_programs(1) - 1)
    def _():
        o_ref[...] = _layer_norm(alpha * x_ref[...] + acc_ref[...], lg_ref[...], lb_ref[...])


def _ffn(x, wg, wu, wd, ln_g, ln_b, alpha):
    t, d = x.shape
    ff = wg.shape[1]
    tm = _tile(t, 1024)
    tf = ff // 2 if (ff // 2) % V7X_LANES == 0 else ff
    vec = pl.BlockSpec((1, d), lambda i, f: (0, 0))
    return pl.pallas_call(
        functools.partial(_ffn_kernel, alpha=alpha),
        grid=(t // tm, ff // tf),
        in_specs=[pl.BlockSpec((tm, d), lambda i, f: (i, 0)),
                  pl.BlockSpec((d, tf), lambda i, f: (0, f)),
                  pl.BlockSpec((d, tf), lambda i, f: (0, f)),
                  pl.BlockSpec((tf, d), lambda i, f: (f, 0)),
                  vec, vec],
        out_specs=pl.BlockSpec((tm, d), lambda i, f: (i, 0)),
        out_shape=jax.ShapeDtypeStruct((t, d), F32),
        scratch_shapes=[pltpu.VMEM((tm, d), BF16), pltpu.VMEM((tm, d), F32)],
        compiler_params=_cparams("parallel", "arbitrary"),
        name="ffn",
    )(x, wg, wu, wd, ln_g.reshape(1, d), ln_b.reshape(1, d))


def _top2(logits, n_experts):
    lane = lax.broadcasted_iota(jnp.int32, logits.shape, 1).astype(F32)
    neg = jnp.float32(-jnp.inf)
    sentinel = jnp.float32(n_experts)
    lg = jnp.where(lane < sentinel, logits, neg)
    m1 = jnp.max(lg, axis=1, keepdims=True)
    i1 = jnp.min(jnp.where(lg == m1, lane, sentinel), axis=1, keepdims=True)
    rest = jnp.where(lane == i1, neg, lg)
    m2 = jnp.max(rest, axis=1, keepdims=True)
    i2 = jnp.min(jnp.where(rest == m2, lane, sentinel), axis=1, keepdims=True)
    e2 = jnp.exp(m2 - m1)
    p1 = 1.0 / (1.0 + e2)
    p2 = e2 / (1.0 + e2)
    return lane == i1, lane == i2, i1, i2, p1, p2


ROUTE_COLS = 8
MOE_TILE_ROWS = 512
MOE_TOKEN_TILE = 512
ROW_DMA_UNROLL = 8


def _router_kernel(x_ref, r_ref, route_ref, cnt_ref, tri_ref, *, n_experts):
    i = pl.program_id(0)
    x = x_ref[...]
    tm = x.shape[0]

    @pl.when(i == 0)
    def _():
        cnt_ref[...] = jnp.zeros_like(cnt_ref)
        row = lax.broadcasted_iota(jnp.int32, (tm, tm), 0)
        col = lax.broadcasted_iota(jnp.int32, (tm, tm), 1)
        tri_ref[...] = jnp.where(col < row, 1.0, 0.0).astype(tri_ref.dtype)

    w = r_ref[...]
    x_hi, w_hi = x.astype(BF16), w.astype(BF16)
    x_lo, w_lo = (x - x_hi.astype(F32)).astype(BF16), (w - w_hi.astype(F32)).astype(BF16)
    logits = (jnp.dot(x_hi, w_hi, preferred_element_type=F32) + jnp.dot(x_lo, w_hi, preferred_element_type=F32)
              + jnp.dot(x_hi, w_lo, preferred_element_type=F32))
    sel1, sel2, i1, i2, p1, p2 = _top2(logits, n_experts)
    chosen = jnp.where(sel1, 1.0, jnp.where(sel2, 1.0, 0.0))
    before = jnp.dot(tri_ref[...], chosen.astype(tri_ref.dtype), preferred_element_type=F32) + cnt_ref[...]
    r1 = jnp.sum(jnp.where(sel1, before, 0.0), axis=1, keepdims=True)
    r2 = jnp.sum(jnp.where(sel2, before, 0.0), axis=1, keepdims=True)
    cnt_ref[...] += jnp.sum(chosen, axis=0, keepdims=True)
    lane = lax.broadcasted_iota(jnp.int32, logits.shape, 1)
    record = jnp.zeros_like(logits)
    for c, val in enumerate((i1, i2, p1, p2, r1, r2)):
        record = jnp.where(lane == c, val, record)
    route_ref[...] = record[:, :ROUTE_COLS]


def _moe_scatter_kernel(pos_ref, pad_ref, tail_ref, x_ref, xs_ref, zero_ref, sem, zsem):
    tm = x_ref.shape[0]
    tr = zero_ref.shape[0]

    @pl.when(pl.program_id(0) == 0)
    def _():
        zero_ref[...] = jnp.zeros_like(zero_ref)

        def pad_copy(first, r):
            return pltpu.make_async_copy(zero_ref.at[pl.ds(0, 1)], xs_ref.at[pl.ds(first + r, 1)], zsem)

        def tile_copy(t):
            rows = pl.ds(pl.multiple_of(t * tr, tr), tr)
            return pltpu.make_async_copy(zero_ref, xs_ref.at[rows], zsem)

        def each(copy_fn, method):
            for e in range(pad_ref.shape[0]):
                first, count = pad_ref[e, 0], pad_ref[e, 1]
                lax.fori_loop(0, count, lambda r, c: (getattr(copy_fn(first, r), method)(), c)[1], 0)

        each(pad_copy, "start")
        lax.fori_loop(tail_ref[0], tail_ref[1], lambda t, c: (tile_copy(t).start(), c)[1], 0)
        each(pad_copy, "wait")
        lax.fori_loop(tail_ref[0], tail_ref[1], lambda t, c: (tile_copy(t).wait(), c)[1], 0)

    def row_copy(r, k):
        slot = pos_ref[0, 0, k * tm + r]
        return pltpu.make_async_copy(x_ref.at[pl.ds(r, 1)], xs_ref.at[pl.ds(slot, 1)], sem)

    def start(r, carry):
        row_copy(r, 0).start()
        row_copy(r, 1).start()
        return carry

    def wait(r, carry):
        row_copy(r, 0).wait()
        row_copy(r, 1).wait()
        return carry

    lax.fori_loop(0, tm, start, 0, unroll=ROW_DMA_UNROLL)
    lax.fori_loop(0, tm, wait, 0, unroll=ROW_DMA_UNROLL)


def _gmm_kernel(te_ref, nu_ref, xs_ref, wg_ref, wu_ref, wd_ref, ys_ref, xb_ref, acc_ref):
    del te_ref
    j = pl.program_id(0)
    f = pl.program_id(1)

    @pl.when(jnp.logical_and(j >= nu_ref[0], f == 0))
    def _():
        ys_ref[...] = jnp.zeros_like(ys_ref)

    @pl.when(j < nu_ref[0])
    def _():
        @pl.when(f == 0)
        def _():
            xb_ref[...] = xs_ref[...].astype(BF16)
            acc_ref[...] = jnp.zeros_like(acc_ref)

        acc_ref[...] += _swiglu_partial(xb_ref[...], wg_ref[...], wu_ref[...], wd_ref[...])

        @pl.when(f == pl.num_programs(1) - 1)
        def _():
            ys_ref[...] = acc_ref[...]


def _moe_combine_kernel(pos_ref, route_ref, x_ref, lg_ref, lb_ref, ys_ref, o_ref, y1_ref, y2_ref, sem,
                        *, alpha):
    tm = x_ref.shape[0]

    def row_copy(r, k):
        slot = pos_ref[0, 0, k * tm + r]
        dst = (y1_ref, y2_ref)[k]
        return pltpu.make_async_copy(ys_ref.at[pl.ds(slot, 1)], dst.at[pl.ds(r, 1)], sem)

    def start(r, carry):
        row_copy(r, 0).start()
        row_copy(r, 1).start()
        return carry

    def wait(r, carry):
        row_copy(r, 0).wait()
        row_copy(r, 1).wait()
        return carry

    lax.fori_loop(0, tm, start, 0, unroll=ROW_DMA_UNROLL)
    lax.fori_loop(0, tm, wait, 0, unroll=ROW_DMA_UNROLL)
    route = route_ref[...]
    mixed = route[:, 2:3] * y1_ref[...] + route[:, 3:4] * y2_ref[...]
    o_ref[...] = _layer_norm(alpha * x_ref[...] + mixed, lg_ref[...], lb_ref[...])


def _moe(x, router, wg, wu, wd, ln_g, ln_b, alpha):
    t, d = x.shape
    n_experts, _, ff = wg.shape
    tm = _tile(t, MOE_TOKEN_TILE)
    tr = MOE_TILE_ROWS
    n_tok_tiles = t // tm

    router_p = jnp.pad(router.astype(F32), ((0, 0), (0, V7X_LANES - n_experts)))
    route, counts = pl.pallas_call(
        functools.partial(_router_kernel, n_experts=n_experts),
        grid=(n_tok_tiles,),
        in_specs=[pl.BlockSpec((tm, d), lambda i: (i, 0)),
                  pl.BlockSpec((d, V7X_LANES), lambda i: (0, 0))],
        out_specs=[pl.BlockSpec((tm, ROUTE_COLS), lambda i: (i, 0)),
                   pl.BlockSpec((1, V7X_LANES), lambda i: (0, 0))],
        out_shape=[jax.ShapeDtypeStruct((t, ROUTE_COLS), F32), jax.ShapeDtypeStruct((1, V7X_LANES), F32)],
        scratch_shapes=[pltpu.VMEM((tm, tm), BF16)],
        compiler_params=_cparams("arbitrary"),
        name="moe_router",
    )(x, router_p)

    cnt = counts[0, :n_experts].astype(jnp.int32)
    padded = (cnt + tr - 1) // tr * tr
    ends = jnp.cumsum(padded)
    starts = ends - padded
    expert_ids = jnp.arange(n_experts, dtype=jnp.int32)

    def slots(expert_col, rank_col):
        onehot = route[:, expert_col:expert_col + 1].astype(jnp.int32) == expert_ids[None]
        return jnp.sum(jnp.where(onehot, starts[None], 0), axis=1) + route[:, rank_col].astype(jnp.int32)

    pos = jnp.concatenate([slots(0, 4).reshape(n_tok_tiles, 1, tm), slots(1, 5).reshape(n_tok_tiles, 1, tm)],
                          axis=2)
    n_tiles = (TOP_K * t) // tr + n_experts
    n_used = (ends[-1] // tr).reshape(1)
    tile_start = jnp.arange(n_tiles, dtype=jnp.int32) * tr
    tile_expert = jnp.sum((tile_start[:, None] >= ends[None]).astype(jnp.int32), axis=1)
    last_expert = jnp.sum((ends[-1] - tr >= ends).astype(jnp.int32))
    tile_expert = jnp.where(tile_start < ends[-1], tile_expert, last_expert)

    pos_spec = pl.BlockSpec((1, 1, 2 * tm), lambda i: (i, 0, 0), memory_space=pltpu.SMEM)
    smem_spec = pl.BlockSpec(memory_space=pltpu.SMEM)
    any_spec = pl.BlockSpec(memory_space=pl.ANY)
    pad_rows = jnp.stack([starts + cnt, padded - cnt], axis=1)
    tail_tiles = jnp.concatenate([n_used, jnp.full((1,), n_tiles, jnp.int32)])
    xs = pl.pallas_call(
        _moe_scatter_kernel,
        grid=(n_tok_tiles,),
        in_specs=[pos_spec, smem_spec, smem_spec, pl.BlockSpec((tm, d), lambda i: (i, 0))],
        out_specs=any_spec,
        out_shape=jax.ShapeDtypeStruct((n_tiles * tr, d), F32),
        scratch_shapes=[pltpu.VMEM((tr, d), F32), pltpu.SemaphoreType.DMA(()), pltpu.SemaphoreType.DMA(())],
        compiler_params=_cparams("arbitrary"),
        name="moe_scatter",
    )(pos, pad_rows, tail_tiles, x)

    tf = ff // 2 if (ff // 2) % V7X_LANES == 0 else ff
    nf = ff // tf
    used = lambda j, nu: jnp.minimum(j, nu[0] - 1)
    fcol = lambda j, f, nu: jnp.where(j < nu[0], f, nf - 1)
    ys = pl.pallas_call(
        _gmm_kernel,
        grid_spec=pltpu.PrefetchScalarGridSpec(
            num_scalar_prefetch=2,
            grid=(n_tiles, nf),
            in_specs=[pl.BlockSpec((tr, d), lambda j, f, te, nu: (used(j, nu), 0)),
                      pl.BlockSpec((None, d, tf), lambda j, f, te, nu: (te[j], 0, fcol(j, f, nu))),
                      pl.BlockSpec((None, d, tf), lambda j, f, te, nu: (te[j], 0, fcol(j, f, nu))),
                      pl.BlockSpec((None, tf, d), lambda j, f, te, nu: (te[j], fcol(j, f, nu), 0))],
            out_specs=pl.BlockSpec((tr, d), lambda j, f, te, nu: (j, 0)),
            scratch_shapes=[pltpu.VMEM((tr, d), BF16), pltpu.VMEM((tr, d), F32)]),
        out_shape=jax.ShapeDtypeStruct((n_tiles * tr, d), F32),
        compiler_params=_cparams("arbitrary", "arbitrary"),
        name="moe_experts",
    )(tile_expert, n_used, xs, wg, wu, wd)

    vec = pl.BlockSpec((1, d), lambda i: (0, 0))
    return pl.pallas_call(
        functools.partial(_moe_combine_kernel, alpha=alpha),
        grid=(n_tok_tiles,),
        in_specs=[pos_spec, pl.BlockSpec((tm, ROUTE_COLS), lambda i: (i, 0)),
                  pl.BlockSpec((tm, d), lambda i: (i, 0)), vec, vec, any_spec],
        out_specs=pl.BlockSpec((tm, d), lambda i: (i, 0)),
        out_shape=jax.ShapeDtypeStruct((t, d), F32),
        scratch_shapes=[pltpu.VMEM((tm, d), F32), pltpu.VMEM((tm, d), F32), pltpu.SemaphoreType.DMA(())],
        compiler_params=_cparams("arbitrary"),
        name="moe_combine",
    )(pos, route, x, ln_g.reshape(1, d), ln_b.reshape(1, d), ys)


def _run_trunk(x3, p, tables):
    bsz, seq, d = x3.shape
    depth = p['w_in'].shape[0]
    alpha = (2 * depth) ** 0.25
    att_w = ATT_HEADS * HEAD_W
    width = p['sc_conv_w'].shape[2]
    gate_w = 3 * d // 2
    assert att_w % width == 0 and (3 * att_w + 6 * width) % gate_w == 0, "column blocks must align"
    hy_col = att_w // width
    d_in = p['w_in'].shape[2]
    col_scale = jnp.asarray(np.where(np.arange(d_in) < att_w, LOG2E * ATT_HEAD_DIM ** -0.5, 1.0)
                            .astype(np.float32)[None])
    x = x3.reshape(bsz * seq, d)
    for i in range(depth):
        proj = _proj(x, p['w_in'][i], col_scale)
        proj3 = proj.reshape(bsz, seq, proj.shape[1])
        y_a = _attention(proj3, p['att_lambda'][i], p['att_subln_g'][i], i)
        u, x0, y_c = _convs(proj3, p['hy_short_w'][i], p['sc_conv_w'][i], hy_col=hy_col, sc_col=hy_col + 1)
        spec = _hyena_filter_spectrum(seq, tables, p['hf_w_in'][i], p['hf_b_in'][i], p['hf_w_mid'][i],
                                      p['hf_b_mid'][i], p['hf_freq'][i], p['hf_w_out'][i])
        y_h = _long_conv(u, x0, tables, spec, p['hy_bias'][i])
        flat = lambda a: a.reshape(bsz * seq, a.shape[2])
        j = i // 2
        x = _merge(flat(y_a), flat(y_h), flat(y_c), proj, x, p['w_branch'][i], p['w_out'][i],
                   p['ln1_g'][i], p['ln1_b'][i], alpha, gate_col=(3 * att_w + 6 * width) // gate_w)
        if i % 2 == 0:
            x = _ffn(x, p['ffn_w_gate'][j], p['ffn_w_up'][j], p['ffn_w_down'][j],
                     p['ln2_g'][i], p['ln2_b'][i], alpha)
        else:
            x = _moe(x, p['router_w'][j], p['moe_w_gate'][j], p['moe_w_up'][j], p['moe_w_down'][j],
                     p['ln2_g'][i], p['ln2_b'][i], alpha)
    return x.reshape(bsz, seq, d)


def kernel(x_prompt, x_sample, w_in, att_lambda, att_subln_g, hy_short_w, hf_w_in, hf_b_in,
           hf_w_mid, hf_b_mid, hf_freq, hf_w_out, hy_bias, sc_conv_w, w_branch, w_out,
           ln1_g, ln1_b, ln2_g, ln2_b, ffn_w_gate, ffn_w_up, ffn_w_down, router_w,
           moe_w_gate, moe_w_up, moe_w_down):
    bf = lambda a: a.astype(BF16)
    p = dict(w_in=bf(w_in), att_lambda=att_lambda, att_subln_g=att_subln_g, hy_short_w=hy_short_w,
             hf_w_in=hf_w_in, hf_b_in=hf_b_in, hf_w_mid=hf_w_mid, hf_b_mid=hf_b_mid,
             hf_freq=hf_freq, hf_w_out=hf_w_out, hy_bias=hy_bias, sc_conv_w=sc_conv_w,
             w_branch=bf(w_branch), w_out=bf(w_out), ln1_g=ln1_g, ln1_b=ln1_b, ln2_g=ln2_g, ln2_b=ln2_b,
             ffn_w_gate=bf(ffn_w_gate), ffn_w_up=bf(ffn_w_up), ffn_w_down=bf(ffn_w_down),
             router_w=router_w, moe_w_gate=bf(moe_w_gate), moe_w_up=bf(moe_w_up),
             moe_w_down=bf(moe_w_down))
    tables = {}
    outs = []
    for x3 in (x_prompt, x_sample):
        seq = x3.shape[1]
        if seq not in tables:
            tables[seq] = _dft_tables(seq)
        outs.append(_run_trunk(x3, p, tables[seq]))
    return tuple(outs)
```

```python
import functools
import math

import numpy as np
import jax
import jax.numpy as jnp
from jax import lax
from jax.experimental import pallas as pl
from jax.experimental.pallas import tpu as pltpu

F32 = jnp.float32
BF16 = jnp.bfloat16

ATT_HEADS = 4
ATT_HEAD_DIM = 64
HEAD_W = 2 * ATT_HEAD_DIM
HY_BANDS = 16
HY_TARGET = 1e-2
HY_FAST_PCT = 0.3
HY_SLOW_PCT = 1.5
TOP_K = 2
LN_EPS = 1e-5
RMS_EPS = 1e-5
LOG2E = math.log2(math.e)
ATT_MIN_Q_TILE = 256
ATT_TILE_BUDGET = 1 << 32
V7X_LANES = 128
V7X_BF16_SUBLANES = 16
V7X_VMEM_LIMIT = 56 * 1024 * 1024


def _cparams(*sem):
    return pltpu.CompilerParams(dimension_semantics=sem, vmem_limit_bytes=V7X_VMEM_LIMIT)


def _tile(n, pref):
    t = min(n, pref)
    while n % t:
        t //= 2
    return t


def _layer_norm(v, g, b):
    mu = jnp.mean(v, axis=-1, keepdims=True)
    d = v - mu
    var = jnp.mean(d * d, axis=-1, keepdims=True)
    return d * lax.rsqrt(var + LN_EPS) * g + b


def _proj_kernel(x_ref, w_ref, cs_ref, o_ref, xb_ref):
    @pl.when(pl.program_id(1) == 0)
    def _():
        xb_ref[...] = x_ref[...].astype(BF16)

    acc = jnp.dot(xb_ref[...], w_ref[...], preferred_element_type=F32)
    o_ref[...] = (acc * cs_ref[...]).astype(o_ref.dtype)


def _proj(x, w, col_scale):
    t, d = x.shape
    n = w.shape[1]
    tm = _tile(t, 1024)
    tn = _tile(n, 2560)
    return pl.pallas_call(
        _proj_kernel,
        grid=(t // tm, n // tn),
        in_specs=[pl.BlockSpec((tm, d), lambda i, j: (i, 0)),
                  pl.BlockSpec((d, tn), lambda i, j: (0, j)),
                  pl.BlockSpec((1, tn), lambda i, j: (0, j))],
        out_specs=pl.BlockSpec((tm, tn), lambda i, j: (i, j)),
        out_shape=jax.ShapeDtypeStruct((t, n), BF16),
        scratch_shapes=[pltpu.VMEM((tm, d), BF16)],
        compiler_params=_cparams("parallel", "arbitrary"),
        name="proj",
    )(x, w, col_scale)


def _attn_kernel(slopes_ref, lam_ref, g_ref, q_ref, k_ref, v_ref, o_ref, bias_ref, *, lam_init, tq):
    h = pl.program_id(0)
    b = pl.program_id(1)
    qi = pl.program_id(2)
    seq = k_ref.shape[1]

    @pl.when(jnp.logical_and(b == 0, qi == 0))
    def _():
        r = lax.broadcasted_iota(jnp.int32, bias_ref.shape, 0)
        j = lax.broadcasted_iota(jnp.int32, bias_ref.shape, 1)
        bias_ref[...] = (-LOG2E * slopes_ref[h]) * jnp.abs(r - j + (seq - tq)).astype(F32)

    start = pl.multiple_of((pl.num_programs(2) - 1 - qi) * tq, tq)
    lf = lam_ref[...].astype(F32)
    lam_full = (jnp.exp(jnp.sum(lf[0:1] * lf[1:2], axis=1, keepdims=True))
                - jnp.exp(jnp.sum(lf[2:3] * lf[3:4], axis=1, keepdims=True)) + lam_init)
    q = q_ref[0]
    k = k_ref[0]
    v = v_ref[0]
    bias = bias_ref[:, pl.ds(start, seq)]
    lane = lax.broadcasted_iota(jnp.int32, q.shape, 1)
    zero = jnp.zeros_like(q)

    def softmax_pv(qm):
        s = lax.dot_general(qm, k, (((1,), (1,)), ((), ())), preferred_element_type=F32)
        m = jnp.max(s + bias, axis=1, keepdims=True)
        e = jnp.exp2(s + (bias - m))
        l = jnp.sum(e, axis=1, keepdims=True)
        return jnp.dot(e.astype(BF16), v, preferred_element_type=F32) * (1.0 / l)

    o = (softmax_pv(jnp.where(lane < ATT_HEAD_DIM, q, zero))
         - lam_full * softmax_pv(jnp.where(lane >= ATT_HEAD_DIM, q, zero)))
    o = o * lax.rsqrt(jnp.mean(o * o, axis=-1, keepdims=True) + RMS_EPS) * g_ref[...].astype(F32)
    o_ref[0] = (o * (1.0 - lam_init)).astype(o_ref.dtype)


def _attention(proj3, lam, subln_g, layer):
    bsz, seq, _ = proj3.shape
    tq = _tile(seq, max(ATT_MIN_Q_TILE, ATT_TILE_BUDGET // (seq * seq)))
    lam_init = 0.8 - 0.6 * math.exp(-0.3 * layer)
    slopes = jnp.asarray(2.0 ** (-8.0 * np.arange(1, ATT_HEADS + 1, dtype=np.float32) / ATT_HEADS))
    kern = functools.partial(_attn_kernel, lam_init=lam_init, tq=tq)
    return pl.pallas_call(
        kern,
        grid=(ATT_HEADS, bsz, seq // tq),
        in_specs=[pl.BlockSpec(memory_space=pltpu.SMEM),
                  pl.BlockSpec((4, ATT_HEAD_DIM), lambda h, b, i: (0, 0)),
                  pl.BlockSpec((1, HEAD_W), lambda h, b, i: (0, 0)),
                  pl.BlockSpec((1, tq, HEAD_W), lambda h, b, i: (b, i, h)),
                  pl.BlockSpec((1, seq, HEAD_W), lambda h, b, i: (b, 0, ATT_HEADS + h)),
                  pl.BlockSpec((1, seq, HEAD_W), lambda h, b, i: (b, 0, 2 * ATT_HEADS + h))],
        out_specs=pl.BlockSpec((1, tq, HEAD_W), lambda h, b, i: (b, i, h)),
        out_shape=jax.ShapeDtypeStruct((bsz, seq, ATT_HEADS * HEAD_W), BF16),
        scratch_shapes=[pltpu.VMEM((tq, 2 * seq - tq), F32)],
        compiler_params=_cparams("arbitrary", "arbitrary", "arbitrary"),
        name="attn",
    )(slopes, lam, subln_g.reshape(1, HEAD_W), proj3, proj3, proj3)


def _conv3(x, prev_row, next_row, w):
    rows = x.shape[0]
    row = lax.broadcasted_iota(jnp.int32, (rows, 1), 0)
    x_prev = jnp.where(row == 0, prev_row, pltpu.roll(x, 1, axis=0))
    x_next = jnp.where(row == rows - 1, next_row, pltpu.roll(x, rows - 1, axis=0))
    return w[0:1] * x_prev + w[1:2] * x + w[2:3] * x_next


def _conv_kernel(hy_ref, hyp_ref, hyn_ref, sc_ref, scp_ref, scn_ref, hw_ref, sw_ref,
                 u_ref, x0_ref, yc_ref):
    i = pl.program_id(1)
    last = pl.num_programs(1) - 1
    width = u_ref.shape[2]
    has_prev = jnp.where(i > 0, 1.0, 0.0)
    has_next = jnp.where(i < last, 1.0, 0.0)
    hsub = hyp_ref.shape[1]

    hy = hy_ref[0].astype(F32)
    hy_prev = hyp_ref[0].astype(F32)[hsub - 1:hsub] * has_prev
    hy_next = hyn_ref[0].astype(F32)[0:1] * has_next
    hyc = _conv3(hy, hy_prev, hy_next, hw_ref[...].astype(F32))
    x0_ref[0] = hyc[:, :width].astype(x0_ref.dtype)
    u_ref[0] = (hyc[:, 2 * width:] * hyc[:, width:2 * width]).astype(u_ref.dtype)

    def gated(blk):
        return blk[:, width:2 * width] * blk[:, 2 * width:]

    sc = sc_ref[0].astype(F32)
    cx_prev = gated(scp_ref[0].astype(F32)[hsub - 1:hsub]) * has_prev
    cx_next = gated(scn_ref[0].astype(F32)[0:1]) * has_next
    conv = _conv3(gated(sc), cx_prev, cx_next, sw_ref[...].astype(F32))
    yc_ref[0] = (sc[:, :width] * conv).astype(yc_ref.dtype)


def _convs(proj3, hy_w, sc_w, hy_col, sc_col):
    bsz, seq, _ = proj3.shape
    width = sc_w.shape[1]
    tl = _tile(seq, 1024)
    hs = V7X_BF16_SUBLANES
    nh = tl // hs
    nblk = seq // hs

    def main(col):
        return pl.BlockSpec((1, tl, 3 * width), lambda b, i: (b, i, col))

    def prev(col):
        return pl.BlockSpec((1, hs, 3 * width), lambda b, i: (b, jnp.maximum(i * nh - 1, 0), col))

    def nxt(col):
        return pl.BlockSpec((1, hs, 3 * width), lambda b, i: (b, jnp.minimum((i + 1) * nh, nblk - 1), col))

    out = jax.ShapeDtypeStruct((bsz, seq, width), BF16)
    ospec = pl.BlockSpec((1, tl, width), lambda b, i: (b, i, 0))
    return pl.pallas_call(
        _conv_kernel,
        grid=(bsz, seq // tl),
        in_specs=[main(hy_col), prev(hy_col), nxt(hy_col), main(sc_col), prev(sc_col), nxt(sc_col),
                  pl.BlockSpec((3, 3 * width), lambda b, i: (0, 0)),
                  pl.BlockSpec((3, width), lambda b, i: (0, 0))],
        out_specs=[ospec, ospec, ospec],
        out_shape=[out, out, out],
        compiler_params=_cparams("parallel", "arbitrary"),
        name="conv",
    )(proj3, proj3, proj3, proj3, proj3, proj3, hy_w, sc_w)


def _tab_kernel(c_ref, s_ref, st_ref, cb_ref, sb_ref, *, seq):
    tr = c_ref.shape[0]
    i = pl.program_id(0)
    dr = lax.broadcasted_iota(jnp.int32, (tr, seq), 0)
    c = lax.broadcasted_iota(jnp.int32, (tr, seq), 1)
    r = i * tr + dr

    def angle(prod):
        return (prod & (2 * seq - 1)).astype(F32) * (math.pi / seq)

    @pl.when(i == 0)
    def _():
        base = angle(dr * c)
        cb_ref[...] = jnp.cos(base)
        sb_ref[...] = jnp.sin(base)

    col = lax.broadcasted_iota(jnp.int32, (1, seq), 1)
    lead = angle((i * tr) * col)
    c0, s0 = jnp.cos(lead), jnp.sin(lead)
    cb, sb = cb_ref[...], sb_ref[...]
    cosv = c0 * cb - s0 * sb
    nsin = -(s0 * cb + c0 * sb)
    c_ref[...] = cosv.astype(c_ref.dtype)
    s_ref[...] = jnp.where(r == 0, jnp.where((c & 1) == 0, 1.0, -1.0), nsin).astype(s_ref.dtype)
    st_ref[...] = jnp.where(c == 0, jnp.where((r & 1) == 0, 1.0, -1.0), nsin).astype(st_ref.dtype)


def _dft_tables(seq):
    assert seq & (seq - 1) == 0, "sequence length must be a power of two"
    tr = _tile(seq, 256)
    spec = pl.BlockSpec((tr, seq), lambda i: (i, 0))
    out = jax.ShapeDtypeStruct((seq, seq), BF16)
    return pl.pallas_call(
        functools.partial(_tab_kernel, seq=seq),
        grid=(seq // tr,),
        out_specs=[spec, spec, spec],
        out_shape=[out, out, out],
        scratch_shapes=[pltpu.VMEM((tr, seq), F32), pltpu.VMEM((tr, seq), F32)],
        compiler_params=_cparams("arbitrary"),
        name="dft_tables",
    )()


def _filt_kernel(z_ref, win_ref, bin_ref, wmid_ref, bmid_ref, fr_ref, wout_ref, dl_ref,
                 hs_ref, l1_ref, *, seq):
    i = pl.program_id(0)
    tl = z_ref.shape[0]
    width = dl_ref.shape[1]
    hi = lax.Precision.HIGHEST
    fr = fr_ref[...]
    h = jnp.sin(fr * (jnp.dot(z_ref[...], win_ref[...], precision=hi, preferred_element_type=F32)
                      + bin_ref[...]))
    for j in range(wmid_ref.shape[0]):
        h = jnp.sin(fr * (jnp.dot(h, wmid_ref[j], precision=hi, preferred_element_type=F32)
                          + bmid_ref[j]))
    ho = jnp.dot(h, wout_ref[...], precision=hi, preferred_element_type=F32)
    pos = i * tl + lax.broadcasted_iota(jnp.int32, (tl, 1), 0)
    t = pos.astype(F32) / float(seq - 1)
    window = jnp.exp(-t * dl_ref[...])
    h_fwd = ho[:, :width] * window
    h_bwd = jnp.where(pos == 0, 0.0, ho[:, width:] * window)
    hs_ref[:, :width] = h_fwd + h_bwd
    hs_ref[:, width:] = h_fwd - h_bwd

    @pl.when(i == 0)
    def _():
        l1_ref[...] = jnp.zeros_like(l1_ref)

    l1_ref[...] += jnp.sum(jnp.abs(h_fwd) + jnp.abs(h_bwd), axis=0, keepdims=True)


def _fspec_kernel(c_ref, s_ref, hs_ref, l1_ref, a_ref, b_ref, d_ref, *, seq):
    i = pl.program_id(0)
    tk = c_ref.shape[0]
    width = l1_ref.shape[1]
    hs = hs_ref[...].astype(BF16)
    xc = jnp.dot(c_ref[...], hs[:, :width], preferred_element_type=F32)
    xs = jnp.dot(s_ref[...], hs, preferred_element_type=F32)
    k = i * tk + lax.broadcasted_iota(jnp.int32, (tk, 1), 0)
    inv_l1 = 1.0 / l1_ref[...]
    wk = jnp.where(k == 0, 1.0, 2.0) / float(2 * seq)
    a = xc * wk * inv_l1
    a_ref[...] = a
    b_ref[...] = jnp.where(k == 0, 0.0, xs[:, width:] * wk * inv_l1)
    d_ref[...] = jnp.where(k == 0, xs[:, :width] * wk * inv_l1, a)


def _hyena_filter_spectrum(seq, tabs, w_in, b_in, w_mid, b_mid, freq, w_out):
    c_tab, s_tab, _ = tabs
    emb, fw = w_in.shape
    width = w_out.shape[1] // 2
    n = np.arange(seq, dtype=np.float64)[:, None]
    f = np.linspace(1e-4, HY_BANDS - 1, HY_BANDS)[None]
    w = 2.0 * math.pi * n / seq
    z = np.concatenate([n / (seq - 1), np.cos(f * w), -np.sin(f * w)], -1)
    z = np.pad(z, ((0, 0), (0, V7X_LANES - emb))).astype(np.float32)
    w_in_p = jnp.pad(w_in.astype(F32), ((0, V7X_LANES - emb), (0, 0)))
    deltas = np.abs(np.linspace(math.log(HY_TARGET) / HY_SLOW_PCT, math.log(HY_TARGET) / HY_FAST_PCT,
                                width)).astype(np.float32)[None]
    tl = _tile(seq, 512)
    full = lambda *shape: pl.BlockSpec(shape, lambda i: (0,) * len(shape))
    hs, l1 = pl.pallas_call(
        functools.partial(_filt_kernel, seq=seq),
        grid=(seq // tl,),
        in_specs=[pl.BlockSpec((tl, V7X_LANES), lambda i: (i, 0)),
                  full(V7X_LANES, fw), full(1, fw), full(*w_mid.shape), full(w_mid.shape[0], 1, fw),
                  full(1, fw), full(fw, 2 * width), full(1, width)],
        out_specs=[pl.BlockSpec((tl, 2 * width), lambda i: (i, 0)), full(1, width)],
        out_shape=[jax.ShapeDtypeStruct((seq, 2 * width), F32), jax.ShapeDtypeStruct((1, width), F32)],
        compiler_params=_cparams("arbitrary"),
        name="hyena_filter",
    )(jnp.asarray(z), w_in_p, b_in.reshape(1, fw).astype(F32), w_mid.astype(F32),
      b_mid.reshape(w_mid.shape[0], 1, fw).astype(F32), freq.reshape(1, fw).astype(F32),
      w_out.astype(F32), jnp.asarray(deltas))

    tk = _tile(seq, 256)
    ospec = pl.BlockSpec((tk, width), lambda i: (i, 0))
    out = jax.ShapeDtypeStruct((seq, width), F32)
    return pl.pallas_call(
        functools.partial(_fspec_kernel, seq=seq),
        grid=(seq // tk,),
        in_specs=[pl.BlockSpec((tk, seq), lambda i: (i, 0)), pl.BlockSpec((tk, seq), lambda i: (i, 0)),
                  full(seq, 2 * width), full(1, width)],
        out_specs=[ospec, ospec, ospec],
        out_shape=[out, out, out],
        compiler_params=_cparams("parallel"),
        name="filter_spectrum",
    )(c_tab, s_tab, hs, l1)


def _dft_fwd_kernel(c_ref, s_ref, u_ref, a_ref, b_ref, d_ref, yre_ref, yim_ref):
    u = u_ref[0]
    x_re = jnp.dot(c_ref[...], u, preferred_element_type=F32)
    x_im = jnp.dot(s_ref[...], u, preferred_element_type=F32)
    b = b_ref[...]
    yre_ref[0] = (x_re * a_ref[...] - x_im * b).astype(yre_ref.dtype)
    yim_ref[0] = (x_re * b + x_im * d_ref[...]).astype(yim_ref.dtype)


def _dft_inv_kernel(c_ref, st_ref, yre_ref, yim_ref, u_ref, x0_ref, bias_ref, o_ref):
    y = (jnp.dot(c_ref[...], yre_ref[0], preferred_element_type=F32)
         + jnp.dot(st_ref[...], yim_ref[0], preferred_element_type=F32))
    y = y + u_ref[0].astype(F32) * bias_ref[...]
    o_ref[0] = (x0_ref[0].astype(F32) * y).astype(o_ref.dtype)


def _long_conv(u, x0, tabs, spec, bias):
    c_tab, s_tab, st_tab = tabs
    a, b, d = spec
    bsz, seq, width = u.shape
    tk = _tile(seq, 512)
    tab = pl.BlockSpec((tk, seq), lambda j, bb: (j, 0))
    whole = pl.BlockSpec((1, seq, width), lambda j, bb: (bb, 0, 0))
    rows = pl.BlockSpec((1, tk, width), lambda j, bb: (bb, j, 0))
    coef = pl.BlockSpec((tk, width), lambda j, bb: (j, 0))
    spec_out = jax.ShapeDtypeStruct((bsz, seq, width), BF16)
    y_re, y_im = pl.pallas_call(
        _dft_fwd_kernel,
        grid=(seq // tk, bsz),
        in_specs=[tab, tab, whole, coef, coef, coef],
        out_specs=[rows, rows],
        out_shape=[spec_out, spec_out],
        compiler_params=_cparams("parallel", "arbitrary"),
        name="dft_fwd",
    )(c_tab, s_tab, u, a, b, d)
    return pl.pallas_call(
        _dft_inv_kernel,
        grid=(seq // tk, bsz),
        in_specs=[tab, tab, whole, whole, rows, rows, pl.BlockSpec((1, width), lambda j, bb: (0, 0))],
        out_specs=rows,
        out_shape=spec_out,
        compiler_params=_cparams("parallel", "arbitrary"),
        name="dft_inv",
    )(c_tab, st_tab, y_re, y_im, u, x0, bias.reshape(1, width).astype(F32))


def _merge_kernel(ya_ref, yh_ref, yc_ref, g0_ref, g1_ref, x_ref, wb_ref, wo_ref, lg_ref, lb_ref,
                  o_ref, m_ref, *, alpha):
    d = o_ref.shape[1]
    half = d // 2
    g0 = g0_ref[...]
    g1 = g1_ref[...]
    gates = ((g0[:, :half], g0[:, d:], g1[:, half:d]),
             (g0[:, half:d], g1[:, :half], g1[:, d:]))
    branches = (ya_ref[...], yh_ref[...], yc_ref[...])
    for c in range(2):
        acc = None
        for j in range(3):
            y = jnp.dot(branches[j], wb_ref[j, :, c * half:(c + 1) * half], preferred_element_type=F32)
            term = jax.nn.sigmoid(gates[c][j].astype(F32)) * y
            acc = term if acc is None else acc + term
        m_ref[:, c * half:(c + 1) * half] = acc.astype(m_ref.dtype)
    t = jnp.dot(m_ref[...], wo_ref[...], preferred_element_type=F32)
    o_ref[...] = _layer_norm(alpha * x_ref[...] + t, lg_ref[...], lb_ref[...])


def _merge(y_a, y_h, y_c, proj, x, wb, wo, ln_g, ln_b, alpha, gate_col):
    t, d = x.shape
    bw = y_a.shape[1]
    tm = _tile(t, 512)
    row = lambda w: pl.BlockSpec((tm, w), lambda i: (i, 0))
    fixed = lambda *shape: pl.BlockSpec(shape, lambda i: (0,) * len(shape))
    gw = 3 * d // 2
    return pl.pallas_call(
        functools.partial(_merge_kernel, alpha=alpha),
        grid=(t // tm,),
        in_specs=[row(bw), row(bw), row(bw),
                  pl.BlockSpec((tm, gw), lambda i: (i, gate_col)),
                  pl.BlockSpec((tm, gw), lambda i: (i, gate_col + 1)),
                  row(d), fixed(*wb.shape), fixed(*wo.shape), fixed(1, d), fixed(1, d)],
        out_specs=row(d),
        out_shape=jax.ShapeDtypeStruct((t, d), F32),
        scratch_shapes=[pltpu.VMEM((tm, d), BF16)],
        compiler_params=_cparams("parallel"),
        name="merge",
    )(y_a, y_h, y_c, proj, proj, x, wb, wo, ln_g.reshape(1, d), ln_b.reshape(1, d))


def _swiglu_partial(xb, wg, wu, wd):
    g = jnp.dot(xb, wg, preferred_element_type=F32)
    u = jnp.dot(xb, wu, preferred_element_type=F32)
    return jnp.dot((jax.nn.silu(g) * u).astype(BF16), wd, preferred_element_type=F32)


def _ffn_kernel(x_ref, wg_ref, wu_ref, wd_ref, lg_ref, lb_ref, o_ref, xb_ref, acc_ref, *, alpha):
    f = pl.program_id(1)

    @pl.when(f == 0)
    def _():
        xb_ref[...] = x_ref[...].astype(BF16)
        acc_ref[...] = jnp.zeros_like(acc_ref)

    acc_ref[...] += _swiglu_partial(xb_ref[...], wg_ref[...], wu_ref[...], wd_ref[...])

    @pl.when(f == pl.num_programs(1) - 1)
    def _():
        o_ref[...] = _layer_norm(alpha * x_ref[...] + acc_ref[...], lg_ref[...], lb_ref[...])


def _ffn(x, wg, wu, wd, ln_g, ln_b, alpha):
    t, d = x.shape
    ff = wg.shape[1]
    tm = _tile(t, 512)
    tf = ff // 2 if (ff // 2) % V7X_LANES == 0 else ff
    vec = pl.BlockSpec((1, d), lambda i, f: (0, 0))
    return pl.pallas_call(
        functools.partial(_ffn_kernel, alpha=alpha),
        grid=(t // tm, ff // tf),
        in_specs=[pl.BlockSpec((tm, d), lambda i, f: (i, 0)),
                  pl.BlockSpec((d, tf), lambda i, f: (0, f)),
                  pl.BlockSpec((d, tf), lambda i, f: (0, f)),
                  pl.BlockSpec((tf, d), lambda i, f: (f, 0)),
                  vec, vec],
        out_specs=pl.BlockSpec((tm, d), lambda i, f: (i, 0)),
        out_shape=jax.ShapeDtypeStruct((t, d), F32),
        scratch_shapes=[pltpu.VMEM((tm, d), BF16), pltpu.VMEM((tm, d), F32)],
        compiler_params=_cparams("parallel", "arbitrary"),
        name="ffn",
    )(x, wg, wu, wd, ln_g.reshape(1, d), ln_b.reshape(1, d))


def _top2(logits, n_experts):
    lane = lax.broadcasted_iota(jnp.int32, logits.shape, 1).astype(F32)
    neg = jnp.float32(-jnp.inf)
    sentinel = jnp.float32(n_experts)
    lg = jnp.where(lane < sentinel, logits, neg)
    m1 = jnp.max(lg, axis=1, keepdims=True)
    i1 = jnp.min(jnp.where(lg == m1, lane, sentinel), axis=1, keepdims=True)
    rest = jnp.where(lane == i1, neg, lg)
    m2 = jnp.max(rest, axis=1, keepdims=True)
    i2 = jnp.min(jnp.where(rest == m2, lane, sentinel), axis=1, keepdims=True)
    e2 = jnp.exp(m2 - m1)
    p1 = 1.0 / (1.0 + e2)
    p2 = e2 / (1.0 + e2)
    return lane == i1, lane == i2, i1, i2, p1, p2


ROUTE_COLS = 8
MOE_TILE_ROWS = 512
MOE_TOKEN_TILE = 512
ROW_DMA_UNROLL = 8


def _router_kernel(x_ref, r_ref, route_ref, cnt_ref, tri_ref, *, n_experts):
    i = pl.program_id(0)
    x = x_ref[...]
    tm = x.shape[0]

    @pl.when(i == 0)
    def _():
        cnt_ref[...] = jnp.zeros_like(cnt_ref)
        row = lax.broadcasted_iota(jnp.int32, (tm, tm), 0)
        col = lax.broadcasted_iota(jnp.int32, (tm, tm), 1)
        tri_ref[...] = jnp.where(col < row, 1.0, 0.0).astype(tri_ref.dtype)

    w = r_ref[...]
    x_hi, w_hi = x.astype(BF16), w.astype(BF16)
    x_lo, w_lo = (x - x_hi.astype(F32)).astype(BF16), (w - w_hi.astype(F32)).astype(BF16)
    logits = (jnp.dot(x_hi, w_hi, preferred_element_type=F32) + jnp.dot(x_lo, w_hi, preferred_element_type=F32)
              + jnp.dot(x_hi, w_lo, preferred_element_type=F32))
    sel1, sel2, i1, i2, p1, p2 = _top2(logits, n_experts)
    chosen = jnp.where(sel1, 1.0, jnp.where(sel2, 1.0, 0.0))
    before = jnp.dot(tri_ref[...], chosen.astype(tri_ref.dtype), preferred_element_type=F32) + cnt_ref[...]
    r1 = jnp.sum(jnp.where(sel1, before, 0.0), axis=1, keepdims=True)
    r2 = jnp.sum(jnp.where(sel2, before, 0.0), axis=1, keepdims=True)
    cnt_ref[...] += jnp.sum(chosen, axis=0, keepdims=True)
    lane = lax.broadcasted_iota(jnp.int32, logits.shape, 1)
    record = jnp.zeros_like(logits)
    for c, val in enumerate((i1, i2, p1, p2, r1, r2)):
        record = jnp.where(lane == c, val, record)
    route_ref[...] = record[:, :ROUTE_COLS]


def _moe_scatter_kernel(pos_ref, pad_ref, tail_ref, x_ref, xs_ref, zero_ref, sem, zsem):
    tm = x_ref.shape[0]
    tr = zero_ref.shape[0]

    @pl.when(pl.program_id(0) == 0)
    def _():
        zero_ref[...] = jnp.zeros_like(zero_ref)

        def pad_copy(first, r):
            return pltpu.make_async_copy(zero_ref.at[pl.ds(0, 1)], xs_ref.at[pl.ds(first + r, 1)], zsem)

        def tile_copy(t):
            rows = pl.ds(pl.multiple_of(t * tr, tr), tr)
            return pltpu.make_async_copy(zero_ref, xs_ref.at[rows], zsem)

        def each(copy_fn, method):
            for e in range(pad_ref.shape[0]):
                first, count = pad_ref[e, 0], pad_ref[e, 1]
                lax.fori_loop(0, count, lambda r, c: (getattr(copy_fn(first, r), method)(), c)[1], 0)

        each(pad_copy, "start")
        lax.fori_loop(tail_ref[0], tail_ref[1], lambda t, c: (tile_copy(t).start(), c)[1], 0)
        each(pad_copy, "wait")
        lax.fori_loop(tail_ref[0], tail_ref[1], lambda t, c: (tile_copy(t).wait(), c)[1], 0)

    def row_copy(r, k):
        slot = pos_ref[0, 0, k * tm + r]
        return pltpu.make_async_copy(x_ref.at[pl.ds(r, 1)], xs_ref.at[pl.ds(slot, 1)], sem)

    def start(r, carry):
        row_copy(r, 0).start()
        row_copy(r, 1).start()
        return carry

    def wait(r, carry):
        row_copy(r, 0).wait()
        row_copy(r, 1).wait()
        return carry

    lax.fori_loop(0, tm, start, 0, unroll=ROW_DMA_UNROLL)
    lax.fori_loop(0, tm, wait, 0, unroll=ROW_DMA_UNROLL)


def _gmm_kernel(te_ref, nu_ref, xs_ref, wg_ref, wu_ref, wd_ref, ys_ref, xb_ref, acc_ref):
    del te_ref
    j = pl.program_id(0)
    f = pl.program_id(1)

    @pl.when(jnp.logical_and(j >= nu_ref[0], f == 0))
    def _():
        ys_ref[...] = jnp.zeros_like(ys_ref)

    @pl.when(j < nu_ref[0])
    def _():
        @pl.when(f == 0)
        def _():
            xb_ref[...] = xs_ref[...].astype(BF16)
            acc_ref[...] = jnp.zeros_like(acc_ref)

        acc_ref[...] += _swiglu_partial(xb_ref[...], wg_ref[...], wu_ref[...], wd_ref[...])

        @pl.when(f == pl.num_programs(1) - 1)
        def _():
            ys_ref[...] = acc_ref[...]


def _moe_combine_kernel(pos_ref, route_ref, x_ref, lg_ref, lb_ref, ys_ref, o_ref, y1_ref, y2_ref, sem,
                        *, alpha):
    tm = x_ref.shape[0]

    def row_copy(r, k):
        slot = pos_ref[0, 0, k * tm + r]
        dst = (y1_ref, y2_ref)[k]
        return pltpu.make_async_copy(ys_ref.at[pl.ds(slot, 1)], dst.at[pl.ds(r, 1)], sem)

    def start(r, carry):
        row_copy(r, 0).start()
        row_copy(r, 1).start()
        return carry

    def wait(r, carry):
        row_copy(r, 0).wait()
        row_copy(r, 1).wait()
        return carry

    lax.fori_loop(0, tm, start, 0, unroll=ROW_DMA_UNROLL)
    lax.fori_loop(0, tm, wait, 0, unroll=ROW_DMA_UNROLL)
    route = route_ref[...]
    mixed = route[:, 2:3] * y1_ref[...] + route[:, 3:4] * y2_ref[...]
    o_ref[...] = _layer_norm(alpha * x_ref[...] + mixed, lg_ref[...], lb_ref[...])


def _moe(x, router, wg, wu, wd, ln_g, ln_b, alpha):
    t, d = x.shape
    n_experts, _, ff = wg.shape
    tm = _tile(t, MOE_TOKEN_TILE)
    tr = MOE_TILE_ROWS
    n_tok_tiles = t // tm

    router_p = jnp.pad(router.astype(F32), ((0, 0), (0, V7X_LANES - n_experts)))
    route, counts = pl.pallas_call(
        functools.partial(_router_kernel, n_experts=n_experts),
        grid=(n_tok_tiles,),
        in_specs=[pl.BlockSpec((tm, d), lambda i: (i, 0)),
                  pl.BlockSpec((d, V7X_LANES), lambda i: (0, 0))],
        out_specs=[pl.BlockSpec((tm, ROUTE_COLS), lambda i: (i, 0)),
                   pl.BlockSpec((1, V7X_LANES), lambda i: (0, 0))],
        out_shape=[jax.ShapeDtypeStruct((t, ROUTE_COLS), F32), jax.ShapeDtypeStruct((1, V7X_LANES), F32)],
        scratch_shapes=[pltpu.VMEM((tm, tm), BF16)],
        compiler_params=_cparams("arbitrary"),
        name="moe_router",
    )(x, router_p)

    cnt = counts[0, :n_experts].astype(jnp.int32)
    padded = (cnt + tr - 1) // tr * tr
    ends = jnp.cumsum(padded)
    starts = ends - padded
    expert_ids = jnp.arange(n_experts, dtype=jnp.int32)

    def slots(expert_col, rank_col):
        onehot = route[:, expert_col:expert_col + 1].astype(jnp.int32) == expert_ids[None]
        return jnp.sum(jnp.where(onehot, starts[None], 0), axis=1) + route[:, rank_col].astype(jnp.int32)

    pos = jnp.concatenate([slots(0, 4).reshape(n_tok_tiles, 1, tm), slots(1, 5).reshape(n_tok_tiles, 1, tm)],
                          axis=2)
    n_tiles = (TOP_K * t) // tr + n_experts
    n_used = (ends[-1] // tr).reshape(1)
    tile_start = jnp.arange(n_tiles, dtype=jnp.int32) * tr
    tile_expert = jnp.sum((tile_start[:, None] >= ends[None]).astype(jnp.int32), axis=1)
    last_expert = jnp.sum((ends[-1] - tr >= ends).astype(jnp.int32))
    tile_expert = jnp.where(tile_start < ends[-1], tile_expert, last_expert)

    pos_spec = pl.BlockSpec((1, 1, 2 * tm), lambda i: (i, 0, 0), memory_space=pltpu.SMEM)
    smem_spec = pl.BlockSpec(memory_space=pltpu.SMEM)
    any_spec = pl.BlockSpec(memory_space=pl.ANY)
    pad_rows = jnp.stack([starts + cnt, padded - cnt], axis=1)
    tail_tiles = jnp.concatenate([n_used, jnp.full((1,), n_tiles, jnp.int32)])
    xs = pl.pallas_call(
        _moe_scatter_kernel,
        grid=(n_tok_tiles,),
        in_specs=[pos_spec, smem_spec, smem_spec, pl.BlockSpec((tm, d), lambda i: (i, 0))],
        out_specs=any_spec,
        out_shape=jax.ShapeDtypeStruct((n_tiles * tr, d), F32),
        scratch_shapes=[pltpu.VMEM((tr, d), F32), pltpu.SemaphoreType.DMA(()), pltpu.SemaphoreType.DMA(())],
        compiler_params=_cparams("arbitrary"),
        name="moe_scatter",
    )(pos, pad_rows, tail_tiles, x)

    tf = ff // 2 if (ff // 2) % V7X_LANES == 0 else ff
    nf = ff // tf
    used = lambda j, nu: jnp.minimum(j, nu[0] - 1)
    fcol = lambda j, f, nu: jnp.where(j < nu[0], f, nf - 1)
    ys = pl.pallas_call(
        _gmm_kernel,
        grid_spec=pltpu.PrefetchScalarGridSpec(
            num_scalar_prefetch=2,
            grid=(n_tiles, nf),
            in_specs=[pl.BlockSpec((tr, d), lambda j, f, te, nu: (used(j, nu), 0)),
                      pl.BlockSpec((None, d, tf), lambda j, f, te, nu: (te[j], 0, fcol(j, f, nu))),
                      pl.BlockSpec((None, d, tf), lambda j, f, te, nu: (te[j], 0, fcol(j, f, nu))),
                      pl.BlockSpec((None, tf, d), lambda j, f, te, nu: (te[j], fcol(j, f, nu), 0))],
            out_specs=pl.BlockSpec((tr, d), lambda j, f, te, nu: (j, 0)),
            scratch_shapes=[pltpu.VMEM((tr, d), BF16), pltpu.VMEM((tr, d), F32)]),
        out_shape=jax.ShapeDtypeStruct((n_tiles * tr, d), F32),
        compiler_params=_cparams("arbitrary", "arbitrary"),
        name="moe_experts",
    )(tile_expert, n_used, xs, wg, wu, wd)

    vec = pl.BlockSpec((1, d), lambda i: (0, 0))
    return pl.pallas_call(
        functools.partial(_moe_combine_kernel, alpha=alpha),
        grid=(n_tok_tiles,),
        in_specs=[pos_spec, pl.BlockSpec((tm, ROUTE_COLS), lambda i: (i, 0)),
                  pl.BlockSpec((tm, d), lambda i: (i, 0)), vec, vec, any_spec],
        out_specs=pl.BlockSpec((tm, d), lambda i: (i, 0)),
        out_shape=jax.ShapeDtypeStruct((t, d), F32),
        scratch_shapes=[pltpu.VMEM((tm, d), F32), pltpu.VMEM((tm, d), F32), pltpu.SemaphoreType.DMA(())],
        compiler_params=_cparams("arbitrary"),
        name="moe_combine",
    )(pos, route, x, ln_g.reshape(1, d), ln_b.reshape(1, d), ys)


def _run_trunk(x3, p, tables):
    bsz, seq, d = x3.shape
    depth = p['w_in'].shape[0]
    alpha = (2 * depth) ** 0.25
    att_w = ATT_HEADS * HEAD_W
    width = p['sc_conv_w'].shape[2]
    gate_w = 3 * d // 2
    assert att_w % width == 0 and (3 * att_w + 6 * width) % gate_w == 0, "column blocks must align"
    hy_col = att_w // width
    d_in = p['w_in'].shape[2]
    col_scale = jnp.asarray(np.where(np.arange(d_in) < att_w, LOG2E * ATT_HEAD_DIM ** -0.5, 1.0)
                            .astype(np.float32)[None])
    x = x3.reshape(bsz * seq, d)
    for i in range(depth):
        proj = _proj(x, p['w_in'][i], col_scale)
        proj3 = proj.reshape(bsz, seq, proj.shape[1])
        y_a = _attention(proj3, p['att_lambda'][i], p['att_subln_g'][i], i)
        u, x0, y_c = _convs(proj3, p['hy_short_w'][i], p['sc_conv_w'][i], hy_col=hy_col, sc_col=hy_col + 1)
        spec = _hyena_filter_spectrum(seq, tables, p['hf_w_in'][i], p['hf_b_in'][i], p['hf_w_mid'][i],
                                      p['hf_b_mid'][i], p['hf_freq'][i], p['hf_w_out'][i])
        y_h = _long_conv(u, x0, tables, spec, p['hy_bias'][i])
        flat = lambda a: a.reshape(bsz * seq, a.shape[2])
        j = i // 2
        x = _merge(flat(y_a), flat(y_h), flat(y_c), proj, x, p['w_branch'][i], p['w_out'][i],
                   p['ln1_g'][i], p['ln1_b'][i], alpha, gate_col=(3 * att_w + 6 * width) // gate_w)
        if i % 2 == 0:
            x = _ffn(x, p['ffn_w_gate'][j], p['ffn_w_up'][j], p['ffn_w_down'][j],
                     p['ln2_g'][i], p['ln2_b'][i], alpha)
        else:
            x = _moe(x, p['router_w'][j], p['moe_w_gate'][j], p['moe_w_up'][j], p['moe_w_down'][j],
                     p['ln2_g'][i], p['ln2_b'][i], alpha)
    return x.reshape(bsz, seq, d)


def kernel(x_prompt, x_sample, w_in, att_lambda, att_subln_g, hy_short_w, hf_w_in, hf_b_in,
           hf_w_mid, hf_b_mid, hf_freq, hf_w_out, hy_bias, sc_conv_w, w_branch, w_out,
           ln1_g, ln1_b, ln2_g, ln2_b, ffn_w_gate, ffn_w_up, ffn_w_down, router_w,
           moe_w_gate, moe_w_up, moe_w_down):
    bf = lambda a: a.astype(BF16)
    p = dict(w_in=bf(w_in), att_lambda=att_lambda, att_subln_g=att_subln_g, hy_short_w=hy_short_w,
             hf_w_in=hf_w_in, hf_b_in=hf_b_in, hf_w_mid=hf_w_mid, hf_b_mid=hf_b_mid,
             hf_freq=hf_freq, hf_w_out=hf_w_out, hy_bias=hy_bias, sc_conv_w=sc_conv_w,
             w_branch=bf(w_branch), w_out=bf(w_out), ln1_g=ln1_g, ln1_b=ln1_b, ln2_g=ln2_g, ln2_b=ln2_b,
             ffn_w_gate=bf(ffn_w_gate), ffn_w_up=bf(ffn_w_up), ffn_w_down=bf(ffn_w_down),
             router_w=router_w, moe_w_gate=bf(moe_w_gate), moe_w_up=bf(moe_w_up),
             moe_w_down=bf(moe_w_down))
    tables = {}
    outs = []
    for x3 in (x_prompt, x_sample):
        seq = x3.shape[1]
        if seq not in tables:
            tables[seq] = _dft_tables(seq)
        outs.append(_run_trunk(x3, p, tables[seq]))
    return tuple(outs)
```

```python
import functools
import math

import numpy as np
import jax
import jax.numpy as jnp
from jax import lax
from jax.experimental import pallas as pl
from jax.experimental.pallas import tpu as pltpu

F32 = jnp.float32
BF16 = jnp.bfloat16

ATT_HEADS = 4
ATT_HEAD_DIM = 64
HEAD_W = 2 * ATT_HEAD_DIM
HY_BANDS = 16
HY_TARGET = 1e-2
HY_FAST_PCT = 0.3
HY_SLOW_PCT = 1.5
TOP_K = 2
LN_EPS = 1e-5
RMS_EPS = 1e-5
LOG2E = math.log2(math.e)
ATT_MIN_Q_TILE = 256
ATT_TILE_BUDGET = 1 << 32
V7X_LANES = 128
V7X_BF16_SUBLANES = 16
V7X_VMEM_LIMIT = 56 * 1024 * 1024


def _cparams(*sem):
    return pltpu.CompilerParams(dimension_semantics=sem, vmem_limit_bytes=V7X_VMEM_LIMIT)


def _tile(n, pref):
    t = min(n, pref)
    while n % t:
        t //= 2
    return t


def _layer_norm(v, g, b):
    mu = jnp.mean(v, axis=-1, keepdims=True)
    d = v - mu
    var = jnp.mean(d * d, axis=-1, keepdims=True)
    return d * lax.rsqrt(var + LN_EPS) * g + b


def _proj_kernel(x_ref, w_ref, cs_ref, o_ref, xb_ref):
    @pl.when(pl.program_id(1) == 0)
    def _():
        xb_ref[...] = x_ref[...].astype(BF16)

    acc = jnp.dot(xb_ref[...], w_ref[...], preferred_element_type=F32)
    o_ref[...] = (acc * cs_ref[...]).astype(o_ref.dtype)


def _proj(x, w, col_scale):
    t, d = x.shape
    n = w.shape[1]
    tm = _tile(t, 1024)
    tn = _tile(n, 2560)
    return pl.pallas_call(
        _proj_kernel,
        grid=(t // tm, n // tn),
        in_specs=[pl.BlockSpec((tm, d), lambda i, j: (i, 0)),
                  pl.BlockSpec((d, tn), lambda i, j: (0, j)),
                  pl.BlockSpec((1, tn), lambda i, j: (0, j))],
        out_specs=pl.BlockSpec((tm, tn), lambda i, j: (i, j)),
        out_shape=jax.ShapeDtypeStruct((t, n), BF16),
        scratch_shapes=[pltpu.VMEM((tm, d), BF16)],
        compiler_params=_cparams("parallel", "arbitrary"),
        name="proj",
    )(x, w, col_scale)


def _attn_kernel(slopes_ref, lam_ref, g_ref, q_ref, k_ref, v_ref, o_ref, bias_ref, *, lam_init, tq):
    h = pl.program_id(0)
    b = pl.program_id(1)
    qi = pl.program_id(2)
    seq = k_ref.shape[1]

    @pl.when(jnp.logical_and(b == 0, qi == 0))
    def _():
        r = lax.broadcasted_iota(jnp.int32, bias_ref.shape, 0)
        j = lax.broadcasted_iota(jnp.int32, bias_ref.shape, 1)
        bias_ref[...] = (-LOG2E * slopes_ref[h]) * jnp.abs(r - j + (seq - tq)).astype(F32)

    start = pl.multiple_of((pl.num_programs(2) - 1 - qi) * tq, tq)
    lf = lam_ref[...].astype(F32)
    lam_full = (jnp.exp(jnp.sum(lf[0:1] * lf[1:2], axis=1, keepdims=True))
                - jnp.exp(jnp.sum(lf[2:3] * lf[3:4], axis=1, keepdims=True)) + lam_init)
    q = q_ref[0]
    k = k_ref[0]
    v = v_ref[0]
    bias = bias_ref[:, pl.ds(start, seq)]
    lane = lax.broadcasted_iota(jnp.int32, q.shape, 1)
    zero = jnp.zeros_like(q)

    def softmax_pv(qm):
        s = lax.dot_general(qm, k, (((1,), (1,)), ((), ())), preferred_element_type=F32)
        m = jnp.max(s + bias, axis=1, keepdims=True)
        e = jnp.exp2(s + (bias - m))
        l = jnp.sum(e, axis=1, keepdims=True)
        return jnp.dot(e.astype(BF16), v, preferred_element_type=F32) * (1.0 / l)

    o = (softmax_pv(jnp.where(lane < ATT_HEAD_DIM, q, zero))
         - lam_full * softmax_pv(jnp.where(lane >= ATT_HEAD_DIM, q, zero)))
    o = o * lax.rsqrt(jnp.mean(o * o, axis=-1, keepdims=True) + RMS_EPS) * g_ref[...].astype(F32)
    o_ref[0] = (o * (1.0 - lam_init)).astype(o_ref.dtype)


def _attention(proj3, lam, subln_g, layer):
    bsz, seq, _ = proj3.shape
    tq = _tile(seq, max(ATT_MIN_Q_TILE, ATT_TILE_BUDGET // (seq * seq)))
    lam_init = 0.8 - 0.6 * math.exp(-0.3 * layer)
    slopes = jnp.asarray(2.0 ** (-8.0 * np.arange(1, ATT_HEADS + 1, dtype=np.float32) / ATT_HEADS))
    kern = functools.partial(_attn_kernel, lam_init=lam_init, tq=tq)
    return pl.pallas_call(
        kern,
        grid=(ATT_HEADS, bsz, seq // tq),
        in_specs=[pl.BlockSpec(memory_space=pltpu.SMEM),
                  pl.BlockSpec((4, ATT_HEAD_DIM), lambda h, b, i: (0, 0)),
                  pl.BlockSpec((1, HEAD_W), lambda h, b, i: (0, 0)),
                  pl.BlockSpec((1, tq, HEAD_W), lambda h, b, i: (b, i, h)),
                  pl.BlockSpec((1, seq, HEAD_W), lambda h, b, i: (b, 0, ATT_HEADS + h)),
                  pl.BlockSpec((1, seq, HEAD_W), lambda h, b, i: (b, 0, 2 * ATT_HEADS + h))],
        out_specs=pl.BlockSpec((1, tq, HEAD_W), lambda h, b, i: (b, i, h)),
        out_shape=jax.ShapeDtypeStruct((bsz, seq, ATT_HEADS * HEAD_W), BF16),
        scratch_shapes=[pltpu.VMEM((tq, 2 * seq - tq), F32)],
        compiler_params=_cparams("arbitrary", "arbitrary", "arbitrary"),
        name="attn",
    )(slopes, lam, subln_g.reshape(1, HEAD_W), proj3, proj3, proj3)


def _conv3(x, prev_row, next_row, w):
    rows = x.shape[0]
    row = lax.broadcasted_iota(jnp.int32, (rows, 1), 0)
    x_prev = jnp.where(row == 0, prev_row, pltpu.roll(x, 1, axis=0))
    x_next = jnp.where(row == rows - 1, next_row, pltpu.roll(x, rows - 1, axis=0))
    return w[0:1] * x_prev + w[1:2] * x + w[2:3] * x_next


def _conv_kernel(hy_ref, hyp_ref, hyn_ref, sc_ref, scp_ref, scn_ref, hw_ref, sw_ref,
                 u_ref, x0_ref, yc_ref):
    i = pl.program_id(1)
    last = pl.num_programs(1) - 1
    width = u_ref.shape[2]
    has_prev = jnp.where(i > 0, 1.0, 0.0)
    has_next = jnp.where(i < last, 1.0, 0.0)
    hsub = hyp_ref.shape[1]

    hy = hy_ref[0].astype(F32)
    hy_prev = hyp_ref[0].astype(F32)[hsub - 1:hsub] * has_prev
    hy_next = hyn_ref[0].astype(F32)[0:1] * has_next
    hyc = _conv3(hy, hy_prev, hy_next, hw_ref[...].astype(F32))
    x0_ref[0] = hyc[:, :width].astype(x0_ref.dtype)
    u_ref[0] = (hyc[:, 2 * width:] * hyc[:, width:2 * width]).astype(u_ref.dtype)

    def gated(blk):
        return blk[:, width:2 * width] * blk[:, 2 * width:]

    sc = sc_ref[0].astype(F32)
    cx_prev = gated(scp_ref[0].astype(F32)[hsub - 1:hsub]) * has_prev
    cx_next = gated(scn_ref[0].astype(F32)[0:1]) * has_next
    conv = _conv3(gated(sc), cx_prev, cx_next, sw_ref[...].astype(F32))
    yc_ref[0] = (sc[:, :width] * conv).astype(yc_ref.dtype)


def _convs(proj3, hy_w, sc_w, hy_col, sc_col):
    bsz, seq, _ = proj3.shape
    width = sc_w.shape[1]
    tl = _tile(seq, 1024)
    hs = V7X_BF16_SUBLANES
    nh = tl // hs
    nblk = seq // hs

    def main(col):
        return pl.BlockSpec((1, tl, 3 * width), lambda b, i: (b, i, col))

    def prev(col):
        return pl.BlockSpec((1, hs, 3 * width), lambda b, i: (b, jnp.maximum(i * nh - 1, 0), col))

    def nxt(col):
        return pl.BlockSpec((1, hs, 3 * width), lambda b, i: (b, jnp.minimum((i + 1) * nh, nblk - 1), col))

    out = jax.ShapeDtypeStruct((bsz, seq, width), BF16)
    ospec = pl.BlockSpec((1, tl, width), lambda b, i: (b, i, 0))
    return pl.pallas_call(
        _conv_kernel,
        grid=(bsz, seq // tl),
        in_specs=[main(hy_col), prev(hy_col), nxt(hy_col), main(sc_col), prev(sc_col), nxt(sc_col),
                  pl.BlockSpec((3, 3 * width), lambda b, i: (0, 0)),
                  pl.BlockSpec((3, width), lambda b, i: (0, 0))],
        out_specs=[ospec, ospec, ospec],
        out_shape=[out, out, out],
        compiler_params=_cparams("parallel", "arbitrary"),
        name="conv",
    )(proj3, proj3, proj3, proj3, proj3, proj3, hy_w, sc_w)


def _tab_kernel(c_ref, s_ref, st_ref, cb_ref, sb_ref, *, seq):
    tr = c_ref.shape[0]
    i = pl.program_id(0)
    dr = lax.broadcasted_iota(jnp.int32, (tr, seq), 0)
    c = lax.broadcasted_iota(jnp.int32, (tr, seq), 1)
    r = i * tr + dr

    def angle(prod):
        return (prod & (2 * seq - 1)).astype(F32) * (math.pi / seq)

    @pl.when(i == 0)
    def _():
        base = angle(dr * c)
        cb_ref[...] = jnp.cos(base)
        sb_ref[...] = jnp.sin(base)

    col = lax.broadcasted_iota(jnp.int32, (1, seq), 1)
    lead = angle((i * tr) * col)
    c0, s0 = jnp.cos(lead), jnp.sin(lead)
    cb, sb = cb_ref[...], sb_ref[...]
    cosv = c0 * cb - s0 * sb
    nsin = -(s0 * cb + c0 * sb)
    c_ref[...] = cosv.astype(c_ref.dtype)
    s_ref[...] = jnp.where(r == 0, jnp.where((c & 1) == 0, 1.0, -1.0), nsin).astype(s_ref.dtype)
    st_ref[...] = jnp.where(c == 0, jnp.where((r & 1) == 0, 1.0, -1.0), nsin).astype(st_ref.dtype)


def _dft_tables(seq):
    assert seq & (seq - 1) == 0, "sequence length must be a power of two"
    tr = _tile(seq, 256)
    spec = pl.BlockSpec((tr, seq), lambda i: (i, 0))
    out = jax.ShapeDtypeStruct((seq, seq), BF16)
    return pl.pallas_call(
        functools.partial(_tab_kernel, seq=seq),
        grid=(seq // tr,),
        out_specs=[spec, spec, spec],
        out_shape=[out, out, out],
        scratch_shapes=[pltpu.VMEM((tr, seq), F32), pltpu.VMEM((tr, seq), F32)],
        compiler_params=_cparams("arbitrary"),
        name="dft_tables",
    )()


def _filt_kernel(z_ref, win_ref, bin_ref, wmid_ref, bmid_ref, fr_ref, wout_ref, dl_ref,
                 hs_ref, l1_ref, *, seq):
    i = pl.program_id(0)
    tl = z_ref.shape[0]
    width = dl_ref.shape[1]
    hi = lax.Precision.HIGHEST
    fr = fr_ref[...]
    h = jnp.sin(fr * (jnp.dot(z_ref[...], win_ref[...], precision=hi, preferred_element_type=F32)
                      + bin_ref[...]))
    for j in range(wmid_ref.shape[0]):
        h = jnp.sin(fr * (jnp.dot(h, wmid_ref[j], precision=hi, preferred_element_type=F32)
                          + bmid_ref[j]))
    ho = jnp.dot(h, wout_ref[...], precision=hi, preferred_element_type=F32)
    pos = i * tl + lax.broadcasted_iota(jnp.int32, (tl, 1), 0)
    t = pos.astype(F32) / float(seq - 1)
    window = jnp.exp(-t * dl_ref[...])
    h_fwd = ho[:, :width] * window
    h_bwd = jnp.where(pos == 0, 0.0, ho[:, width:] * window)
    hs_ref[:, :width] = h_fwd + h_bwd
    hs_ref[:, width:] = h_fwd - h_bwd

    @pl.when(i == 0)
    def _():
        l1_ref[...] = jnp.zeros_like(l1_ref)

    l1_ref[...] += jnp.sum(jnp.abs(h_fwd) + jnp.abs(h_bwd), axis=0, keepdims=True)


def _fspec_kernel(c_ref, s_ref, hs_ref, l1_ref, a_ref, b_ref, d_ref, *, seq):
    i = pl.program_id(0)
    tk = c_ref.shape[0]
    width = l1_ref.shape[1]
    hs = hs_ref[...].astype(BF16)
    xc = jnp.dot(c_ref[...], hs[:, :width], preferred_element_type=F32)
    xs = jnp.dot(s_ref[...], hs, preferred_element_type=F32)
    k = i * tk + lax.broadcasted_iota(jnp.int32, (tk, 1), 0)
    inv_l1 = 1.0 / l1_ref[...]
    wk = jnp.where(k == 0, 1.0, 2.0) / float(2 * seq)
    a = xc * wk * inv_l1
    a_ref[...] = a
    b_ref[...] = jnp.where(k == 0, 0.0, xs[:, width:] * wk * inv_l1)
    d_ref[...] = jnp.where(k == 0, xs[:, :width] * wk * inv_l1, a)


def _hyena_filter_spectrum(seq, tabs, w_in, b_in, w_mid, b_mid, freq, w_out):
    c_tab, s_tab, _ = tabs
    emb, fw = w_in.shape
    width = w_out.shape[1] // 2
    n = np.arange(seq, dtype=np.float64)[:, None]
    f = np.linspace(1e-4, HY_BANDS - 1, HY_BANDS)[None]
    w = 2.0 * math.pi * n / seq
    z = np.concatenate([n / (seq - 1), np.cos(f * w), -np.sin(f * w)], -1)
    z = np.pad(z, ((0, 0), (0, V7X_LANES - emb))).astype(np.float32)
    w_in_p = jnp.pad(w_in.astype(F32), ((0, V7X_LANES - emb), (0, 0)))
    deltas = np.abs(np.linspace(math.log(HY_TARGET) / HY_SLOW_PCT, math.log(HY_TARGET) / HY_FAST_PCT,
                                width)).astype(np.float32)[None]
    tl = _tile(seq, 512)
    full = lambda *shape: pl.BlockSpec(shape, lambda i: (0,) * len(shape))
    hs, l1 = pl.pallas_call(
        functools.partial(_filt_kernel, seq=seq),
        grid=(seq // tl,),
        in_specs=[pl.BlockSpec((tl, V7X_LANES), lambda i: (i, 0)),
                  full(V7X_LANES, fw), full(1, fw), full(*w_mid.shape), full(w_mid.shape[0], 1, fw),
                  full(1, fw), full(fw, 2 * width), full(1, width)],
        out_specs=[pl.BlockSpec((tl, 2 * width), lambda i: (i, 0)), full(1, width)],
        out_shape=[jax.ShapeDtypeStruct((seq, 2 * width), F32), jax.ShapeDtypeStruct((1, width), F32)],
        compiler_params=_cparams("arbitrary"),
        name="hyena_filter",
    )(jnp.asarray(z), w_in_p, b_in.reshape(1, fw).astype(F32), w_mid.astype(F32),
      b_mid.reshape(w_mid.shape[0], 1, fw).astype(F32), freq.reshape(1, fw).astype(F32),
      w_out.astype(F32), jnp.asarray(deltas))

    tk = _tile(seq, 256)
    ospec = pl.BlockSpec((tk, width), lambda i: (i, 0))
    out = jax.ShapeDtypeStruct((seq, width), F32)
    return pl.pallas_call(
        functools.partial(_fspec_kernel, seq=seq),
        grid=(seq // tk,),
        in_specs=[pl.BlockSpec((tk, seq), lambda i: (i, 0)), pl.BlockSpec((tk, seq), lambda i: (i, 0)),
                  full(seq, 2 * width), full(1, width)],
        out_specs=[ospec, ospec, ospec],
        out_shape=[out, out, out],
        compiler_params=_cparams("parallel"),
        name="filter_spectrum",
    )(c_tab, s_tab, hs, l1)


def _dft_fwd_kernel(c_ref, s_ref, u_ref, a_ref, b_ref, d_ref, yre_ref, yim_ref):
    u = u_ref[0]
    x_re = jnp.dot(c_ref[...], u, preferred_element_type=F32)
    x_im = jnp.dot(s_ref[...], u, preferred_element_type=F32)
    b = b_ref[...]
    yre_ref[0] = (x_re * a_ref[...] - x_im * b).astype(yre_ref.dtype)
    yim_ref[0] = (x_re * b + x_im * d_ref[...]).astype(yim_ref.dtype)


def _dft_inv_kernel(c_ref, st_ref, yre_ref, yim_ref, u_ref, x0_ref, bias_ref, o_ref):
    y = (jnp.dot(c_ref[...], yre_ref[0], preferred_element_type=F32)
         + jnp.dot(st_ref[...], yim_ref[0], preferred_element_type=F32))
    y = y + u_ref[0].astype(F32) * bias_ref[...]
    o_ref[0] = (x0_ref[0].astype(F32) * y).astype(o_ref.dtype)


def _long_conv(u, x0, tabs, spec, bias):
    c_tab, s_tab, st_tab = tabs
    a, b, d = spec
    bsz, seq, width = u.shape
    tk = _tile(seq, 512)
    tab = pl.BlockSpec((tk, seq), lambda j, bb: (j, 0))
    whole = pl.BlockSpec((1, seq, width), lambda j, bb: (bb, 0, 0))
    rows = pl.BlockSpec((1, tk, width), lambda j, bb: (bb, j, 0))
    coef = pl.BlockSpec((tk, width), lambda j, bb: (j, 0))
    spec_out = jax.ShapeDtypeStruct((bsz, seq, width), BF16)
    y_re, y_im = pl.pallas_call(
        _dft_fwd_kernel,
        grid=(seq // tk, bsz),
        in_specs=[tab, tab, whole, coef, coef, coef],
        out_specs=[rows, rows],
        out_shape=[spec_out, spec_out],
        compiler_params=_cparams("parallel", "arbitrary"),
        name="dft_fwd",
    )(c_tab, s_tab, u, a, b, d)
    return pl.pallas_call(
        _dft_inv_kernel,
        grid=(seq // tk, bsz),
        in_specs=[tab, tab, whole, whole, rows, rows, pl.BlockSpec((1, width), lambda j, bb: (0, 0))],
        out_specs=rows,
        out_shape=spec_out,
        compiler_params=_cparams("parallel", "arbitrary"),
        name="dft_inv",
    )(c_tab, st_tab, y_re, y_im, u, x0, bias.reshape(1, width).astype(F32))


def _merge_kernel(ya_ref, yh_ref, yc_ref, g0_ref, g1_ref, x_ref, wb_ref, wo_ref, lg_ref, lb_ref,
                  o_ref, m_ref, *, alpha):
    d = o_ref.shape[1]
    half = d // 2
    g0 = g0_ref[...]
    g1 = g1_ref[...]
    gates = ((g0[:, :half], g0[:, d:], g1[:, half:d]),
             (g0[:, half:d], g1[:, :half], g1[:, d:]))
    branches = (ya_ref[...], yh_ref[...], yc_ref[...])
    for c in range(2):
        acc = None
        for j in range(3):
            y = jnp.dot(branches[j], wb_ref[j, :, c * half:(c + 1) * half], preferred_element_type=F32)
            term = jax.nn.sigmoid(gates[c][j].astype(F32)) * y
            acc = term if acc is None else acc + term
        m_ref[:, c * half:(c + 1) * half] = acc.astype(m_ref.dtype)
    t = jnp.dot(m_ref[...], wo_ref[...], preferred_element_type=F32)
    o_ref[...] = _layer_norm(alpha * x_ref[...] + t, lg_ref[...], lb_ref[...])


def _merge(y_a, y_h, y_c, proj, x, wb, wo, ln_g, ln_b, alpha, gate_col):
    t, d = x.shape
    bw = y_a.shape[1]
    tm = _tile(t, 512)
    row = lambda w: pl.BlockSpec((tm, w), lambda i: (i, 0))
    fixed = lambda *shape: pl.BlockSpec(shape, lambda i: (0,) * len(shape))
    gw = 3 * d // 2
    return pl.pallas_call(
        functools.partial(_merge_kernel, alpha=alpha),
        grid=(t // tm,),
        in_specs=[row(bw), row(bw), row(bw),
                  pl.BlockSpec((tm, gw), lambda i: (i, gate_col)),
                  pl.BlockSpec((tm, gw), lambda i: (i, gate_col + 1)),
                  row(d), fixed(*wb.shape), fixed(*wo.shape), fixed(1, d), fixed(1, d)],
        out_specs=row(d),
        out_shape=jax.ShapeDtypeStruct((t, d), F32),
        scratch_shapes=[pltpu.VMEM((tm, d), BF16)],
        compiler_params=_cparams("parallel"),
        name="merge",
    )(y_a, y_h, y_c, proj, proj, x, wb, wo, ln_g.reshape(1, d), ln_b.reshape(1, d))


def _swiglu_partial(xb, wg, wu, wd):
    g = jnp.dot(xb, wg, preferred_element_type=F32)
    u = jnp.dot(xb, wu, preferred_element_type=F32)
    return jnp.dot((jax.nn.silu(g) * u).astype(BF16), wd, preferred_element_type=F32)


def _ffn_kernel(x_ref, wg_ref, wu_ref, wd_ref, lg_ref, lb_ref, o_ref, xb_ref, acc_ref, *, alpha):
    f = pl.program_id(1)

    @pl.when(f == 0)
    def _():
        xb_ref[...] = x_ref[...].astype(BF16)
        acc_ref[...] = jnp.zeros_like(acc_ref)

    acc_ref[...] += _swiglu_partial(xb_ref[...], wg_ref[...], wu_ref[...], wd_ref[...])

    @pl.when(f == pl.num_programs(1) - 1)
    def _():
        o_ref[...] = _layer_norm(alpha * x_ref[...] + acc_ref[...], lg_ref[...], lb_ref[...])


def _ffn(x, wg, wu, wd, ln_g, ln_b, alpha):
    t, d = x.shape
    ff = wg.shape[1]
    tm = _tile(t, 1024)
    tf = ff // 2 if (ff // 2) % V7X_LANES == 0 else ff
    vec = pl.BlockSpec((1, d), lambda i, f: (0, 0))
    return pl.pallas_call(
        functools.partial(_ffn_kernel, alpha=alpha),
        grid=(t // tm, ff // tf),
        in_specs=[pl.BlockSpec((tm, d), lambda i, f: (i, 0)),
                  pl.BlockSpec((d, tf), lambda i, f: (0, f)),
                  pl.BlockSpec((d, tf), lambda i, f: (0, f)),
                  pl.BlockSpec((tf, d), lambda i, f: (f, 0)),
                  vec, vec],
        out_specs=pl.BlockSpec((tm, d), lambda i, f: (i, 0)),
        out_shape=jax.ShapeDtypeStruct((t, d), F32),
        scratch_shapes=[pltpu.VMEM((tm, d), BF16), pltpu.VMEM((tm, d), F32)],
        compiler_params=_cparams("parallel", "arbitrary"),
        name="ffn",
    )(x, wg, wu, wd, ln_g.reshape(1, d), ln_b.reshape(1, d))


def _top2(logits, n_experts):
    lane = lax.broadcasted_iota(jnp.int32, logits.shape, 1).astype(F32)
    neg = jnp.float32(-jnp.inf)
    sentinel = jnp.float32(n_experts)
    lg = jnp.where(lane < sentinel, logits, neg)
    m1 = jnp.max(lg, axis=1, keepdims=True)
    i1 = jnp.min(jnp.where(lg == m1, lane, sentinel), axis=1, keepdims=True)
    rest = jnp.where(lane == i1, neg, lg)
    m2 = jnp.max(rest, axis=1, keepdims=True)
    i2 = jnp.min(jnp.where(rest == m2, lane, sentinel), axis=1, keepdims=True)
    e2 = jnp.exp(m2 - m1)
    p1 = 1.0 / (1.0 + e2)
    p2 = e2 / (1.0 + e2)
    return lane == i1, lane == i2, i1, i2, p1, p2


ROUTE_COLS = 8
MOE_TILE_ROWS = 512
MOE_TOKEN_TILE = 512
ROW_DMA_UNROLL = 8


def _router_kernel(x_ref, r_ref, route_ref, cnt_ref, tri_ref, *, n_experts):
    i = pl.program_id(0)
    x = x_ref[...]
    tm = x.shape[0]

    @pl.when(i == 0)
    def _():
        cnt_ref[...] = jnp.zeros_like(cnt_ref)
        row = lax.broadcasted_iota(jnp.int32, (tm, tm), 0)
        col = lax.broadcasted_iota(jnp.int32, (tm, tm), 1)
        tri_ref[...] = jnp.where(col < row, 1.0, 0.0).astype(tri_ref.dtype)

    w = r_ref[...]
    x_hi, w_hi = x.astype(BF16), w.astype(BF16)
    x_lo, w_lo = (x - x_hi.astype(F32)).astype(BF16), (w - w_hi.astype(F32)).astype(BF16)
    logits = (jnp.dot(x_hi, w_hi, preferred_element_type=F32) + jnp.dot(x_lo, w_hi, preferred_element_type=F32)
              + jnp.dot(x_hi, w_lo, preferred_element_type=F32))
    sel1, sel2, i1, i2, p1, p2 = _top2(logits, n_experts)
    chosen = jnp.where(sel1, 1.0, jnp.where(sel2, 1.0, 0.0))
    before = jnp.dot(tri_ref[...], chosen.astype(tri_ref.dtype), preferred_element_type=F32) + cnt_ref[...]
    r1 = jnp.sum(jnp.where(sel1, before, 0.0), axis=1, keepdims=True)
    r2 = jnp.sum(jnp.where(sel2, before, 0.0), axis=1, keepdims=True)
    cnt_ref[...] += jnp.sum(chosen, axis=0, keepdims=True)
    lane = lax.broadcasted_iota(jnp.int32, logits.shape, 1)
    record = jnp.zeros_like(logits)
    for c, val in enumerate((i1, i2, p1, p2, r1, r2)):
        record = jnp.where(lane == c, val, record)
    route_ref[...] = record[:, :ROUTE_COLS]


def _moe_scatter_kernel(pos_ref, pad_ref, tail_ref, x_ref, xs_ref, zero_ref, sem, zsem):
    tm = x_ref.shape[0]
    tr = zero_ref.shape[0]

    @pl.when(pl.program_id(0) == 0)
    def _():
        zero_ref[...] = jnp.zeros_like(zero_ref)

        def pad_copy(first, r):
            return pltpu.make_async_copy(zero_ref.at[pl.ds(0, 1)], xs_ref.at[pl.ds(first + r, 1)], zsem)

        def tile_copy(t):
            rows = pl.ds(pl.multiple_of(t * tr, tr), tr)
            return pltpu.make_async_copy(zero_ref, xs_ref.at[rows], zsem)

        def each(copy_fn, method):
            for e in range(pad_ref.shape[0]):
                first, count = pad_ref[e, 0], pad_ref[e, 1]
                lax.fori_loop(0, count, lambda r, c: (getattr(copy_fn(first, r), method)(), c)[1], 0)

        each(pad_copy, "start")
        lax.fori_loop(tail_ref[0], tail_ref[1], lambda t, c: (tile_copy(t).start(), c)[1], 0)
        each(pad_copy, "wait")
        lax.fori_loop(tail_ref[0], tail_ref[1], lambda t, c: (tile_copy(t).wait(), c)[1], 0)

    def row_copy(r, k):
        slot = pos_ref[0, 0, k * tm + r]
        return pltpu.make_async_copy(x_ref.at[pl.ds(r, 1)], xs_ref.at[pl.ds(slot, 1)], sem)

    def start(r, carry):
        row_copy(r, 0).start(priority=0)
        row_copy(r, 1).start(priority=1)
        return carry

    def wait(r, carry):
        row_copy(r, 0).wait()
        row_copy(r, 1).wait()
        return carry

    lax.fori_loop(0, tm, start, 0, unroll=ROW_DMA_UNROLL)
    lax.fori_loop(0, tm, wait, 0, unroll=ROW_DMA_UNROLL)


def _gmm_kernel(te_ref, nu_ref, xs_ref, wg_ref, wu_ref, wd_ref, ys_ref, xb_ref, acc_ref):
    del te_ref
    j = pl.program_id(0)
    f = pl.program_id(1)

    @pl.when(jnp.logical_and(j >= nu_ref[0], f == 0))
    def _():
        ys_ref[...] = jnp.zeros_like(ys_ref)

    @pl.when(j < nu_ref[0])
    def _():
        @pl.when(f == 0)
        def _():
            xb_ref[...] = xs_ref[...].astype(BF16)
            acc_ref[...] = jnp.zeros_like(acc_ref)

        acc_ref[...] += _swiglu_partial(xb_ref[...], wg_ref[...], wu_ref[...], wd_ref[...])

        @pl.when(f == pl.num_programs(1) - 1)
        def _():
            ys_ref[...] = acc_ref[...]


def _moe_combine_kernel(pos_ref, route_ref, x_ref, lg_ref, lb_ref, ys_ref, o_ref, y1_ref, y2_ref, sem,
                        *, alpha):
    tm = x_ref.shape[0]

    def row_copy(r, k):
        slot = pos_ref[0, 0, k * tm + r]
        dst = (y1_ref, y2_ref)[k]
        return pltpu.make_async_copy(ys_ref.at[pl.ds(slot, 1)], dst.at[pl.ds(r, 1)], sem)

    def start(r, carry):
        row_copy(r, 0).start(priority=0)
        row_copy(r, 1).start(priority=1)
        return carry

    def wait(r, carry):
        row_copy(r, 0).wait()
        row_copy(r, 1).wait()
        return carry

    lax.fori_loop(0, tm, start, 0, unroll=ROW_DMA_UNROLL)
    lax.fori_loop(0, tm, wait, 0, unroll=ROW_DMA_UNROLL)
    route = route_ref[...]
    mixed = route[:, 2:3] * y1_ref[...] + route[:, 3:4] * y2_ref[...]
    o_ref[...] = _layer_norm(alpha * x_ref[...] + mixed, lg_ref[...], lb_ref[...])


def _moe(x, router, wg, wu, wd, ln_g, ln_b, alpha):
    t, d = x.shape
    n_experts, _, ff = wg.shape
    tm = _tile(t, MOE_TOKEN_TILE)
    tr = MOE_TILE_ROWS
    n_tok_tiles = t // tm

    router_p = jnp.pad(router.astype(F32), ((0, 0), (0, V7X_LANES - n_experts)))
    route, counts = pl.pallas_call(
        functools.partial(_router_kernel, n_experts=n_experts),
        grid=(n_tok_tiles,),
        in_specs=[pl.BlockSpec((tm, d), lambda i: (i, 0)),
                  pl.BlockSpec((d, V7X_LANES), lambda i: (0, 0))],
        out_specs=[pl.BlockSpec((tm, ROUTE_COLS), lambda i: (i, 0)),
                   pl.BlockSpec((1, V7X_LANES), lambda i: (0, 0))],
        out_shape=[jax.ShapeDtypeStruct((t, ROUTE_COLS), F32), jax.ShapeDtypeStruct((1, V7X_LANES), F32)],
        scratch_shapes=[pltpu.VMEM((tm, tm), BF16)],
        compiler_params=_cparams("arbitrary"),
        name="moe_router",
    )(x, router_p)

    cnt = counts[0, :n_experts].astype(jnp.int32)
    padded = (cnt + tr - 1) // tr * tr
    ends = jnp.cumsum(padded)
    starts = ends - padded
    expert_ids = jnp.arange(n_experts, dtype=jnp.int32)

    def slots(expert_col, rank_col):
        onehot = route[:, expert_col:expert_col + 1].astype(jnp.int32) == expert_ids[None]
        return jnp.sum(jnp.where(onehot, starts[None], 0), axis=1) + route[:, rank_col].astype(jnp.int32)

    pos = jnp.concatenate([slots(0, 4).reshape(n_tok_tiles, 1, tm), slots(1, 5).reshape(n_tok_tiles, 1, tm)],
                          axis=2)
    n_tiles = (TOP_K * t) // tr + n_experts
    n_used = (ends[-1] // tr).reshape(1)
    tile_start = jnp.arange(n_tiles, dtype=jnp.int32) * tr
    tile_expert = jnp.sum((tile_start[:, None] >= ends[None]).astype(jnp.int32), axis=1)
    last_expert = jnp.sum((ends[-1] - tr >= ends).astype(jnp.int32))
    tile_expert = jnp.where(tile_start < ends[-1], tile_expert, last_expert)

    pos_spec = pl.BlockSpec((1, 1, 2 * tm), lambda i: (i, 0, 0), memory_space=pltpu.SMEM)
    smem_spec = pl.BlockSpec(memory_space=pltpu.SMEM)
    any_spec = pl.BlockSpec(memory_space=pl.ANY)
    pad_rows = jnp.stack([starts + cnt, padded - cnt], axis=1)
    tail_tiles = jnp.concatenate([n_used, jnp.full((1,), n_tiles, jnp.int32)])
    xs = pl.pallas_call(
        _moe_scatter_kernel,
        grid=(n_tok_tiles,),
        in_specs=[pos_spec, smem_spec, smem_spec, pl.BlockSpec((tm, d), lambda i: (i, 0))],
        out_specs=any_spec,
        out_shape=jax.ShapeDtypeStruct((n_tiles * tr, d), F32),
        scratch_shapes=[pltpu.VMEM((tr, d), F32), pltpu.SemaphoreType.DMA(()), pltpu.SemaphoreType.DMA(())],
        compiler_params=_cparams("arbitrary"),
        name="moe_scatter",
    )(pos, pad_rows, tail_tiles, x)

    tf = ff // 2 if (ff // 2) % V7X_LANES == 0 else ff
    nf = ff // tf
    used = lambda j, nu: jnp.minimum(j, nu[0] - 1)
    fcol = lambda j, f, nu: jnp.where(j < nu[0], f, nf - 1)
    ys = pl.pallas_call(
        _gmm_kernel,
        grid_spec=pltpu.PrefetchScalarGridSpec(
            num_scalar_prefetch=2,
            grid=(n_tiles, nf),
            in_specs=[pl.BlockSpec((tr, d), lambda j, f, te, nu: (used(j, nu), 0)),
                      pl.BlockSpec((None, d, tf), lambda j, f, te, nu: (te[j], 0, fcol(j, f, nu))),
                      pl.BlockSpec((None, d, tf), lambda j, f, te, nu: (te[j], 0, fcol(j, f, nu))),
                      pl.BlockSpec((None, tf, d), lambda j, f, te, nu: (te[j], fcol(j, f, nu), 0))],
            out_specs=pl.BlockSpec((tr, d), lambda j, f, te, nu: (j, 0)),
            scratch_shapes=[pltpu.VMEM((tr, d), BF16), pltpu.VMEM((tr, d), F32)]),
        out_shape=jax.ShapeDtypeStruct((n_tiles * tr, d), F32),
        compiler_params=_cparams("arbitrary", "arbitrary"),
        name="moe_experts",
    )(tile_expert, n_used, xs, wg, wu, wd)

    vec = pl.BlockSpec((1, d), lambda i: (0, 0))
    return pl.pallas_call(
        functools.partial(_moe_combine_kernel, alpha=alpha),
        grid=(n_tok_tiles,),
        in_specs=[pos_spec, pl.BlockSpec((tm, ROUTE_COLS), lambda i: (i, 0)),
                  pl.BlockSpec((tm, d), lambda i: (i, 0)), vec, vec, any_spec],
        out_specs=pl.BlockSpec((tm, d), lambda i: (i, 0)),
        out_shape=jax.ShapeDtypeStruct((t, d), F32),
        scratch_shapes=[pltpu.VMEM((tm, d), F32), pltpu.VMEM((tm, d), F32), pltpu.SemaphoreType.DMA(())],
        compiler_params=_cparams("arbitrary"),
        name="moe_combine",
    )(pos, route, x, ln_g.reshape(1, d), ln_b.reshape(1, d), ys)


def _run_trunk(x3, p, tables):
    bsz, seq, d = x3.shape
    depth = p['w_in'].shape[0]
    alpha = (2 * depth) ** 0.25
    att_w = ATT_HEADS * HEAD_W
    width = p['sc_conv_w'].shape[2]
    gate_w = 3 * d // 2
    assert att_w % width == 0 and (3 * att_w + 6 * width) % gate_w == 0, "column blocks must align"
    hy_col = att_w // width
    d_in = p['w_in'].shape[2]
    col_scale = jnp.asarray(np.where(np.arange(d_in) < att_w, LOG2E * ATT_HEAD_DIM ** -0.5, 1.0)
                            .astype(np.float32)[None])
    x = x3.reshape(bsz * seq, d)
    for i in range(depth):
        proj = _proj(x, p['w_in'][i], col_scale)
        proj3 = proj.reshape(bsz, seq, proj.shape[1])
        y_a = _attention(proj3, p['att_lambda'][i], p['att_subln_g'][i], i)
        u, x0, y_c = _convs(proj3, p['hy_short_w'][i], p['sc_conv_w'][i], hy_col=hy_col, sc_col=hy_col + 1)
        spec = _hyena_filter_spectrum(seq, tables, p['hf_w_in'][i], p['hf_b_in'][i], p['hf_w_mid'][i],
                                      p['hf_b_mid'][i], p['hf_freq'][i], p['hf_w_out'][i])
        y_h = _long_conv(u, x0, tables, spec, p['hy_bias'][i])
        flat = lambda a: a.reshape(bsz * seq, a.shape[2])
        j = i // 2
        x = _merge(flat(y_a), flat(y_h), flat(y_c), proj, x, p['w_branch'][i], p['w_out'][i],
                   p['ln1_g'][i], p['ln1_b'][i], alpha, gate_col=(3 * att_w + 6 * width) // gate_w)
        if i % 2 == 0:
            x = _ffn(x, p['ffn_w_gate'][j], p['ffn_w_up'][j], p['ffn_w_down'][j],
                     p['ln2_g'][i], p['ln2_b'][i], alpha)
        else:
            x = _moe(x, p['router_w'][j], p['moe_w_gate'][j], p['moe_w_up'][j], p['moe_w_down'][j],
                     p['ln2_g'][i], p['ln2_b'][i], alpha)
    return x.reshape(bsz, seq, d)


def kernel(x_prompt, x_sample, w_in, att_lambda, att_subln_g, hy_short_w, hf_w_in, hf_b_in,
           hf_w_mid, hf_b_mid, hf_freq, hf_w_out, hy_bias, sc_conv_w, w_branch, w_out,
           ln1_g, ln1_b, ln2_g, ln2_b, ffn_w_gate, ffn_w_up, ffn_w_down, router_w,
           moe_w_gate, moe_w_up, moe_w_down):
    bf = lambda a: a.astype(BF16)
    p = dict(w_in=bf(w_in), att_lambda=att_lambda, att_subln_g=att_subln_g, hy_short_w=hy_short_w,
             hf_w_in=hf_w_in, hf_b_in=hf_b_in, hf_w_mid=hf_w_mid, hf_b_mid=hf_b_mid,
             hf_freq=hf_freq, hf_w_out=hf_w_out, hy_bias=hy_bias, sc_conv_w=sc_conv_w,
             w_branch=bf(w_branch), w_out=bf(w_out), ln1_g=ln1_g, ln1_b=ln1_b, ln2_g=ln2_g, ln2_b=ln2_b,
             ffn_w_gate=bf(ffn_w_gate), ffn_w_up=bf(ffn_w_up), ffn_w_down=bf(ffn_w_down),
             router_w=router_w, moe_w_gate=bf(moe_w_gate), moe_w_up=bf(moe_w_up),
             moe_w_down=bf(moe_w_down))
    tables = {}
    outs = []
    for x3 in (x_prompt, x_sample):
        seq = x3.shape[1]
        if seq not in tables:
            tables[seq] = _dft_tables(seq)
        outs.append(_run_trunk(x3, p, tables[seq]))
    return tuple(outs)
```
